```python
import math
import jax, jax.numpy as jnp
from jax import lax
import numpy as np

D_MODEL = 1024
BATCH = 4
SEQ = 4096
DEPTH = 4
DEC_BATCH = 32
DEC_SEQ = 8
PAST_LEN = 8192
PAGE_SIZE = 128

ML_HEADS = 4
ML_DK = D_MODEL // 16
ML_DV = D_MODEL // 8
GLA_HEADS = 4
GLA_DK = D_MODEL // 16
GLA_DV = D_MODEL // 8
GLA_RANK = 16
GLA_TAU = 16.0
SB_HEADS = 8
SB_DH = D_MODEL // 16
BRANCH_W = D_MODEL // 2
N_BRANCH = 3
D_FF = 11 * D_MODEL // 4
CONV_W = 3
CHUNK = 64
Q_BLOCK = 128
EPS = 1e-6
NEG_BIG = -1e30

IN_WIDTHS = (
    ML_HEADS * ML_DK, ML_HEADS * ML_DK, ML_HEADS * ML_DV, ML_HEADS, ML_HEADS, ML_HEADS * ML_DV,
    GLA_HEADS * GLA_DK, GLA_HEADS * GLA_DK, GLA_HEADS * GLA_DV, GLA_RANK, GLA_HEADS * GLA_DV,
    SB_HEADS * SB_DH, SB_HEADS * SB_DH, SB_HEADS * SB_DH,
    N_BRANCH * D_MODEL,
)
D_IN = sum(IN_WIDTHS)

kernel_name = "hybrid_mlstm_gla_stickbreak_decoder_step"


def rmsnorm(x, w):
    xf = x.astype(jnp.float32)
    y = xf * lax.rsqrt(jnp.mean(xf * xf, axis=-1, keepdims=True) + EPS)
    return y.astype(x.dtype) * w


def head_rmsnorm(h, w):
    H, d = h.shape[-2:]
    return rmsnorm(h, w.reshape(H, d)).reshape(h.shape[:-2] + (H * d,))


def _to_chunks(t, L):
    t = jnp.moveaxis(t.astype(jnp.float32), 2, 1)
    nc = t.shape[2] // L
    t = t.reshape(t.shape[:2] + (nc, L) + t.shape[3:])
    return jnp.moveaxis(t, 2, 0)


def _from_chunks(h):
    h = jnp.moveaxis(h, 0, 2)
    B, H, nc, L, d = h.shape
    return jnp.moveaxis(h.reshape(B, H, nc * L, d), 1, 2)


def mlstm_chunk(state, inp):
    C, n, m = state
    q, k, v, ig, lf = inp
    L = q.shape[2]
    causal = jnp.tril(jnp.ones((L, L), dtype=bool))
    b = jnp.cumsum(lf, axis=-1)
    dmat = jnp.where(causal, b[..., :, None] - b[..., None, :] + ig[..., None, :], -jnp.inf)
    inter = b + m[..., None]
    m_t = jnp.maximum(inter, jnp.max(dmat, axis=-1))
    w = jnp.exp(dmat - m_t[..., None])
    s = jnp.einsum('bhtd,bhsd->bhts', q, k) * w
    e_inter = jnp.exp(inter - m_t)
    num = jnp.einsum('bhts,bhsv->bhtv', s, v) + e_inter[..., None] * jnp.einsum('bhtd,bhdv->bhtv', q, C)
    den = jnp.sum(s, axis=-1) + e_inter * jnp.einsum('bhtd,bhd->bht', q, n)
    h = num / jnp.maximum(jnp.abs(den), jnp.exp(-m_t))[..., None]
    m_new = m_t[..., -1]
    wk = jnp.exp(b[..., -1:] - b + ig - m_new[..., None])
    decay = jnp.exp(b[..., -1] + m - m_new)
    C_new = decay[..., None, None] * C + jnp.einsum('bhs,bhsd,bhsv->bhdv', wk, k, v)
    n_new = decay[..., None] * n + jnp.einsum('bhs,bhsd->bhd', wk, k)
    return (C_new, n_new, m_new), h


def run_mlstm(q, k, v, ig, lf, C0, n0, m0):
    S = q.shape[1]
    L = math.gcd(S, CHUNK)
    xs = tuple(_to_chunks(t, L) for t in (q, k, v, ig, lf))
    init = (C0.astype(jnp.float32), n0.astype(jnp.float32), m0.astype(jnp.float32))
    (C, n, m), h = lax.scan(mlstm_chunk, init, xs)
    return _from_chunks(h).astype(q.dtype), C, n, m


def gla_chunk(Sst, inp):
    q, k, v, la = inp
    L = q.shape[2]
    causal = jnp.tril(jnp.ones((L, L), dtype=bool))
    bc = jnp.cumsum(la, axis=2)
    diff = jnp.where(causal[:, :, None], bc[:, :, :, None, :] - bc[:, :, None, :, :], -jnp.inf)
    a = jnp.einsum('bhtd,bhsd,bhtsd->bhts', q, k, jnp.exp(diff))
    o = jnp.einsum('bhts,bhsv->bhtv', a, v) + jnp.einsum('bhtd,bhdv->bhtv', q * jnp.exp(bc), Sst)
    last = bc[:, :, -1]
    S_new = jnp.exp(last)[..., None] * Sst + jnp.einsum('bhsd,bhsv->bhdv', k * jnp.exp(last[:, :, None, :] - bc), v)
    return S_new, o


def run_gla(q, k, v, la, S0):
    S = q.shape[1]
    L = math.gcd(S, CHUNK)
    xs = tuple(_to_chunks(t, L) for t in (q, k, v, la))
    Sf, o = lax.scan(gla_chunk, S0.astype(jnp.float32), xs)
    return _from_chunks(o).astype(q.dtype), Sf


def stick_breaking(q, k, v, sb_bias, q_pos0):
    B, Sq, H, d = q.shape
    Sk = k.shape[1]
    qb = math.gcd(Sq, Q_BLOCK)
    nb = Sq // qb
    qf = q.astype(jnp.float32) * (d ** -0.5)
    qblocks = qf.reshape(B, nb, qb, H, d).transpose(1, 0, 3, 2, 4)
    kf = k.astype(jnp.float32)
    vf = v.astype(jnp.float32)
    bias = sb_bias.astype(jnp.float32)[None, :, None, None]
    key_pos = jnp.arange(Sk)

    def block(args):
        qblk, start = args
        z = jnp.einsum('bhtd,bshd->bhts', qblk, kf) + bias
        qpos = q_pos0 + start + jnp.arange(qb)
        mask = key_pos[None, :] < qpos[:, None]
        log_beta = jax.nn.log_sigmoid(z)
        log_1mb = jnp.where(mask, log_beta - z, 0.0)
        rest = lax.cumsum(log_1mb, axis=3, reverse=True) - log_1mb
        a = jnp.where(mask, jnp.exp(log_beta + rest), 0.0)
        return jnp.einsum('bhts,bshd->bthd', a, vf)

    out = lax.map(block, (qblocks, jnp.arange(nb) * qb))
    return out.transpose(1, 0, 2, 3, 4).reshape(B, Sq, H, d).astype(q.dtype)


def token_mixers(h, w_in, ml_b_i, ml_b_f, ml_norm_w, gla_w_gate, gla_b_gate, gla_norm_w, sb_bias,
                 w_branch, w_out, ml_C0, ml_n0, ml_m0, gla_S0, past_k, past_v, q_pos0):
    B, S, _ = h.shape
    proj = h @ w_in
    offs = [int(o) for o in np.cumsum(IN_WIDTHS)[:-1]]
    (ml_q, ml_k, ml_v, ml_i, ml_f, ml_o, gla_q, gla_k, gla_v, gla_lr, gla_o,
     sb_q, sb_k, sb_v, gates) = jnp.split(proj, offs, axis=-1)
    q = ml_q.reshape(B, S, ML_HEADS, ML_DK) * (ML_DK ** -0.5)
    k = ml_k.reshape(B, S, ML_HEADS, ML_DK)
    v = ml_v.reshape(B, S, ML_HEADS, ML_DV)
    ig = ml_i.astype(jnp.float32) + ml_b_i
    lf = jax.nn.log_sigmoid(ml_f.astype(jnp.float32) + ml_b_f)
    h_ml, mC, mn, mm = run_mlstm(q, k, v, ig, lf, ml_C0, ml_n0, ml_m0)
    h_ml = head_rmsnorm(h_ml, ml_norm_w) * jax.nn.sigmoid(ml_o)
    q = gla_q.reshape(B, S, GLA_HEADS, GLA_DK) * (GLA_DK ** -0.5)
    k = gla_k.reshape(B, S, GLA_HEADS, GLA_DK)
    v = gla_v.reshape(B, S, GLA_HEADS, GLA_DV)
    la = (jax.nn.log_sigmoid((gla_lr @ gla_w_gate + gla_b_gate).astype(jnp.float32)) / GLA_TAU)
    la = la.reshape(B, S, GLA_HEADS, GLA_DK)
    h_gla, gS = run_gla(q, k, v, la, gla_S0)
    h_gla = head_rmsnorm(h_gla, gla_norm_w) * jax.nn.silu(gla_o)
    q = sb_q.reshape(B, S, SB_HEADS, SB_DH)
    k_new = sb_k.reshape(B, S, SB_HEADS, SB_DH)
    v_new = sb_v.reshape(B, S, SB_HEADS, SB_DH)
    if past_k is None:
        keys, vals = k_new, v_new
    else:
        keys = jnp.concatenate([past_k.astype(k_new.dtype), k_new], axis=1)
        vals = jnp.concatenate([past_v.astype(v_new.dtype), v_new], axis=1)
    h_sb = stick_breaking(q, keys, vals, sb_bias, q_pos0).reshape(B, S, BRANCH_W)
    branches = jnp.stack([h_ml, h_gla, h_sb], axis=2)
    proj_b = jnp.einsum('bsnw,nwd->bsnd', branches, w_branch)
    g = jax.nn.sigmoid(gates.reshape(B, S, N_BRANCH, D_MODEL))
    merged = jnp.sum(g * proj_b, axis=2)
    return merged @ w_out, mC, mn, mm, gS, k_new, v_new


def conv_ffn(h, buf, w_up, conv_w, conv_b, w_down):
    S = h.shape[1]
    u, val = jnp.split(h @ w_up, 2, axis=-1)
    full = jnp.concatenate([buf.astype(u.dtype), u], axis=1)
    conv = conv_b
    for j in range(CONV_W):
        conv = conv + conv_w[j] * full[:, j:j + S]
    act = jax.nn.gelu(conv, approximate=False) * val
    return act @ w_down, full[:, S:]


def setup_inputs(seed: int = 0) -> dict:
    key = jax.random.key(seed)
    ks = iter(jax.random.split(key, 40))
    f32 = jnp.float32
    nrm = lambda shape, s=1.0: jax.random.normal(next(ks), shape, f32) * s
    n_pages = PAST_LEN // PAGE_SIZE
    n_used = DEC_BATCH * n_pages
    n_pool = n_used + (n_used + 3) // 4
    page_table = jax.random.permutation(next(ks), n_pool)[:n_used].reshape(DEC_BATCH, n_pages).astype(jnp.int32)
    return {
        "x_prompt": nrm((BATCH, SEQ, D_MODEL)),
        "x_sample": nrm((DEC_BATCH, DEC_SEQ, D_MODEL)),
        "cache_k": nrm((DEPTH, n_pool, PAGE_SIZE, SB_HEADS, SB_DH)),
        "cache_v": nrm((DEPTH, n_pool, PAGE_SIZE, SB_HEADS, SB_DH)),
        "state_mlstm_c": nrm((DEPTH, DEC_BATCH, ML_HEADS, ML_DK, ML_DV), 0.3),
        "state_mlstm_n": nrm((DEPTH, DEC_BATCH, ML_HEADS, ML_DK)),
        "state_mlstm_m": nrm((DEPTH, DEC_BATCH, ML_HEADS)),
        "state_gla": nrm((DEPTH, DEC_BATCH, GLA_HEADS, GLA_DK, GLA_DV), 0.3),
        "state_conv": nrm((DEPTH, DEC_BATCH, CONV_W - 1, D_FF)),
        "page_table": page_table,
        "c_prompt": nrm((BATCH, D_MODEL)),
        "c_sample": nrm((DEC_BATCH, D_MODEL)),
        "w_ada": nrm((DEPTH, D_MODEL, 6 * D_MODEL), 0.5 * D_MODEL ** -0.5),
        "b_ada": nrm((DEPTH, 6 * D_MODEL), 0.02),
        "norm1_w": 1.0 + nrm((DEPTH, D_MODEL), 0.05),
        "norm2_w": 1.0 + nrm((DEPTH, D_MODEL), 0.05),
        "w_in": nrm((DEPTH, D_MODEL, D_IN), D_MODEL ** -0.5),
        "ml_b_i": nrm((DEPTH, ML_HEADS), 0.1),
        "ml_b_f": jnp.linspace(3.0, 6.0, ML_HEADS, dtype=f32)[None, :] + nrm((DEPTH, ML_HEADS), 0.1),
        "ml_norm_w": 1.0 + nrm((DEPTH, ML_HEADS * ML_DV), 0.05),
        "gla_w_gate": nrm((DEPTH, GLA_RANK, GLA_HEADS * GLA_DK), GLA_RANK ** -0.5),
        "gla_b_gate": nrm((DEPTH, GLA_HEADS * GLA_DK), 0.1),
        "gla_norm_w": 1.0 + nrm((DEPTH, GLA_HEADS * GLA_DV), 0.05),
        "sb_bias": jnp.linspace(-6.0, -9.0, SB_HEADS, dtype=f32)[None, :] + nrm((DEPTH, SB_HEADS), 0.1),
        "w_branch": nrm((DEPTH, N_BRANCH, BRANCH_W, D_MODEL), BRANCH_W ** -0.5),
        "w_out": nrm((DEPTH, D_MODEL, D_MODEL), D_MODEL ** -0.5),
        "w_up": nrm((DEPTH, D_MODEL, 2 * D_FF), D_MODEL ** -0.5),
        "conv_w": nrm((DEPTH, CONV_W, D_FF), CONV_W ** -0.5),
        "conv_b": nrm((DEPTH, D_FF), 0.02),
        "w_down": nrm((DEPTH, D_FF, D_MODEL), D_FF ** -0.5),
        "final_norm_w": 1.0 + nrm((D_MODEL,), 0.05),
    }


def reference(x_prompt, x_sample, cache_k, cache_v, state_mlstm_c, state_mlstm_n, state_mlstm_m,
              state_gla, state_conv, page_table, c_prompt, c_sample, w_ada, b_ada, norm1_w, norm2_w,
              w_in, ml_b_i, ml_b_f, ml_norm_w, gla_w_gate, gla_b_gate, gla_norm_w, sb_bias, w_branch, w_out,
              w_up, conv_w, conv_b, w_down, final_norm_w):

    def run_group(x, c, ml_C0, ml_n0, ml_m0, gla_S0, conv0, paged, q_pos0):
        B = x.shape[0]
        ks_, vs_, mcs, mns, mms, gss, bufs = [], [], [], [], [], [], []
        for l in range(DEPTH):
            mod = (jax.nn.silu(c) @ w_ada[l] + b_ada[l]).reshape(B, 6, D_MODEL)
            sh1, sc1, g1, sh2, sc2, g2 = [mod[:, i, None, :] for i in range(6)]
            h = rmsnorm(x, norm1_w[l]) * (1.0 + sc1) + sh1
            if paged:
                past_k = cache_k[l][page_table].reshape(B, -1, SB_HEADS, SB_DH)
                past_v = cache_v[l][page_table].reshape(B, -1, SB_HEADS, SB_DH)
            else:
                past_k, past_v = None, None
            mix, mC, mn, mm, gS, kn, vn = token_mixers(
                h, w_in[l], ml_b_i[l], ml_b_f[l], ml_norm_w[l], gla_w_gate[l], gla_b_gate[l],
                gla_norm_w[l], sb_bias[l], w_branch[l], w_out[l], ml_C0[l], ml_n0[l], ml_m0[l], gla_S0[l],
                past_k, past_v, q_pos0)
            x = x + g1 * mix
            h = rmsnorm(x, norm2_w[l]) * (1.0 + sc2) + sh2
            ff, buf = conv_ffn(h, conv0[l], w_up[l], conv_w[l], conv_b[l], w_down[l])
            x = x + g2 * ff
            ks_.append(kn); vs_.append(vn); mcs.append(mC); mns.append(mn); mms.append(mm)
            gss.append(gS); bufs.append(buf)
        dt = x.dtype
        st = lambda lst: jnp.stack(lst).astype(dt)
        return (rmsnorm(x, final_norm_w), st(ks_), st(vs_), st(mcs), st(mns), st(mms), st(gss), st(bufs))

    Bp = x_prompt.shape[0]
    f32 = jnp.float32
    zC = jnp.zeros((DEPTH, Bp, ML_HEADS, ML_DK, ML_DV), f32)
    zn = jnp.zeros((DEPTH, Bp, ML_HEADS, ML_DK), f32)
    zm = jnp.full((DEPTH, Bp, ML_HEADS), NEG_BIG, f32)
    zS = jnp.zeros((DEPTH, Bp, GLA_HEADS, GLA_DK, GLA_DV), f32)
    zbuf = jnp.zeros((DEPTH, Bp, CONV_W - 1, D_FF), x_prompt.dtype)
    (y_prompt, k_prompt, v_prompt, mlstm_c_prompt, mlstm_n_prompt, mlstm_m_prompt,
     gla_prompt, conv_prompt) = run_group(x_prompt, c_prompt, zC, zn, zm, zS, zbuf, False, 0)
    (y_sample, k_sample, v_sample, mlstm_c_sample, mlstm_n_sample, mlstm_m_sample,
     gla_sample, conv_sample) = run_group(x_sample, c_sample, state_mlstm_c, state_mlstm_n, state_mlstm_m,
                                          state_gla, state_conv, True, PAST_LEN)
    return (y_prompt, y_sample, k_prompt, v_prompt, k_sample, v_sample,
            mlstm_c_prompt, mlstm_n_prompt, mlstm_m_prompt, mlstm_c_sample, mlstm_n_sample, mlstm_m_sample,
            gla_prompt, gla_sample, conv_prompt, conv_sample)
```

```python
import functools
import math

import numpy as np
import jax
import jax.numpy as jnp
from jax import lax
from jax.experimental import pallas as pl
from jax.experimental.pallas import tpu as pltpu

F32 = jnp.float32
BF16 = jnp.bfloat16
HIGHEST = lax.Precision.HIGHEST

D_MODEL = 1024
ML_HEADS, ML_DK, ML_DV = 4, 64, 128
GLA_HEADS, GLA_DK, GLA_DV = 4, 64, 128
GLA_RANK = 16
GLA_TAU = 16.0
SB_HEADS, SB_DH = 8, 64
BRANCH_W = 512
N_BRANCH = 3
D_FF = 2816
CONV_W = 3
EPS = 1e-6
NEG_BIG = -1e30
SCAN_CHUNK = 64

LANES = 128
SUBLANES = 8
VMEM_LIMIT = 56 * 1024 * 1024

N_MAIN = 7680
COL_GATES = 0
BLK_ML_V, BLK_ML_O, BLK_GLA_V, BLK_GLA_O, BLK_SB_Q, BLK_SB_K, BLK_SB_V = 6, 7, 8, 9, 10, 11, 12
BLK_ML_Q, BLK_ML_K, BLK_GLA_Q, BLK_GLA_K = 26, 27, 28, 29
SMALL_W = 128
SMALL_T_ROWS = 32


def _nt(a, b, precision=None):
    return lax.dot_general(a, b, (((1,), (1,)), ((), ())), preferred_element_type=F32, precision=precision)


def _tn(a, b):
    return lax.dot_general(a, b, (((0,), (0,)), ((), ())), preferred_element_type=F32)


def _mm(a, b, precision=None):
    return jnp.dot(a, b, preferred_element_type=F32, precision=precision)


def _log_sigmoid(x):
    return jnp.minimum(x, 0.0) - jnp.log1p(jnp.exp(-jnp.abs(x)))


def _softplus(x):
    return jnp.maximum(x, 0.0) + jnp.log1p(jnp.exp(-jnp.abs(x)))


def _params(sem):
    return pltpu.CompilerParams(dimension_semantics=sem, vmem_limit_bytes=VMEM_LIMIT)


def _mod_kernel(c_ref, w_ref, b_ref, o_ref):
    c = c_ref[...]
    s = c * jax.nn.sigmoid(c)
    o_ref[...] = _mm(s.astype(BF16), w_ref[...].astype(BF16)) + b_ref[...]


def _modulation(c_all, w_ada, b_ada):
    depth, d, n = w_ada.shape
    nb = c_all.shape[0]
    tn = 1536
    return pl.pallas_call(
        _mod_kernel,
        grid=(depth, n // tn),
        in_specs=[
            pl.BlockSpec((nb, d), lambda l, j: (0, 0)),
            pl.BlockSpec((None, d, tn), lambda l, j: (l, 0, j)),
            pl.BlockSpec((None, 1, tn), lambda l, j: (l, 0, j)),
        ],
        out_specs=pl.BlockSpec((None, nb, tn), lambda l, j: (l, 0, j)),
        out_shape=jax.ShapeDtypeStruct((depth, nb, n), F32),
        compiler_params=_params(("arbitrary", "arbitrary")),
        name="adaln_mod",
    )(c_all, w_ada, b_ada.reshape(depth, 1, n))


def _normmm_kernel(x_ref, nw_ref, sc_ref, sh_ref, w_ref, ws_ref, wt_ref, o_ref, os_ref, ot_ref, h_scr):
    @pl.when(pl.program_id(1) == 0)
    def _():
        x = x_ref[...]
        y = x * lax.rsqrt(jnp.mean(x * x, axis=-1, keepdims=True) + EPS)
        h = y * nw_ref[...] * (1.0 + sc_ref[...]) + sh_ref[...]
        hb = h.astype(BF16)
        h_scr[...] = hb
        os_ref[...] = _mm(hb, ws_ref[...])
        ot_ref[...] = _nt(wt_ref[...], hb)

    o_ref[...] = _mm(h_scr[...], w_ref[...])


def _norm_proj(x, nw, sc, sh, w_main, w_small, w_small_t, layer, *, tm, rows_per_group):
    t, d = x.shape
    n = w_main.shape[-1]
    tn = 1536
    r = sc.shape[1]
    tiles_per_group = rows_per_group // tm
    grp = lambda i, j: (i // tiles_per_group, 0, 0)
    return pl.pallas_call(
        _normmm_kernel,
        grid=(t // tm, n // tn),
        in_specs=[
            pl.BlockSpec((tm, d), lambda i, j: (i, 0)),
            pl.BlockSpec((None, 1, d), lambda i, j: (layer, 0, 0)),
            pl.BlockSpec((None, r, d), grp),
            pl.BlockSpec((None, r, d), grp),
            pl.BlockSpec((None, d, tn), lambda i, j: (layer, 0, j)),
            pl.BlockSpec((None, d, SMALL_W), lambda i, j: (layer, 0, 0)),
            pl.BlockSpec((None, SMALL_T_ROWS, d), lambda i, j: (layer, 0, 0)),
        ],
        out_specs=[
            pl.BlockSpec((tm, tn), lambda i, j: (i, j)),
            pl.BlockSpec((tm, SMALL_W), lambda i, j: (i, 0)),
            pl.BlockSpec((SMALL_T_ROWS, tm), lambda i, j: (0, i)),
        ],
        out_shape=[
            jax.ShapeDtypeStruct((t, n), F32),
            jax.ShapeDtypeStruct((t, SMALL_W), F32),
            jax.ShapeDtypeStruct((SMALL_T_ROWS, t), F32),
        ],
        scratch_shapes=[pltpu.VMEM((tm, d), BF16)],
        compiler_params=_params(("arbitrary", "arbitrary")),
        name="norm_in_proj",
    )(x, nw, sc, sh, w_main, w_small, w_small_t)


def _mlstm_kernel(q_ref, k_ref, v_ref, o_ref, sc_ref, st_ref, brow_ref, bcol_ref, nw_ref, c0_ref, n0_ref, m0_ref,
                  hn_ref, c_ref, n_ref, m_ref, *, L):
    @pl.when(pl.program_id(1) == 0)
    def _():
        c_ref[...] = c0_ref[...]
        n_ref[...] = n0_ref[...]
        m_ref[...] = m0_ref[...]

    row = lax.broadcasted_iota(jnp.int32, (L, L), 0)
    col = lax.broadcasted_iota(jnp.int32, (L, L), 1)
    causal = col <= row
    tril = causal.astype(F32)
    triu = (row <= col).astype(F32)
    small = sc_ref[...] + brow_ref[...]
    small_t = st_ref[...] + bcol_ref[...]
    bc_all = _mm(tril, _log_sigmoid(small), HIGHEST)
    br_all = _mm(_log_sigmoid(small_t), triu, HIGHEST)
    q = q_ref[...] * (ML_DK ** -0.5)
    k = k_ref[...]
    v = v_ref[...]
    og = o_ref[...]
    nw = nw_ref[...]
    for h in range(ML_HEADS):
        ks = slice(h * ML_DK, (h + 1) * ML_DK)
        vs = slice(h * ML_DV, (h + 1) * ML_DV)
        qh, kh, vh = q[:, ks], k[:, ks], v[:, vs]
        b_col = bc_all[:, ML_HEADS + h:ML_HEADS + h + 1]
        b_row = br_all[ML_HEADS + h:ML_HEADS + h + 1, :]
        ig_col = small[:, h:h + 1]
        ig_row = small_t[h:h + 1, :]
        m_prev = m_ref[h:h + 1, :]
        dmat = jnp.where(causal, b_col - b_row + ig_row, -jnp.inf)
        inter = b_col + m_prev
        m_t = jnp.maximum(inter, jnp.max(dmat, axis=-1, keepdims=True))
        w = jnp.exp(dmat - m_t)
        qb = qh.astype(BF16)
        vb = vh.astype(BF16)
        s = _nt(qb, kh.astype(BF16)) * w
        e_inter = jnp.exp(inter - m_t)
        c_st = c_ref[h]
        n_st = n_ref[h:h + 1, :]
        num = _mm(s.astype(BF16), vb) + e_inter * _mm(qb, c_st.astype(BF16))
        den = jnp.sum(s, axis=-1, keepdims=True) + e_inter * jnp.sum(qh * n_st, axis=-1, keepdims=True)
        hh = num / jnp.maximum(jnp.abs(den), jnp.exp(-m_t))
        m_new = m_t[L - 1:L, :]
        b_last = b_col[L - 1:L, :]
        wk = jnp.exp(b_last - b_col + ig_col - m_new)
        decay = jnp.exp(b_last + m_prev - m_new)
        kw = kh * wk
        c_ref[h] = decay * c_st + _tn(kw.astype(BF16), vb)
        n_ref[h:h + 1, :] = decay * n_st + jnp.sum(kw, axis=0, keepdims=True)
        m_ref[h:h + 1, :] = m_new
        y = hh * lax.rsqrt(jnp.mean(hh * hh, axis=-1, keepdims=True) + EPS) * nw[:, vs]
        hn_ref[:, vs] = (y * jax.nn.sigmoid(og[:, vs])).astype(hn_ref.dtype)


def _small_t_spec(L, nc, per_seq):
    if per_seq:
        return pl.BlockSpec((None, SMALL_T_ROWS, L), lambda b, c: (b, 0, 0))
    return pl.BlockSpec((None, SMALL_T_ROWS, L), lambda b, c: (0, 0, b * nc + c))


def _mlstm(main, small, small_t3, b_row, b_col, nw, c0, n0, m0, *, batch, seq, L, out_dtype, per_seq):
    nc = seq // L
    t = batch * seq
    tok = lambda blk: (lambda b, c: (b * nc + c, blk))
    st = lambda b, c: (b, 0, 0)
    kern = functools.partial(_mlstm_kernel, L=L)
    return pl.pallas_call(
        kern,
        grid=(batch, nc),
        in_specs=[
            pl.BlockSpec((L, 256), tok(BLK_ML_Q)),
            pl.BlockSpec((L, 256), tok(BLK_ML_K)),
            pl.BlockSpec((L, 512), tok(BLK_ML_V)),
            pl.BlockSpec((L, 512), tok(BLK_ML_O)),
            pl.BlockSpec((L, SMALL_W), tok(0)),
            _small_t_spec(L, nc, per_seq),
            pl.BlockSpec((1, SMALL_W), lambda b, c: (0, 0)),
            pl.BlockSpec((SMALL_T_ROWS, 1), lambda b, c: (0, 0)),
            pl.BlockSpec((1, 512), lambda b, c: (0, 0)),
            pl.BlockSpec((None, ML_HEADS, ML_DK, ML_DV), lambda b, c: (b, 0, 0, 0)),
            pl.BlockSpec((None, ML_HEADS, ML_DK), st),
            pl.BlockSpec((None, ML_HEADS, 1), st),
        ],
        out_specs=[
            pl.BlockSpec((L, 512), tok(0)),
            pl.BlockSpec((None, ML_HEADS, ML_DK, ML_DV), lambda b, c: (b, 0, 0, 0)),
            pl.BlockSpec((None, ML_HEADS, ML_DK), st),
            pl.BlockSpec((None, ML_HEADS, 1), st),
        ],
        out_shape=[
            jax.ShapeDtypeStruct((t, 512), out_dtype),
            jax.ShapeDtypeStruct((batch, ML_HEADS, ML_DK, ML_DV), F32),
            jax.ShapeDtypeStruct((batch, ML_HEADS, ML_DK), F32),
            jax.ShapeDtypeStruct((batch, ML_HEADS, 1), F32),
        ],
        compiler_params=_params(("arbitrary", "arbitrary")),
        name="mlstm_scan",
    )(main, main, main, main, small, small_t3, b_row, b_col, nw, c0, n0, m0)


def _gla_kernel(q_ref, k_ref, v_ref, o_ref, sc_ref, wg_ref, bg_ref, nw_ref, s0_ref, hn_ref, s_ref, *, L, LS):
    @pl.when(pl.program_id(1) == 0)
    def _():
        s_ref[...] = s0_ref[...]

    row = lax.broadcasted_iota(jnp.int32, (LS, LS), 0)
    col = lax.broadcasted_iota(jnp.int32, (LS, LS), 1)
    causal = col <= row
    tril = causal.astype(F32)
    la_all = _log_sigmoid(_mm(sc_ref[...].astype(BF16), wg_ref[...]) + bg_ref[...]) * (1.0 / GLA_TAU)
    nw = nw_ref[...]
    for c in range(L // LS):
        rs = slice(c * LS, (c + 1) * LS)
        bc = _mm(tril, la_all[rs, :], HIGHEST)
        ref_row = bc[LS // 2:LS // 2 + 1, :]
        last = bc[LS - 1:LS, :]
        q = q_ref[rs, :] * (GLA_DK ** -0.5)
        k = k_ref[rs, :]
        qe = (q * jnp.exp(bc - ref_row)).astype(BF16)
        ke = (k * jnp.exp(ref_row - bc)).astype(BF16)
        qs = (q * jnp.exp(bc)).astype(BF16)
        kl = (k * jnp.exp(last - bc)).astype(BF16)
        el = jnp.exp(last)
        v = v_ref[rs, :]
        og = o_ref[rs, :]
        for h in range(GLA_HEADS):
            ks = slice(h * GLA_DK, (h + 1) * GLA_DK)
            vs = slice(h * GLA_DV, (h + 1) * GLA_DV)
            vb = v[:, vs].astype(BF16)
            st = s_ref[h]
            a = jnp.where(causal, _nt(qe[:, ks], ke[:, ks]), 0.0)
            o = _mm(a.astype(BF16), vb) + _nt(qs[:, ks], st.astype(BF16))
            s_ref[h] = el[:, ks] * st + _tn(vb, kl[:, ks])
            y = o * lax.rsqrt(jnp.mean(o * o, axis=-1, keepdims=True) + EPS) * nw[:, vs]
            g = og[:, vs]
            hn_ref[rs, vs] = (y * (g * jax.nn.sigmoid(g))).astype(hn_ref.dtype)


def _gla(main, small, wg_pad, bg, nw, s0t, layer, *, batch, seq, L, out_dtype):
    nc = seq // L
    t = batch * seq
    ls = math.gcd(L, SCAN_CHUNK)
    tok = lambda blk: (lambda b, c: (b * nc + c, blk))
    kern = functools.partial(_gla_kernel, L=L, LS=ls)
    return pl.pallas_call(
        kern,
        grid=(batch, nc),
        in_specs=[
            pl.BlockSpec((L, 256), tok(BLK_GLA_Q)),
            pl.BlockSpec((L, 256), tok(BLK_GLA_K)),
            pl.BlockSpec((L, 512), tok(BLK_GLA_V)),
            pl.BlockSpec((L, 512), tok(BLK_GLA_O)),
            pl.BlockSpec((L, SMALL_W), tok(0)),
            pl.BlockSpec((None, SMALL_W, 256), lambda b, c: (layer, 0, 0)),
            pl.BlockSpec((None, 1, 256), lambda b, c: (layer, 0, 0)),
            pl.BlockSpec((None, 1, 512), lambda b, c: (layer, 0, 0)),
            pl.BlockSpec((None, GLA_HEADS, GLA_DV, GLA_DK), lambda b, c: (b, 0, 0, 0)),
        ],
        out_specs=[
            pl.BlockSpec((L, 512), tok(0)),
            pl.BlockSpec((None, GLA_HEADS, GLA_DV, GLA_DK), lambda b, c: (b, 0, 0, 0)),
        ],
        out_shape=[
            jax.ShapeDtypeStruct((t, 512), out_dtype),
            jax.ShapeDtypeStruct((batch, GLA_HEADS, GLA_DV, GLA_DK), F32),
        ],
        compiler_params=_params(("arbitrary", "arbitrary")),
        name="gla_scan",
    )(main, main, main, main, small, wg_pad, bg, nw, s0t)


def _sb_prompt_kernel(qi_ref, kj_ref, q_ref, k_ref, v_ref, bias_ref, o_ref, acc_ref, r_ref, *, TB):
    p = pl.program_id(1)
    i = qi_ref[p]
    j = kj_ref[p]
    diag = j == i

    @pl.when(diag)
    def _():
        acc_ref[...] = jnp.zeros_like(acc_ref)
        r_ref[...] = jnp.zeros_like(r_ref)

    row = lax.broadcasted_iota(jnp.int32, (TB, TB), 0)
    col = lax.broadcasted_iota(jnp.int32, (TB, TB), 1)
    suffix = (row >= col).astype(BF16)
    valid = jnp.logical_or(jnp.logical_not(diag), col < row)
    q = q_ref[...] * (SB_DH ** -0.5)
    k = k_ref[...]
    v = v_ref[...]
    bias = bias_ref[...]
    for h in range(SB_HEADS):
        hs = slice(h * SB_DH, (h + 1) * SB_DH)
        z = _nt(q[:, hs].astype(BF16), k[:, hs].astype(BF16)) + bias[:, h * SB_DH:h * SB_DH + 1]
        l1 = jnp.where(valid, -_softplus(z), 0.0)
        hi = l1.astype(BF16)
        lo = (l1 - hi.astype(F32)).astype(BF16)
        rest = _mm(hi, suffix) + _mm(lo, suffix)
        r_h = r_ref[:, h:h + 1]
        a = jnp.where(valid, jnp.exp(z + rest + r_h), 0.0)
        acc_ref[:, hs] += _mm(a.astype(BF16), v[:, hs].astype(BF16))
        r_ref[:, h:h + 1] = r_h + rest[:, 0:1]

    @pl.when(j == 0)
    def _():
        o_ref[...] = acc_ref[...].astype(o_ref.dtype)


def _sb_prompt(main, bias_row, *, batch, seq, tb):
    nq = seq // tb
    qi = np.concatenate([np.full(i + 1, i) for i in range(nq)]).astype(np.int32)
    kj = np.concatenate([np.arange(i, -1, -1) for i in range(nq)]).astype(np.int32)
    t = batch * seq
    kern = functools.partial(_sb_prompt_kernel, TB=tb)
    grid_spec = pltpu.PrefetchScalarGridSpec(
        num_scalar_prefetch=2,
        grid=(batch, len(qi)),
        in_specs=[
            pl.BlockSpec((tb, 512), lambda b, p, qi, kj: (b * nq + qi[p], BLK_SB_Q)),
            pl.BlockSpec((tb, 512), lambda b, p, qi, kj: (b * nq + kj[p], BLK_SB_K)),
            pl.BlockSpec((tb, 512), lambda b, p, qi, kj: (b * nq + kj[p], BLK_SB_V)),
            pl.BlockSpec((1, 512), lambda b, p, qi, kj: (0, 0)),
        ],
        out_specs=pl.BlockSpec((tb, 512), lambda b, p, qi, kj: (b * nq + qi[p], 0)),
        scratch_shapes=[pltpu.VMEM((tb, 512), F32), pltpu.VMEM((tb, LANES), F32)],
    )
    return pl.pallas_call(
        kern,
        grid_spec=grid_spec,
        out_shape=jax.ShapeDtypeStruct((t, 512), BF16),
        compiler_params=_params(("arbitrary", "arbitrary")),
        name="sb_prompt",
    )(jnp.asarray(qi), jnp.asarray(kj), main, main, main, bias_row)


def _sb_sample_kernel(pt_ref, q_ref, bias_ref, kn_ref, vn_ref, *rest, PP, PS, NQ):
    kp = rest[:PP]
    vp = rest[PP:2 * PP]
    o_ref, acc_ref, r_ref = rest[2 * PP:]
    s = pl.program_id(1)
    hq = SB_HEADS * NQ
    q = (q_ref[...] * (SB_DH ** -0.5)).astype(BF16)
    bias = bias_ref[...]
    row = lax.broadcasted_iota(jnp.int32, (PS, PS), 0)
    col = lax.broadcasted_iota(jnp.int32, (PS, PS), 1)
    suffix = (row >= col).astype(BF16)

    def block(get_k, get_v, valid):
        z = jnp.concatenate(
            [_nt(q[:, h * SB_DH:(h + 1) * SB_DH], get_k(h).astype(BF16)) for h in range(SB_HEADS)], axis=0) + bias
        l1 = -_softplus(z)
        if valid is not None:
            l1 = jnp.where(valid, l1, 0.0)
        hi = l1.astype(BF16)
        lo = (l1 - hi.astype(F32)).astype(BF16)
        rest_ = _mm(hi, suffix) + _mm(lo, suffix)
        r = r_ref[:, 0:1]
        a = jnp.exp(z + rest_ + r)
        if valid is not None:
            a = jnp.where(valid, a, 0.0)
        for h in range(SB_HEADS):
            rows = slice(h * NQ, (h + 1) * NQ)
            acc_ref[rows, :] += _mm(a[rows, :].astype(BF16), get_v(h).astype(BF16))
        r_ref[:, 0:1] = r + rest_[:, 0:1]

    @pl.when(s == 0)
    def _():
        acc_ref[...] = jnp.zeros_like(acc_ref)
        r_ref[...] = jnp.zeros_like(r_ref)
        key = lax.broadcasted_iota(jnp.int32, (hq, PS), 1)
        qpos = lax.broadcasted_iota(jnp.int32, (hq, PS), 0) % NQ
        block(lambda h: kn_ref[:, h * SB_DH:(h + 1) * SB_DH], lambda h: vn_ref[:, h * SB_DH:(h + 1) * SB_DH],
              key < qpos)

    for p in range(PP):
        block(lambda h, p=p: kp[p][:, h, :], lambda h, p=p: vp[p][:, h, :], None)

    @pl.when(s == pl.num_programs(1) - 1)
    def _():
        for h in range(SB_HEADS):
            o_ref[:, h * SB_DH:(h + 1) * SB_DH] = acc_ref[h * NQ:(h + 1) * NQ, :]


def _sb_sample(page_table, q, bias_col, k_new_pad, v_new_pad, cache_k, cache_v, layer, *, pages_per_step):
    bs, n_pages = page_table.shape
    ps = cache_k.shape[2]
    nq = q.shape[1]
    pp = pages_per_step
    steps = n_pages // pp

    def page_spec(r):
        def idx(b, s, pt):
            return (layer, pt[b * n_pages + (n_pages - 1 - (s * pp + r))], 0, 0, 0)
        return pl.BlockSpec((None, None, ps, SB_HEADS, SB_DH), idx)

    kern = functools.partial(_sb_sample_kernel, PP=pp, PS=ps, NQ=nq)
    grid_spec = pltpu.PrefetchScalarGridSpec(
        num_scalar_prefetch=1,
        grid=(bs, steps),
        in_specs=[
            pl.BlockSpec((None, nq, 512), lambda b, s, pt: (b, 0, 0)),
            pl.BlockSpec((SB_HEADS * nq, 1), lambda b, s, pt: (0, 0)),
            pl.BlockSpec((None, ps, 512), lambda b, s, pt: (b, 0, 0)),
            pl.BlockSpec((None, ps, 512), lambda b, s, pt: (b, 0, 0)),
        ] + [page_spec(r) for r in range(pp)] + [page_spec(r) for r in range(pp)],
        out_specs=pl.BlockSpec((None, nq, 512), lambda b, s, pt: (b, 0, 0)),
        scratch_shapes=[pltpu.VMEM((SB_HEADS * nq, SB_DH), F32), pltpu.VMEM((SB_HEADS * nq, LANES), F32)],
    )
    return pl.pallas_call(
        kern,
        grid_spec=grid_spec,
        out_shape=jax.ShapeDtypeStruct((bs, nq, 512), F32),
        compiler_params=_params(("arbitrary", "arbitrary")),
        name="sb_sample",
    )(page_table.reshape(-1), q, bias_col, k_new_pad, v_new_pad, *([cache_k] * pp), *([cache_v] * pp))


def _merge_kernel(hm_ref, hg_ref, hs_ref, g0_ref, g1_ref, g2_ref, x_ref, gate_ref, wb_ref, wo_ref, o_ref):
    merged = jax.nn.sigmoid(g0_ref[...]) * _mm(hm_ref[...].astype(BF16), wb_ref[0])
    merged += jax.nn.sigmoid(g1_ref[...]) * _mm(hg_ref[...].astype(BF16), wb_ref[1])
    merged += jax.nn.sigmoid(g2_ref[...]) * _mm(hs_ref[...].astype(BF16), wb_ref[2])
    mix = _mm(merged.astype(BF16), wo_ref[...])
    o_ref[...] = x_ref[...] + gate_ref[...] * mix


def _merge(hm, hg, hs, main, x, gate, w_branch, w_out, layer, *, tm, rows_per_group):
    t, d = x.shape
    r = gate.shape[1]
    tiles_per_group = rows_per_group // tm
    tok = lambda blk: (lambda i: (i, blk))
    return pl.pallas_call(
        _merge_kernel,
        grid=(t // tm,),
        in_specs=[
            pl.BlockSpec((tm, 512), tok(0)),
            pl.BlockSpec((tm, 512), tok(0)),
            pl.BlockSpec((tm, 512), tok(0)),
            pl.BlockSpec((tm, d), tok(0)),
            pl.BlockSpec((tm, d), tok(1)),
            pl.BlockSpec((tm, d), tok(2)),
            pl.BlockSpec((tm, d), tok(0)),
            pl.BlockSpec((None, r, d), lambda i: (i // tiles_per_group, 0, 0)),
            pl.BlockSpec((None, N_BRANCH, BRANCH_W, d), lambda i: (layer, 0, 0, 0)),
            pl.BlockSpec((None, d, d), lambda i: (layer, 0, 0)),
        ],
        out_specs=pl.BlockSpec((tm, d), tok(0)),
        out_shape=jax.ShapeDtypeStruct((t, d), F32),
        compiler_params=_params(("arbitrary",)),
        name="branch_merge",
    )(hm, hg, hs, main, main, main, x, gate, w_branch, w_out)


FF_CHUNK = 256


def _ffn_kernel(x_ref, nw_ref, sc_ref, sh_ref, gate_ref, wu_ref, cw_ref, cb_ref, wd_ref, p1_ref, p2_ref,
                o_ref, u_ref, carry_ref, acc_ref, *, TM, SEQ_IN_TILE):
    x = x_ref[...]
    y = x * lax.rsqrt(jnp.mean(x * x, axis=-1, keepdims=True) + EPS)
    hb = (y * nw_ref[...] * (1.0 + sc_ref[...]) + sh_ref[...]).astype(BF16)
    row = lax.broadcasted_iota(jnp.int32, (TM, FF_CHUNK), 0)
    if SEQ_IN_TILE is None:
        @pl.when(pl.program_id(1) == 0)
        def _():
            carry_ref[...] = p1_ref[...]
        pos = row
    else:
        pos = row % SEQ_IN_TILE
    acc_ref[...] = jnp.zeros_like(acc_ref)
    for c in range(D_FF // FF_CHUNK):
        cs = slice(c * FF_CHUNK, (c + 1) * FF_CHUNK)
        vs = slice(D_FF + c * FF_CHUNK, D_FF + (c + 1) * FF_CHUNK)
        u = _mm(hb, wu_ref[:, cs])
        val = _mm(hb, wu_ref[:, vs])
        if SEQ_IN_TILE is None:
            prev = carry_ref[:, cs]
            prev1 = prev[SUBLANES - 1:SUBLANES, :]
            prev2 = jnp.where(row == 0, prev[SUBLANES - 2:SUBLANES - 1, :], prev1)
            carry_ref[:, cs] = u[TM - SUBLANES:TM, :]
            u_ref[:, cs] = u[TM - SUBLANES:TM, :]
        else:
            prev1 = p1_ref[:, cs]
            prev2 = p2_ref[:, cs]
            u_ref[:, cs] = u
        u_m1 = jnp.where(pos >= 1, pltpu.roll(u, 1, axis=0), prev1)
        u_m2 = jnp.where(pos >= 2, pltpu.roll(u, 2, axis=0), prev2)
        cw = cw_ref[:, cs]
        conv = cb_ref[:, cs] + cw[0:1, :] * u_m2 + cw[1:2, :] * u_m1 + cw[2:3, :] * u
        act = 0.5 * conv * (1.0 + lax.erf(conv * (2.0 ** -0.5))) * val
        acc_ref[...] += _mm(act.astype(BF16), wd_ref[cs, :])
    o_ref[...] = x + gate_ref[...] * acc_ref[...]


def _ffn(x, nw, sc, sh, gate, w_up, conv_w, conv_b, w_down, p1, p2, layer, *, batch, seq, tm, seq_in_tile):
    t, d = x.shape
    r = sc.shape[1]
    per_tile = seq_in_tile is not None
    if per_tile:
        grid = (1, t // tm)
        rowmap = lambda b, i: (i, 0)
        grp = lambda b, i: (i, 0, 0)
        pspec = pl.BlockSpec((tm, D_FF), rowmap)
        uspec = pl.BlockSpec((tm, D_FF), rowmap)
        ushape = jax.ShapeDtypeStruct((t, D_FF), F32)
    else:
        nt = seq // tm
        grid = (batch, nt)
        rowmap = lambda b, i: (b * nt + i, 0)
        grp = lambda b, i: (b, 0, 0)
        pspec = pl.BlockSpec((None, SUBLANES, D_FF), lambda b, i: (b, 0, 0))
        uspec = pl.BlockSpec((None, SUBLANES, D_FF), lambda b, i: (b, 0, 0))
        ushape = jax.ShapeDtypeStruct((batch, SUBLANES, D_FF), F32)
    const3 = lambda b, i: (layer, 0, 0)
    once = pl.Buffered(1)
    kern = functools.partial(_ffn_kernel, TM=tm, SEQ_IN_TILE=seq_in_tile)
    return pl.pallas_call(
        kern,
        grid=grid,
        in_specs=[
            pl.BlockSpec((tm, d), rowmap),
            pl.BlockSpec((None, 1, d), const3),
            pl.BlockSpec((None, r, d), grp),
            pl.BlockSpec((None, r, d), grp),
            pl.BlockSpec((None, r, d), grp),
            pl.BlockSpec((None, d, 2 * D_FF), const3, pipeline_mode=once),
            pl.BlockSpec((None, CONV_W, D_FF), const3),
            pl.BlockSpec((None, 1, D_FF), const3),
            pl.BlockSpec((None, D_FF, d), const3, pipeline_mode=once),
            pspec,
            pspec,
        ],
        out_specs=[pl.BlockSpec((tm, d), rowmap), uspec],
        out_shape=[jax.ShapeDtypeStruct((t, d), F32), ushape],
        scratch_shapes=[pltpu.VMEM((SUBLANES, D_FF), F32), pltpu.VMEM((tm, d), F32)],
        compiler_params=_params(("arbitrary", "arbitrary")),
        name="conv_ffn",
    )(x, nw, sc, sh, gate, w_up, conv_w, conv_b, w_down, p1, p2)


def _final_norm_kernel(x_ref, w_ref, o_ref):
    x = x_ref[...]
    o_ref[...] = x * lax.rsqrt(jnp.mean(x * x, axis=-1, keepdims=True) + EPS) * w_ref[...]


def _final_norm(x, w, *, tm):
    t, d = x.shape
    return pl.pallas_call(
        _final_norm_kernel,
        grid=(t // tm,),
        in_specs=[pl.BlockSpec((tm, d), lambda i: (i, 0)), pl.BlockSpec((1, d), lambda i: (0, 0))],
        out_specs=pl.BlockSpec((tm, d), lambda i: (i, 0)),
        out_shape=jax.ShapeDtypeStruct((t, d), F32),
        compiler_params=_params(("arbitrary",)),
        name="final_norm",
    )(x, w)


def _prep_weights(w_in, gla_w_gate, w_branch, w_out, w_up, w_down):
    widths = (256, 256, 512, 4, 4, 512, 256, 256, 512, GLA_RANK, 512, 512, 512, 512, N_BRANCH * D_MODEL)
    offs = np.concatenate([[0], np.cumsum(widths)])
    names = ("ml_q", "ml_k", "ml_v", "ml_i", "ml_f", "ml_o", "gla_q", "gla_k", "gla_v", "gla_lr", "gla_o",
             "sb_q", "sb_k", "sb_v", "gates")
    part = {nm: w_in[:, :, int(offs[i]):int(offs[i + 1])] for i, nm in enumerate(names)}
    order = ("gates", "ml_v", "ml_o", "gla_v", "gla_o", "sb_q", "sb_k", "sb_v", "ml_q", "ml_k", "gla_q", "gla_k")
    w_main = jnp.concatenate([part[nm] for nm in order], axis=-1).astype(BF16)
    depth, d = w_in.shape[:2]
    small = jnp.concatenate([part["ml_i"], part["ml_f"], part["gla_lr"]], axis=-1)
    w_small = jnp.pad(small, ((0, 0), (0, 0), (0, SMALL_W - small.shape[-1]))).astype(BF16)
    gates_t = jnp.swapaxes(jnp.concatenate([part["ml_i"], part["ml_f"]], axis=-1), 1, 2)
    w_small_t = jnp.pad(gates_t, ((0, 0), (0, SMALL_T_ROWS - gates_t.shape[1]), (0, 0))).astype(BF16)
    wg_pad = jnp.pad(gla_w_gate, ((0, 0), (2 * ML_HEADS, SMALL_W - 2 * ML_HEADS - GLA_RANK), (0, 0))).astype(BF16)
    return (w_main, w_small, w_small_t, wg_pad, w_branch.astype(BF16), w_out.astype(BF16),
            w_up.astype(BF16), w_down.astype(BF16))


def _run_group(x, mod, states, paged, weights, params, *, tm, scan_len, sb_block):
    (w_main, w_small, w_small_t, wg_pad, w_branch, w_out, w_up, w_down) = weights
    (norm1_w, norm2_w, ml_b_i, ml_b_f, ml_norm_w, gla_b_gate, gla_norm_w, sb_bias, conv_w, conv_b) = params
    ml_c0, ml_n0, ml_m0, gla_s0, conv0 = states
    batch, seq, d = x.shape
    depth = w_main.shape[0]
    t = batch * seq
    per_seq = seq < tm
    xt = x.reshape(t, d)
    scan_dtype = F32 if per_seq else BF16
    ks, vs, mcs, mns, mms, gss, bufs = [], [], [], [], [], [], []
    for l in range(depth):
        def modrow(i):
            m = mod[l, :, i, :]
            if per_seq:
                return jnp.repeat(m, seq, axis=0).reshape(t // tm, tm, d)
            return m.reshape(batch, 1, d)
        sh1, sc1, g1, sh2, sc2, g2 = [modrow(i) for i in range(6)]
        rows_per_group = tm if per_seq else seq
        main, small, small_t = _norm_proj(xt, norm1_w, sc1, sh1, w_main, w_small, w_small_t, l,
                                          tm=tm, rows_per_group=rows_per_group)
        if per_seq:
            small_t3 = small_t.reshape(SMALL_T_ROWS, batch, seq).transpose(1, 0, 2)
        else:
            small_t3 = small_t.reshape(1, SMALL_T_ROWS, t)
        b_row = jnp.zeros((1, SMALL_W), F32).at[0, 0:ML_HEADS].set(ml_b_i[l]).at[0, ML_HEADS:2 * ML_HEADS].set(ml_b_f[l])
        b_col = jnp.zeros((SMALL_T_ROWS, 1), F32).at[0:ML_HEADS, 0].set(ml_b_i[l]).at[ML_HEADS:2 * ML_HEADS, 0].set(ml_b_f[l])
        hm, mc, mn, mm = _mlstm(main, small, small_t3, b_row, b_col, ml_norm_w[l].reshape(1, 512),
                                ml_c0[l], ml_n0[l], ml_m0[l].reshape(batch, ML_HEADS, 1),
                                batch=batch, seq=seq, L=scan_len, out_dtype=scan_dtype, per_seq=per_seq)
        hg, gst = _gla(main, small, wg_pad, gla_b_gate, gla_norm_w, jnp.swapaxes(gla_s0[l], -1, -2), l,
                       batch=batch, seq=seq, L=scan_len, out_dtype=scan_dtype)
        k_new = main[:, BLK_SB_K * 512:(BLK_SB_K + 1) * 512]
        v_new = main[:, BLK_SB_V * 512:(BLK_SB_V + 1) * 512]
        if paged is None:
            bias_row = jnp.repeat(sb_bias[l], SB_DH).reshape(1, 512)
            hs = _sb_prompt(main, bias_row, batch=batch, seq=seq, tb=sb_block)
        else:
            cache_k, cache_v, page_table = paged
            ps = cache_k.shape[2]
            q = main[:, BLK_SB_Q * 512:(BLK_SB_Q + 1) * 512].reshape(batch, seq, 512)
            bias_col = jnp.repeat(sb_bias[l], seq).reshape(SB_HEADS * seq, 1)
            pad = ((0, 0), (0, ps - seq), (0, 0))
            hs = _sb_sample(page_table, q, bias_col, jnp.pad(k_new.reshape(batch, seq, 512), pad),
                            jnp.pad(v_new.reshape(batch, seq, 512), pad), cache_k, cache_v, l,
                            pages_per_step=math.gcd(page_table.shape[1], 8)).reshape(t, 512)
        xt = _merge(hm, hg, hs, main, xt, g1, w_branch, w_out, l, tm=tm, rows_per_group=rows_per_group)
        if per_seq:
            c0 = conv0[l]
            zero = jnp.zeros((batch, seq - 1, D_FF), F32)
            p1 = jnp.concatenate([c0[:, 1:2], zero], axis=1).reshape(t, D_FF)
            p2 = jnp.concatenate([c0[:, 0:1], c0[:, 1:2], zero[:, 1:]], axis=1).reshape(t, D_FF)
            xt, u = _ffn(xt, norm2_w, sc2, sh2, g2, w_up, conv_w, conv_b, w_down, p1, p2, l,
                         batch=batch, seq=seq, tm=tm, seq_in_tile=seq)
            full = jnp.concatenate([conv0[l], u.reshape(batch, seq, D_FF)], axis=1)
            buf = full[:, seq:]
        else:
            p = jnp.pad(conv0[l], ((0, 0), (SUBLANES - (CONV_W - 1), 0), (0, 0)))
            xt, tail = _ffn(xt, norm2_w, sc2, sh2, g2, w_up, conv_w, conv_b, w_down, p, p, l,
                            batch=batch, seq=seq, tm=tm, seq_in_tile=None)
            buf = tail[:, SUBLANES - (CONV_W - 1):]
        ks.append(k_new.reshape(batch, seq, SB_HEADS, SB_DH))
        vs.append(v_new.reshape(batch, seq, SB_HEADS, SB_DH))
        mcs.append(mc)
        mns.append(mn)
        mms.append(mm.reshape(batch, ML_HEADS))
        gss.append(jnp.swapaxes(gst, -1, -2))
        bufs.append(buf)
    st = jnp.stack
    return xt, st(ks), st(vs), st(mcs), st(mns), st(mms), st(gss), st(bufs)


def kernel(x_prompt, x_sample, cache_k, cache_v, state_mlstm_c, state_mlstm_n, state_mlstm_m, state_gla, state_conv, page_table, c_prompt, c_sample, w_ada, b_ada, norm1_w, norm2_w, w_in, ml_b_i, ml_b_f, ml_norm_w, gla_w_gate, gla_b_gate, gla_norm_w, sb_bias, w_branch, w_out, w_up, conv_w, conv_b, w_down, final_norm_w):
    depth = w_in.shape[0]
    bp, sp, d = x_prompt.shape
    bs, ss, _ = x_sample.shape
    weights = _prep_weights(w_in, gla_w_gate, w_branch, w_out, w_up, w_down)
    params = (norm1_w.reshape(depth, 1, d), norm2_w.reshape(depth, 1, d), ml_b_i, ml_b_f, ml_norm_w,
              gla_b_gate.reshape(depth, 1, -1), gla_norm_w.reshape(depth, 1, -1), sb_bias, conv_w,
              conv_b.reshape(depth, 1, -1))
    mod = _modulation(jnp.concatenate([c_prompt, c_sample], axis=0), w_ada, b_ada)
    mod_p = mod[:, :bp].reshape(depth, bp, 6, d)
    mod_s = mod[:, bp:].reshape(depth, bs, 6, d)

    zeros = lambda *shape: jnp.zeros((depth, bp) + shape, F32)
    states_p = (zeros(ML_HEADS, ML_DK, ML_DV), zeros(ML_HEADS, ML_DK),
                jnp.full((depth, bp, ML_HEADS), NEG_BIG, F32), zeros(GLA_HEADS, GLA_DK, GLA_DV),
                zeros(CONV_W - 1, D_FF))
    tm_p = math.gcd(sp, 512)
    scan_p = math.gcd(sp, 256)
    out_p = _run_group(x_prompt, mod_p, states_p, None, weights, params, tm=tm_p, scan_len=scan_p, sb_block=scan_p)

    states_s = (state_mlstm_c, state_mlstm_n, state_mlstm_m, state_gla, state_conv)
    out_s = _run_group(x_sample, mod_s, states_s, (cache_k, cache_v, page_table), weights, params,
                       tm=bs * ss, scan_len=ss, sb_block=None)

    y_p = _final_norm(out_p[0], final_norm_w.reshape(1, d), tm=tm_p).reshape(bp, sp, d)
    y_s = _final_norm(out_s[0], final_norm_w.reshape(1, d), tm=bs * ss).reshape(bs, ss, d)
    (_, k_p, v_p, mc_p, mn_p, mm_p, g_p, cb_p) = out_p
    (_, k_s, v_s, mc_s, mn_s, mm_s, g_s, cb_s) = out_s
    return (y_p, y_s, k_p, v_p, k_s, v_s, mc_p, mn_p, mm_p, mc_s, mn_s, mm_s, g_p, g_s, cb_p, cb_s)
```

```python
import functools
import math

import numpy as np
import jax
import jax.numpy as jnp
from jax import lax
from jax.experimental import pallas as pl
from jax.experimental.pallas import tpu as pltpu

F32 = jnp.float32
BF16 = jnp.bfloat16
HIGHEST = lax.Precision.HIGHEST

D_MODEL = 1024
ML_HEADS, ML_DK, ML_DV = 4, 64, 128
GLA_HEADS, GLA_DK, GLA_DV = 4, 64, 128
GLA_RANK = 16
GLA_TAU = 16.0
SB_HEADS, SB_DH = 8, 64
BRANCH_W = 512
N_BRANCH = 3
D_FF = 2816
CONV_W = 3
EPS = 1e-6
NEG_BIG = -1e30
SCAN_CHUNK = 64

LANES = 128
SUBLANES = 8
VMEM_LIMIT = 56 * 1024 * 1024

N_MAIN = 6656
BLK_ML_V, BLK_ML_O, BLK_GLA_V, BLK_GLA_O, BLK_SB_Q = 6, 7, 8, 9, 10
BLK_ML_Q, BLK_ML_K, BLK_GLA_Q, BLK_GLA_K = 22, 23, 24, 25
MAIN_TN = 1664
LOG2E = 1.4426950408889634
SMALL_W = 128
SMALL_T_ROWS = 32


def _nt(a, b, precision=None):
    return lax.dot_general(a, b, (((1,), (1,)), ((), ())), preferred_element_type=F32, precision=precision)


def _tn(a, b):
    return lax.dot_general(a, b, (((0,), (0,)), ((), ())), preferred_element_type=F32)


def _mm(a, b, precision=None):
    return jnp.dot(a, b, preferred_element_type=F32, precision=precision)


def _log_sigmoid(x):
    return jnp.minimum(x, 0.0) - jnp.log1p(jnp.exp(-jnp.abs(x)))


def _softplus2(x):
    return jnp.maximum(x, 0.0) + jnp.log2(1.0 + jnp.exp2(-jnp.abs(x)))


def _params(sem):
    return pltpu.CompilerParams(dimension_semantics=sem, vmem_limit_bytes=VMEM_LIMIT)


def _mod_kernel(c_ref, w_ref, b_ref, o_ref):
    c = c_ref[...]
    s = c * jax.nn.sigmoid(c)
    o_ref[...] = _mm(s.astype(BF16), w_ref[...].astype(BF16)) + b_ref[...]


def _modulation(c_all, w_ada, b_ada):
    depth, d, n = w_ada.shape
    nb = c_all.shape[0]
    tn = 1536
    return pl.pallas_call(
        _mod_kernel,
        grid=(depth, n // tn),
        in_specs=[
            pl.BlockSpec((nb, d), lambda l, j: (0, 0)),
            pl.BlockSpec((None, d, tn), lambda l, j: (l, 0, j)),
            pl.BlockSpec((None, 1, tn), lambda l, j: (l, 0, j)),
        ],
        out_specs=pl.BlockSpec((None, nb, tn), lambda l, j: (l, 0, j)),
        out_shape=jax.ShapeDtypeStruct((depth, nb, n), F32),
        compiler_params=_params(("arbitrary", "arbitrary")),
        name="adaln_mod",
    )(c_all, w_ada, b_ada.reshape(depth, 1, n))


def _normmm_kernel(x_ref, nw_ref, sc_ref, sh_ref, w_ref, ws_ref, wt_ref, wk_ref, wv_ref, kbuf_ref, vbuf_ref,
                   o_ref, os_ref, ot_ref, kt_ref, vt_ref, h_scr):
    del kbuf_ref, vbuf_ref

    @pl.when(pl.program_id(1) == 0)
    def _():
        x = x_ref[...]
        y = x * lax.rsqrt(jnp.mean(x * x, axis=-1, keepdims=True) + EPS)
        h = y * nw_ref[...] * (1.0 + sc_ref[...]) + sh_ref[...]
        hb = h.astype(BF16)
        h_scr[...] = hb
        os_ref[...] = _mm(hb, ws_ref[...])
        ot_ref[...] = _nt(wt_ref[...], hb)
        kt_ref[...] = _nt(wk_ref[...], hb)
        vt_ref[...] = _nt(wv_ref[...], hb)

    o_ref[...] = _mm(h_scr[...], w_ref[...])


def _norm_proj(x, nw, sc, sh, weights, kbuf, vbuf, layer, *, tm, rows_per_group):
    w_main, w_small, w_small_t, w_kt, w_vt = weights
    t, d = x.shape
    n = w_main.shape[-1]
    tn = MAIN_TN
    r = sc.shape[1]
    tiles_per_group = rows_per_group // tm
    grp = lambda i, j: (i // tiles_per_group, 0, 0)
    kv_rows = w_kt.shape[1]
    kv_map = lambda i, j: (layer, i // tiles_per_group, 0, i % tiles_per_group)
    wmap = lambda i, j: (layer, 0, 0)
    return pl.pallas_call(
        _normmm_kernel,
        grid=(t // tm, n // tn),
        in_specs=[
            pl.BlockSpec((tm, d), lambda i, j: (i, 0)),
            pl.BlockSpec((None, 1, d), wmap),
            pl.BlockSpec((None, r, d), grp),
            pl.BlockSpec((None, r, d), grp),
            pl.BlockSpec((None, d, tn), lambda i, j: (layer, 0, j)),
            pl.BlockSpec((None, d, SMALL_W), wmap),
            pl.BlockSpec((None, SMALL_T_ROWS, d), wmap),
            pl.BlockSpec((None, kv_rows, d), wmap),
            pl.BlockSpec((None, kv_rows, d), wmap),
            pl.BlockSpec(memory_space=pl.ANY),
            pl.BlockSpec(memory_space=pl.ANY),
        ],
        out_specs=[
            pl.BlockSpec((tm, tn), lambda i, j: (i, j)),
            pl.BlockSpec((tm, SMALL_W), lambda i, j: (i, 0)),
            pl.BlockSpec((SMALL_T_ROWS, tm), lambda i, j: (0, i)),
            pl.BlockSpec((None, None, kv_rows, tm), kv_map),
            pl.BlockSpec((None, None, kv_rows, tm), kv_map),
        ],
        out_shape=[
            jax.ShapeDtypeStruct((t, n), F32),
            jax.ShapeDtypeStruct((t, SMALL_W), F32),
            jax.ShapeDtypeStruct((SMALL_T_ROWS, t), F32),
            jax.ShapeDtypeStruct(kbuf.shape, F32),
            jax.ShapeDtypeStruct(vbuf.shape, F32),
        ],
        input_output_aliases={9: 3, 10: 4},
        scratch_shapes=[pltpu.VMEM((tm, d), BF16)],
        compiler_params=_params(("arbitrary", "arbitrary")),
        name="norm_in_proj",
    )(x, nw, sc, sh, w_main, w_small, w_small_t, w_kt, w_vt, kbuf, vbuf)


def _mlstm_kernel(q_ref, k_ref, v_ref, o_ref, sc_ref, st_ref, brow_ref, bcol_ref, nw_ref, c0_ref, n0_ref, m0_ref,
                  hn_ref, c_ref, n_ref, m_ref, *, L):
    @pl.when(pl.program_id(1) == 0)
    def _():
        c_ref[...] = c0_ref[...]
        n_ref[...] = n0_ref[...]
        m_ref[...] = m0_ref[...]

    row = lax.broadcasted_iota(jnp.int32, (L, L), 0)
    col = lax.broadcasted_iota(jnp.int32, (L, L), 1)
    causal = col <= row
    tril = causal.astype(F32)
    triu = (row <= col).astype(F32)
    small = sc_ref[...] + brow_ref[...]
    small_t = st_ref[...] + bcol_ref[...]
    bc_all = _mm(tril, _log_sigmoid(small), HIGHEST)
    br_all = _mm(_log_sigmoid(small_t), triu, HIGHEST)
    q = q_ref[...] * (ML_DK ** -0.5)
    k = k_ref[...]
    v = v_ref[...]
    og = o_ref[...]
    nw = nw_ref[...]
    for h in range(ML_HEADS):
        ks = slice(h * ML_DK, (h + 1) * ML_DK)
        vs = slice(h * ML_DV, (h + 1) * ML_DV)
        qh, kh, vh = q[:, ks], k[:, ks], v[:, vs]
        b_col = bc_all[:, ML_HEADS + h:ML_HEADS + h + 1]
        b_row = br_all[ML_HEADS + h:ML_HEADS + h + 1, :]
        ig_col = small[:, h:h + 1]
        ig_row = small_t[h:h + 1, :]
        m_prev = m_ref[h:h + 1, :]
        dmat = jnp.where(causal, b_col - b_row + ig_row, -jnp.inf)
        inter = b_col + m_prev
        m_t = jnp.maximum(inter, jnp.max(dmat, axis=-1, keepdims=True))
        w = jnp.exp(dmat - m_t)
        qb = qh.astype(BF16)
        vb = vh.astype(BF16)
        s = _nt(qb, kh.astype(BF16)) * w
        e_inter = jnp.exp(inter - m_t)
        c_st = c_ref[h]
        n_st = n_ref[h:h + 1, :]
        num = _mm(s.astype(BF16), vb) + e_inter * _mm(qb, c_st.astype(BF16))
        den = jnp.sum(s, axis=-1, keepdims=True) + e_inter * jnp.sum(qh * n_st, axis=-1, keepdims=True)
        hh = num / jnp.maximum(jnp.abs(den), jnp.exp(-m_t))
        m_new = m_t[L - 1:L, :]
        b_last = b_col[L - 1:L, :]
        wk = jnp.exp(b_last - b_col + ig_col - m_new)
        decay = jnp.exp(b_last + m_prev - m_new)
        kw = kh * wk
        c_ref[h] = decay * c_st + _tn(kw.astype(BF16), vb)
        n_ref[h:h + 1, :] = decay * n_st + jnp.sum(kw, axis=0, keepdims=True)
        m_ref[h:h + 1, :] = m_new
        y = hh * lax.rsqrt(jnp.mean(hh * hh, axis=-1, keepdims=True) + EPS) * nw[:, vs]
        hn_ref[:, vs] = (y * jax.nn.sigmoid(og[:, vs])).astype(hn_ref.dtype)


def _small_t_spec(L, nc, per_seq):
    if per_seq:
        return pl.BlockSpec((None, SMALL_T_ROWS, L), lambda b, c: (b, 0, 0))
    return pl.BlockSpec((None, SMALL_T_ROWS, L), lambda b, c: (0, 0, b * nc + c))


def _mlstm(main, small, small_t3, b_row, b_col, nw, c0, n0, m0, *, batch, seq, L, out_dtype, per_seq):
    nc = seq // L
    t = batch * seq
    tok = lambda blk: (lambda b, c: (b * nc + c, blk))
    st = lambda b, c: (b, 0, 0)
    kern = functools.partial(_mlstm_kernel, L=L)
    return pl.pallas_call(
        kern,
        grid=(batch, nc),
        in_specs=[
            pl.BlockSpec((L, 256), tok(BLK_ML_Q)),
            pl.BlockSpec((L, 256), tok(BLK_ML_K)),
            pl.BlockSpec((L, 512), tok(BLK_ML_V)),
            pl.BlockSpec((L, 512), tok(BLK_ML_O)),
            pl.BlockSpec((L, SMALL_W), tok(0)),
            _small_t_spec(L, nc, per_seq),
            pl.BlockSpec((1, SMALL_W), lambda b, c: (0, 0)),
            pl.BlockSpec((SMALL_T_ROWS, 1), lambda b, c: (0, 0)),
            pl.BlockSpec((1, 512), lambda b, c: (0, 0)),
            pl.BlockSpec((None, ML_HEADS, ML_DK, ML_DV), lambda b, c: (b, 0, 0, 0)),
            pl.BlockSpec((None, ML_HEADS, ML_DK), st),
            pl.BlockSpec((None, ML_HEADS, 1), st),
        ],
        out_specs=[
            pl.BlockSpec((L, 512), tok(0)),
            pl.BlockSpec((None, ML_HEADS, ML_DK, ML_DV), lambda b, c: (b, 0, 0, 0)),
            pl.BlockSpec((None, ML_HEADS, ML_DK), st),
            pl.BlockSpec((None, ML_HEADS, 1), st),
        ],
        out_shape=[
            jax.ShapeDtypeStruct((t, 512), out_dtype),
            jax.ShapeDtypeStruct((batch, ML_HEADS, ML_DK, ML_DV), F32),
            jax.ShapeDtypeStruct((batch, ML_HEADS, ML_DK), F32),
            jax.ShapeDtypeStruct((batch, ML_HEADS, 1), F32),
        ],
        compiler_params=_params(("arbitrary", "arbitrary")),
        name="mlstm_scan",
    )(main, main, main, main, small, small_t3, b_row, b_col, nw, c0, n0, m0)


def _gla_kernel(q_ref, k_ref, v_ref, o_ref, sc_ref, wg_ref, bg_ref, nw_ref, s0_ref, hn_ref, s_ref, *, L, LS):
    @pl.when(pl.program_id(1) == 0)
    def _():
        s_ref[...] = s0_ref[...]

    row = lax.broadcasted_iota(jnp.int32, (LS, LS), 0)
    col = lax.broadcasted_iota(jnp.int32, (LS, LS), 1)
    causal = col <= row
    tril = causal.astype(F32)
    la_all = _log_sigmoid(_mm(sc_ref[...].astype(BF16), wg_ref[...]) + bg_ref[...]) * (1.0 / GLA_TAU)
    nw = nw_ref[...]
    for c in range(L // LS):
        rs = slice(c * LS, (c + 1) * LS)
        bc = _mm(tril, la_all[rs, :], HIGHEST)
        ref_row = bc[LS // 2:LS // 2 + 1, :]
        last = bc[LS - 1:LS, :]
        q = q_ref[rs, :] * (GLA_DK ** -0.5)
        k = k_ref[rs, :]
        qe = (q * jnp.exp(bc - ref_row)).astype(BF16)
        ke = (k * jnp.exp(ref_row - bc)).astype(BF16)
        qs = (q * jnp.exp(bc)).astype(BF16)
        kl = (k * jnp.exp(last - bc)).astype(BF16)
        el = jnp.exp(last)
        v = v_ref[rs, :]
        og = o_ref[rs, :]
        for h in range(GLA_HEADS):
            ks = slice(h * GLA_DK, (h + 1) * GLA_DK)
            vs = slice(h * GLA_DV, (h + 1) * GLA_DV)
            vb = v[:, vs].astype(BF16)
            st = s_ref[h]
            a = jnp.where(causal, _nt(qe[:, ks], ke[:, ks]), 0.0)
            o = _mm(a.astype(BF16), vb) + _nt(qs[:, ks], st.astype(BF16))
            s_ref[h] = el[:, ks] * st + _tn(vb, kl[:, ks])
            y = o * lax.rsqrt(jnp.mean(o * o, axis=-1, keepdims=True) + EPS) * nw[:, vs]
            g = og[:, vs]
            hn_ref[rs, vs] = (y * (g * jax.nn.sigmoid(g))).astype(hn_ref.dtype)


def _gla(main, small, wg_pad, bg, nw, s0t, layer, *, batch, seq, L, out_dtype):
    nc = seq // L
    t = batch * seq
    ls = math.gcd(L, SCAN_CHUNK)
    tok = lambda blk: (lambda b, c: (b * nc + c, blk))
    kern = functools.partial(_gla_kernel, L=L, LS=ls)
    return pl.pallas_call(
        kern,
        grid=(batch, nc),
        in_specs=[
            pl.BlockSpec((L, 256), tok(BLK_GLA_Q)),
            pl.BlockSpec((L, 256), tok(BLK_GLA_K)),
            pl.BlockSpec((L, 512), tok(BLK_GLA_V)),
            pl.BlockSpec((L, 512), tok(BLK_GLA_O)),
            pl.BlockSpec((L, SMALL_W), tok(0)),
            pl.BlockSpec((None, SMALL_W, 256), lambda b, c: (layer, 0, 0)),
            pl.BlockSpec((None, 1, 256), lambda b, c: (layer, 0, 0)),
            pl.BlockSpec((None, 1, 512), lambda b, c: (layer, 0, 0)),
            pl.BlockSpec((None, GLA_HEADS, GLA_DV, GLA_DK), lambda b, c: (b, 0, 0, 0)),
        ],
        out_specs=[
            pl.BlockSpec((L, 512), tok(0)),
            pl.BlockSpec((None, GLA_HEADS, GLA_DV, GLA_DK), lambda b, c: (b, 0, 0, 0)),
        ],
        out_shape=[
            jax.ShapeDtypeStruct((t, 512), out_dtype),
            jax.ShapeDtypeStruct((batch, GLA_HEADS, GLA_DV, GLA_DK), F32),
        ],
        compiler_params=_params(("arbitrary", "arbitrary")),
        name="gla_scan",
    )(main, main, main, main, small, wg_pad, bg, nw, s0t)


def _sb_prompt_kernel(qi_ref, kj_ref, q_ref, kt_ref, vt_ref, bias_ref, o_ref, acc_ref, r_ref, *, TB):
    p = pl.program_id(1)
    i = qi_ref[p]
    j = kj_ref[p]

    @pl.when(j == i)
    def _():
        acc_ref[...] = jnp.zeros_like(acc_ref)
        r_ref[...] = jnp.zeros_like(r_ref)

    row = lax.broadcasted_iota(jnp.int32, (TB, TB), 0)
    col = lax.broadcasted_iota(jnp.int32, (TB, TB), 1)
    suffix = (row >= col).astype(BF16)
    q = (q_ref[...] * (SB_DH ** -0.5 * LOG2E)).astype(BF16)
    bias = bias_ref[...] * LOG2E

    def run(masked):
        valid = col < row
        for h in range(SB_HEADS):
            hs = slice(h * SB_DH, (h + 1) * SB_DH)
            z = _mm(q[:, hs], kt_ref[hs, :].astype(BF16)) + bias[:, h * SB_DH:h * SB_DH + 1]
            sp = _softplus2(z)
            if masked:
                sp = jnp.where(valid, sp, 0.0)
            rest = _mm(sp.astype(BF16), suffix)
            r_h = r_ref[:, h:h + 1]
            a = jnp.exp2(z - rest - r_h)
            if masked:
                a = jnp.where(valid, a, 0.0)
            acc_ref[h] += _nt(a.astype(BF16), vt_ref[hs, :].astype(BF16))
            r_ref[:, h:h + 1] = r_h + rest[:, 0:1]

    pl.when(j == i)(lambda: run(True))
    pl.when(j != i)(lambda: run(False))

    @pl.when(j == 0)
    def _():
        for h in range(SB_HEADS):
            o_ref[:, h * SB_DH:(h + 1) * SB_DH] = acc_ref[h].astype(o_ref.dtype)


def _sb_prompt(main, kbuf, vbuf, bias_row, layer, *, batch, seq, tb):
    nq = seq // tb
    qi = np.concatenate([np.full(i + 1, i) for i in range(nq)]).astype(np.int32)
    kj = np.concatenate([np.arange(i, -1, -1) for i in range(nq)]).astype(np.int32)
    t = batch * seq
    kern = functools.partial(_sb_prompt_kernel, TB=tb)
    kv_spec = pl.BlockSpec((None, None, SB_HEADS * SB_DH, tb), lambda b, p, qi, kj: (layer, b, 0, kj[p]))
    grid_spec = pltpu.PrefetchScalarGridSpec(
        num_scalar_prefetch=2,
        grid=(batch, len(qi)),
        in_specs=[
            pl.BlockSpec((tb, 512), lambda b, p, qi, kj: (b * nq + qi[p], BLK_SB_Q)),
            kv_spec,
            kv_spec,
            pl.BlockSpec((1, 512), lambda b, p, qi, kj: (0, 0)),
        ],
        out_specs=pl.BlockSpec((tb, 512), lambda b, p, qi, kj: (b * nq + qi[p], 0)),
        scratch_shapes=[pltpu.VMEM((SB_HEADS, tb, SB_DH), F32), pltpu.VMEM((tb, LANES), F32)],
    )
    return pl.pallas_call(
        kern,
        grid_spec=grid_spec,
        out_shape=jax.ShapeDtypeStruct((t, 512), BF16),
        compiler_params=_params(("arbitrary", "arbitrary")),
        name="sb_prompt",
    )(jnp.asarray(qi), jnp.asarray(kj), main, kbuf, vbuf, bias_row)


def _sb_sample_kernel(pt_ref, q_ref, bias_ref, kn_ref, vn_ref, *rest, PP, PS, NQ):
    kp = rest[:PP]
    vp = rest[PP:2 * PP]
    o_ref, acc_ref, r_ref = rest[2 * PP:]
    s = pl.program_id(1)
    hq = SB_HEADS * NQ
    q = (q_ref[...] * (SB_DH ** -0.5 * LOG2E)).astype(BF16)
    bias = bias_ref[...] * LOG2E
    group = 2 if PP % 2 == 0 else 1
    row = lax.broadcasted_iota(jnp.int32, (group * PS, group * PS), 0)
    col = lax.broadcasted_iota(jnp.int32, (group * PS, group * PS), 1)
    suffix = (row >= col).astype(BF16)

    def block(get_kt, get_vt, valid):
        z = jnp.concatenate(
            [_mm(q[:, h * SB_DH:(h + 1) * SB_DH], get_kt(h).astype(BF16)) for h in range(SB_HEADS)], axis=0) + bias
        sp = _softplus2(z)
        if valid is not None:
            sp = jnp.where(valid, sp, 0.0)
        nk = z.shape[1]
        rest_ = _mm(sp.astype(BF16), suffix[:nk, :nk])
        r = r_ref[:, 0:1]
        a = jnp.exp2(z - rest_ - r)
        if valid is not None:
            a = jnp.where(valid, a, 0.0)
        for h in range(SB_HEADS):
            rows = slice(h * NQ, (h + 1) * NQ)
            acc_ref[rows, :] += _nt(a[rows, :].astype(BF16), get_vt(h).astype(BF16))
        r_ref[:, 0:1] = r + rest_[:, 0:1]

    @pl.when(s == 0)
    def _():
        acc_ref[...] = jnp.zeros_like(acc_ref)
        r_ref[...] = jnp.zeros_like(r_ref)
        key = lax.broadcasted_iota(jnp.int32, (hq, PS), 1)
        qpos = lax.broadcasted_iota(jnp.int32, (hq, PS), 0) % NQ
        block(lambda h: kn_ref[h * SB_DH:(h + 1) * SB_DH, :], lambda h: vn_ref[h * SB_DH:(h + 1) * SB_DH, :],
              key < qpos)

    def pages(refs, p0):
        return lambda h: jnp.concatenate([refs[p][h] for p in range(p0 + group - 1, p0 - 1, -1)], axis=1)

    for p0 in range(0, PP, group):
        block(pages(kp, p0), pages(vp, p0), None)

    @pl.when(s == pl.num_programs(1) - 1)
    def _():
        for h in range(SB_HEADS):
            o_ref[:, h * SB_DH:(h + 1) * SB_DH] = acc_ref[h * NQ:(h + 1) * NQ, :]


def _sb_sample(page_table, q, bias_col, k_new_pad, v_new_pad, cache_k, cache_v, layer, *, pages_per_step):
    bs, n_pages = page_table.shape
    ps = cache_k.shape[-1]
    nq = q.shape[1]
    pp = pages_per_step
    steps = n_pages // pp

    def page_spec(r):
        def idx(b, s, pt):
            return (layer, pt[b * n_pages + (n_pages - 1 - (s * pp + r))], 0, 0, 0)
        return pl.BlockSpec((None, None, SB_HEADS, SB_DH, ps), idx)

    kern = functools.partial(_sb_sample_kernel, PP=pp, PS=ps, NQ=nq)
    grid_spec = pltpu.PrefetchScalarGridSpec(
        num_scalar_prefetch=1,
        grid=(bs, steps),
        in_specs=[
            pl.BlockSpec((None, nq, 512), lambda b, s, pt: (b, 0, 0)),
            pl.BlockSpec((SB_HEADS * nq, 1), lambda b, s, pt: (0, 0)),
            pl.BlockSpec((None, 512, ps), lambda b, s, pt: (b, 0, 0)),
            pl.BlockSpec((None, 512, ps), lambda b, s, pt: (b, 0, 0)),
        ] + [page_spec(r) for r in range(pp)] + [page_spec(r) for r in range(pp)],
        out_specs=pl.BlockSpec((None, nq, 512), lambda b, s, pt: (b, 0, 0)),
        scratch_shapes=[pltpu.VMEM((SB_HEADS * nq, SB_DH), F32), pltpu.VMEM((SB_HEADS * nq, LANES), F32)],
    )
    return pl.pallas_call(
        kern,
        grid_spec=grid_spec,
        out_shape=jax.ShapeDtypeStruct((bs, nq, 512), F32),
        compiler_params=_params(("arbitrary", "arbitrary")),
        name="sb_sample",
    )(page_table.reshape(-1), q, bias_col, k_new_pad, v_new_pad, *([cache_k] * pp), *([cache_v] * pp))


def _merge_kernel(hm_ref, hg_ref, hs_ref, g0_ref, g1_ref, g2_ref, x_ref, gate_ref, wb_ref, wo_ref, o_ref):
    merged = jax.nn.sigmoid(g0_ref[...]) * _mm(hm_ref[...].astype(BF16), wb_ref[0])
    merged += jax.nn.sigmoid(g1_ref[...]) * _mm(hg_ref[...].astype(BF16), wb_ref[1])
    merged += jax.nn.sigmoid(g2_ref[...]) * _mm(hs_ref[...].astype(BF16), wb_ref[2])
    mix = _mm(merged.astype(BF16), wo_ref[...])
    o_ref[...] = x_ref[...] + gate_ref[...] * mix


def _merge(hm, hg, hs, main, x, gate, w_branch, w_out, layer, *, tm, rows_per_group):
    t, d = x.shape
    r = gate.shape[1]
    tiles_per_group = rows_per_group // tm
    tok = lambda blk: (lambda i: (i, blk))
    return pl.pallas_call(
        _merge_kernel,
        grid=(t // tm,),
        in_specs=[
            pl.BlockSpec((tm, 512), tok(0)),
            pl.BlockSpec((tm, 512), tok(0)),
            pl.BlockSpec((tm, 512), tok(0)),
            pl.BlockSpec((tm, d), tok(0)),
            pl.BlockSpec((tm, d), tok(1)),
            pl.BlockSpec((tm, d), tok(2)),
            pl.BlockSpec((tm, d), tok(0)),
            pl.BlockSpec((None, r, d), lambda i: (i // tiles_per_group, 0, 0)),
            pl.BlockSpec((None, N_BRANCH, BRANCH_W, d), lambda i: (layer, 0, 0, 0)),
            pl.BlockSpec((None, d, d), lambda i: (layer, 0, 0)),
        ],
        out_specs=pl.BlockSpec((tm, d), tok(0)),
        out_shape=jax.ShapeDtypeStruct((t, d), F32),
        compiler_params=_params(("arbitrary",)),
        name="branch_merge",
    )(hm, hg, hs, main, main, main, x, gate, w_branch, w_out)


FF_CHUNK = 256


def _ffn_kernel(x_ref, nw_ref, sc_ref, sh_ref, gate_ref, wu_ref, cw_ref, cb_ref, wd_ref, p1_ref, p2_ref,
                o_ref, u_ref, carry_ref, acc_ref, *, TM, SEQ_IN_TILE):
    x = x_ref[...]
    y = x * lax.rsqrt(jnp.mean(x * x, axis=-1, keepdims=True) + EPS)
    hb = (y * nw_ref[...] * (1.0 + sc_ref[...]) + sh_ref[...]).astype(BF16)
    row = lax.broadcasted_iota(jnp.int32, (TM, FF_CHUNK), 0)
    if SEQ_IN_TILE is None:
        @pl.when(pl.program_id(1) == 0)
        def _():
            carry_ref[...] = p1_ref[...]
        pos = row
    else:
        pos = row % SEQ_IN_TILE
    acc_ref[...] = jnp.zeros_like(acc_ref)
    for c in range(D_FF // FF_CHUNK):
        cs = slice(c * FF_CHUNK, (c + 1) * FF_CHUNK)
        vs = slice(D_FF + c * FF_CHUNK, D_FF + (c + 1) * FF_CHUNK)
        u = _mm(hb, wu_ref[:, cs])
        val = _mm(hb, wu_ref[:, vs])
        if SEQ_IN_TILE is None:
            prev = carry_ref[:, cs]
            prev1 = prev[SUBLANES - 1:SUBLANES, :]
            prev2 = jnp.where(row == 0, prev[SUBLANES - 2:SUBLANES - 1, :], prev1)
            carry_ref[:, cs] = u[TM - SUBLANES:TM, :]
            u_ref[:, cs] = u[TM - SUBLANES:TM, :]
        else:
            prev1 = p1_ref[:, cs]
            prev2 = p2_ref[:, cs]
            u_ref[:, cs] = u
        u_m1 = jnp.where(pos >= 1, pltpu.roll(u, 1, axis=0), prev1)
        u_m2 = jnp.where(pos >= 2, pltpu.roll(u, 2, axis=0), prev2)
        cw = cw_ref[:, cs]
        conv = cb_ref[:, cs] + cw[0:1, :] * u_m2 + cw[1:2, :] * u_m1 + cw[2:3, :] * u
        act = 0.5 * conv * (1.0 + lax.erf(conv * (2.0 ** -0.5))) * val
        acc_ref[...] += _mm(act.astype(BF16), wd_ref[cs, :])
    o_ref[...] = x + gate_ref[...] * acc_ref[...]


def _ffn(x, nw, sc, sh, gate, w_up, conv_w, conv_b, w_down, p1, p2, layer, *, batch, seq, tm, seq_in_tile):
    t, d = x.shape
    r = sc.shape[1]
    per_tile = seq_in_tile is not None
    if per_tile:
        grid = (1, t // tm)
        rowmap = lambda b, i: (i, 0)
        grp = lambda b, i: (i, 0, 0)
        pspec = pl.BlockSpec((tm, D_FF), rowmap)
        uspec = pl.BlockSpec((tm, D_FF), rowmap)
        ushape = jax.ShapeDtypeStruct((t, D_FF), F32)
    else:
        nt = seq // tm
        grid = (batch, nt)
        rowmap = lambda b, i: (b * nt + i, 0)
        grp = lambda b, i: (b, 0, 0)
        pspec = pl.BlockSpec((None, SUBLANES, D_FF), lambda b, i: (b, 0, 0))
        uspec = pl.BlockSpec((None, SUBLANES, D_FF), lambda b, i: (b, 0, 0))
        ushape = jax.ShapeDtypeStruct((batch, SUBLANES, D_FF), F32)
    const3 = lambda b, i: (layer, 0, 0)
    once = pl.Buffered(1)
    kern = functools.partial(_ffn_kernel, TM=tm, SEQ_IN_TILE=seq_in_tile)
    return pl.pallas_call(
        kern,
        grid=grid,
        in_specs=[
            pl.BlockSpec((tm, d), rowmap),
            pl.BlockSpec((None, 1, d), const3),
            pl.BlockSpec((None, r, d), grp),
            pl.BlockSpec((None, r, d), grp),
            pl.BlockSpec((None, r, d), grp),
            pl.BlockSpec((None, d, 2 * D_FF), const3, pipeline_mode=once),
            pl.BlockSpec((None, CONV_W, D_FF), const3),
            pl.BlockSpec((None, 1, D_FF), const3),
            pl.BlockSpec((None, D_FF, d), const3, pipeline_mode=once),
            pspec,
            pspec,
        ],
        out_specs=[pl.BlockSpec((tm, d), rowmap), uspec],
        out_shape=[jax.ShapeDtypeStruct((t, d), F32), ushape],
        scratch_shapes=[pltpu.VMEM((SUBLANES, D_FF), F32), pltpu.VMEM((tm, d), F32)],
        compiler_params=_params(("arbitrary", "arbitrary")),
        name="conv_ffn",
    )(x, nw, sc, sh, gate, w_up, conv_w, conv_b, w_down, p1, p2)


def _final_norm_kernel(x_ref, w_ref, o_ref):
    x = x_ref[...]
    o_ref[...] = x * lax.rsqrt(jnp.mean(x * x, axis=-1, keepdims=True) + EPS) * w_ref[...]


def _final_norm(x, w, *, tm):
    t, d = x.shape
    return pl.pallas_call(
        _final_norm_kernel,
        grid=(t // tm,),
        in_specs=[pl.BlockSpec((tm, d), lambda i: (i, 0)), pl.BlockSpec((1, d), lambda i: (0, 0))],
        out_specs=pl.BlockSpec((tm, d), lambda i: (i, 0)),
        out_shape=jax.ShapeDtypeStruct((t, d), F32),
        compiler_params=_params(("arbitrary",)),
        name="final_norm",
    )(x, w)


def _prep_weights(w_in, gla_w_gate, w_branch, w_out, w_up, w_down):
    widths = (256, 256, 512, 4, 4, 512, 256, 256, 512, GLA_RANK, 512, 512, 512, 512, N_BRANCH * D_MODEL)
    offs = np.concatenate([[0], np.cumsum(widths)])
    names = ("ml_q", "ml_k", "ml_v", "ml_i", "ml_f", "ml_o", "gla_q", "gla_k", "gla_v", "gla_lr", "gla_o",
             "sb_q", "sb_k", "sb_v", "gates")
    part = {nm: w_in[:, :, int(offs[i]):int(offs[i + 1])] for i, nm in enumerate(names)}
    order = ("gates", "ml_v", "ml_o", "gla_v", "gla_o", "sb_q", "ml_q", "ml_k", "gla_q", "gla_k")
    w_main = jnp.concatenate([part[nm] for nm in order], axis=-1).astype(BF16)
    w_kt = jnp.swapaxes(part["sb_k"], 1, 2).astype(BF16)
    w_vt = jnp.swapaxes(part["sb_v"], 1, 2).astype(BF16)
    small = jnp.concatenate([part["ml_i"], part["ml_f"], part["gla_lr"]], axis=-1)
    w_small = jnp.pad(small, ((0, 0), (0, 0), (0, SMALL_W - small.shape[-1]))).astype(BF16)
    gates_t = jnp.swapaxes(jnp.concatenate([part["ml_i"], part["ml_f"]], axis=-1), 1, 2)
    w_small_t = jnp.pad(gates_t, ((0, 0), (0, SMALL_T_ROWS - gates_t.shape[1]), (0, 0))).astype(BF16)
    wg_pad = jnp.pad(gla_w_gate, ((0, 0), (2 * ML_HEADS, SMALL_W - 2 * ML_HEADS - GLA_RANK), (0, 0))).astype(BF16)
    return ((w_main, w_small, w_small_t, w_kt, w_vt), wg_pad, w_branch.astype(BF16), w_out.astype(BF16),
            w_up.astype(BF16), w_down.astype(BF16))


def _run_group(x, mod, states, paged, weights, params, *, tm, scan_len, sb_block):
    (w_proj, wg_pad, w_branch, w_out, w_up, w_down) = weights
    (norm1_w, norm2_w, ml_b_i, ml_b_f, ml_norm_w, gla_b_gate, gla_norm_w, sb_bias, conv_w, conv_b) = params
    ml_c0, ml_n0, ml_m0, gla_s0, conv0 = states
    batch, seq, d = x.shape
    depth = w_proj[0].shape[0]
    t = batch * seq
    per_seq = seq < tm
    xt = x.reshape(t, d)
    scan_dtype = F32 if per_seq else BF16
    kv_groups = 1 if per_seq else batch
    kbuf = jnp.zeros((depth, kv_groups, SB_HEADS * SB_DH, t // kv_groups), F32)
    vbuf = jnp.zeros((depth, kv_groups, SB_HEADS * SB_DH, t // kv_groups), F32)
    mcs, mns, mms, gss, bufs = [], [], [], [], []
    for l in range(depth):
        def modrow(i):
            m = mod[l, :, i, :]
            if per_seq:
                return jnp.repeat(m, seq, axis=0).reshape(t // tm, tm, d)
            return m.reshape(batch, 1, d)
        sh1, sc1, g1, sh2, sc2, g2 = [modrow(i) for i in range(6)]
        rows_per_group = tm if per_seq else seq
        tm_proj = tm if per_seq else math.gcd(seq, 2 * tm)
        main, small, small_t, kbuf, vbuf = _norm_proj(xt, norm1_w, sc1, sh1, w_proj, kbuf, vbuf, l,
                                                      tm=tm_proj, rows_per_group=rows_per_group)
        if per_seq:
            small_t3 = small_t.reshape(SMALL_T_ROWS, batch, seq).transpose(1, 0, 2)
        else:
            small_t3 = small_t.reshape(1, SMALL_T_ROWS, t)
        b_row = jnp.zeros((1, SMALL_W), F32).at[0, 0:ML_HEADS].set(ml_b_i[l]).at[0, ML_HEADS:2 * ML_HEADS].set(ml_b_f[l])
        b_col = jnp.zeros((SMALL_T_ROWS, 1), F32).at[0:ML_HEADS, 0].set(ml_b_i[l]).at[ML_HEADS:2 * ML_HEADS, 0].set(ml_b_f[l])
        hm, mc, mn, mm = _mlstm(main, small, small_t3, b_row, b_col, ml_norm_w[l].reshape(1, 512),
                                ml_c0[l], ml_n0[l], ml_m0[l].reshape(batch, ML_HEADS, 1),
                                batch=batch, seq=seq, L=scan_len, out_dtype=scan_dtype, per_seq=per_seq)
        hg, gst = _gla(main, small, wg_pad, gla_b_gate, gla_norm_w, jnp.swapaxes(gla_s0[l], -1, -2), l,
                       batch=batch, seq=seq, L=scan_len, out_dtype=scan_dtype)
        if paged is None:
            bias_row = jnp.repeat(sb_bias[l], SB_DH).reshape(1, 512)
            hs = _sb_prompt(main, kbuf, vbuf, bias_row, l, batch=batch, seq=seq, tb=sb_block)
        else:
            cache_kt, cache_vt, page_table = paged
            ps = cache_kt.shape[-1]
            q = main[:, BLK_SB_Q * 512:(BLK_SB_Q + 1) * 512].reshape(batch, seq, 512)
            bias_col = jnp.repeat(sb_bias[l], seq).reshape(SB_HEADS * seq, 1)
            new_t = lambda buf: jnp.pad(buf[l, 0].reshape(-1, batch, seq).transpose(1, 0, 2),
                                        ((0, 0), (0, 0), (0, ps - seq)))
            hs = _sb_sample(page_table, q, bias_col, new_t(kbuf), new_t(vbuf), cache_kt, cache_vt, l,
                            pages_per_step=math.gcd(page_table.shape[1], 8)).reshape(t, 512)
        xt = _merge(hm, hg, hs, main, xt, g1, w_branch, w_out, l, tm=tm, rows_per_group=rows_per_group)
        if per_seq:
            c0 = conv0[l]
            zero = jnp.zeros((batch, seq - 1, D_FF), F32)
            p1 = jnp.concatenate([c0[:, 1:2], zero], axis=1).reshape(t, D_FF)
            p2 = jnp.concatenate([c0[:, 0:1], c0[:, 1:2], zero[:, 1:]], axis=1).reshape(t, D_FF)
            xt, u = _ffn(xt, norm2_w, sc2, sh2, g2, w_up, conv_w, conv_b, w_down, p1, p2, l,
                         batch=batch, seq=seq, tm=tm, seq_in_tile=seq)
            full = jnp.concatenate([conv0[l], u.reshape(batch, seq, D_FF)], axis=1)
            buf = full[:, seq:]
        else:
            p = jnp.pad(conv0[l], ((0, 0), (SUBLANES - (CONV_W - 1), 0), (0, 0)))
            xt, tail = _ffn(xt, norm2_w, sc2, sh2, g2, w_up, conv_w, conv_b, w_down, p, p, l,
                            batch=batch, seq=seq, tm=tm, seq_in_tile=None)
            buf = tail[:, SUBLANES - (CONV_W - 1):]
        mcs.append(mc)
        mns.append(mn)
        mms.append(mm.reshape(batch, ML_HEADS))
        gss.append(jnp.swapaxes(gst, -1, -2))
        bufs.append(buf)
    st = jnp.stack

    def kv_out(buf):
        b6 = buf.reshape(depth, kv_groups, SB_HEADS, SB_DH, batch // kv_groups, seq)
        return b6.transpose(0, 1, 4, 5, 2, 3).reshape(depth, batch, seq, SB_HEADS, SB_DH)

    return xt, kv_out(kbuf), kv_out(vbuf), st(mcs), st(mns), st(mms), st(gss), st(bufs)


def kernel(x_prompt, x_sample, cache_k, cache_v, state_mlstm_c, state_mlstm_n, state_mlstm_m, state_gla, state_conv, page_table, c_prompt, c_sample, w_ada, b_ada, norm1_w, norm2_w, w_in, ml_b_i, ml_b_f, ml_norm_w, gla_w_gate, gla_b_gate, gla_norm_w, sb_bias, w_branch, w_out, w_up, conv_w, conv_b, w_down, final_norm_w):
    depth = w_in.shape[0]
    bp, sp, d = x_prompt.shape
    bs, ss, _ = x_sample.shape
    weights = _prep_weights(w_in, gla_w_gate, w_branch, w_out, w_up, w_down)
    params = (norm1_w.reshape(depth, 1, d), norm2_w.reshape(depth, 1, d), ml_b_i, ml_b_f, ml_norm_w,
              gla_b_gate.reshape(depth, 1, -1), gla_norm_w.reshape(depth, 1, -1), sb_bias, conv_w,
              conv_b.reshape(depth, 1, -1))
    mod = _modulation(jnp.concatenate([c_prompt, c_sample], axis=0), w_ada, b_ada)
    mod_p = mod[:, :bp].reshape(depth, bp, 6, d)
    mod_s = mod[:, bp:].reshape(depth, bs, 6, d)

    zeros = lambda *shape: jnp.zeros((depth, bp) + shape, F32)
    states_p = (zeros(ML_HEADS, ML_DK, ML_DV), zeros(ML_HEADS, ML_DK),
                jnp.full((depth, bp, ML_HEADS), NEG_BIG, F32), zeros(GLA_HEADS, GLA_DK, GLA_DV),
                zeros(CONV_W - 1, D_FF))
    tm_p = math.gcd(sp, 512)
    scan_p = math.gcd(sp, 256)
    out_p = _run_group(x_prompt, mod_p, states_p, None, weights, params, tm=tm_p, scan_len=scan_p, sb_block=scan_p)

    states_s = (state_mlstm_c, state_mlstm_n, state_mlstm_m, state_gla, state_conv)
    cache_kt = jnp.transpose(cache_k, (0, 1, 3, 4, 2))
    cache_vt = jnp.transpose(cache_v, (0, 1, 3, 4, 2))
    out_s = _run_group(x_sample, mod_s, states_s, (cache_kt, cache_vt, page_table), weights, params,
                       tm=bs * ss, scan_len=ss, sb_block=None)

    y_p = _final_norm(out_p[0], final_norm_w.reshape(1, d), tm=tm_p).reshape(bp, sp, d)
    y_s = _final_norm(out_s[0], final_norm_w.reshape(1, d), tm=bs * ss).reshape(bs, ss, d)
    (_, k_p, v_p, mc_p, mn_p, mm_p, g_p, cb_p) = out_p
    (_, k_s, v_s, mc_s, mn_s, mm_s, g_s, cb_s) = out_s
    return (y_p, y_s, k_p, v_p, k_s, v_s, mc_p, mn_p, mm_p, mc_s, mn_s, mm_s, g_p, g_s, cb_p, cb_s)
```

```python
import functools
import math

import numpy as np
import jax
import jax.numpy as jnp
from jax import lax
from jax.experimental import pallas as pl
from jax.experimental.pallas import tpu as pltpu

F32 = jnp.float32
BF16 = jnp.bfloat16

D_MODEL = 1024
ML_HEADS, ML_DK, ML_DV = 4, 64, 128
GLA_HEADS, GLA_DK, GLA_DV = 4, 64, 128
GLA_RANK = 16
GLA_TAU = 16.0
SB_HEADS, SB_DH = 8, 64
BRANCH_W = 512
N_BRANCH = 3
D_FF = 2816
CONV_W = 3
EPS = 1e-6
NEG_BIG = -1e30
SCAN_CHUNK = 64

LANES = 128
SUBLANES = 8
VMEM_LIMIT = 56 * 1024 * 1024

N_MAIN = 6656
BLK_ML_V, BLK_ML_O, BLK_GLA_V, BLK_GLA_O, BLK_SB_Q = 6, 7, 8, 9, 10
BLK_ML_Q, BLK_ML_K, BLK_GLA_Q, BLK_GLA_K = 22, 23, 24, 25
MAIN_TN = 1664
LOG2E = 1.4426950408889634
SMALL_W = 128
SMALL_T_ROWS = 32


def _nt(a, b, precision=None):
    return lax.dot_general(a, b, (((1,), (1,)), ((), ())), preferred_element_type=F32, precision=precision)


def _tn(a, b):
    return lax.dot_general(a, b, (((0,), (0,)), ((), ())), preferred_element_type=F32)


def _mm(a, b, precision=None):
    return jnp.dot(a, b, preferred_element_type=F32, precision=precision)


def _cumsum_mm(tri, x, *, tri_first):
    t = tri.astype(BF16)
    hi = x.astype(BF16)
    r1 = x - hi.astype(F32)
    mid = r1.astype(BF16)
    lo = (r1 - mid.astype(F32)).astype(BF16)
    parts = (hi, mid, lo)
    terms = [_mm(t, p) if tri_first else _mm(p, t) for p in parts]
    return terms[0] + (terms[1] + terms[2])


def _log_sigmoid(x):
    return jnp.minimum(x, 0.0) - jnp.log1p(jnp.exp(-jnp.abs(x)))


def _softplus2(x):
    return jnp.maximum(x, 0.0) + jnp.log2(1.0 + jnp.exp2(-jnp.abs(x)))


def _params(sem):
    return pltpu.CompilerParams(dimension_semantics=sem, vmem_limit_bytes=VMEM_LIMIT)


def _mod_kernel(c_ref, w_ref, b_ref, o_ref):
    c = c_ref[...]
    s = c * jax.nn.sigmoid(c)
    o_ref[...] = _mm(s.astype(BF16), w_ref[...].astype(BF16)) + b_ref[...]


def _modulation(c_all, w_ada, b_ada):
    depth, d, n = w_ada.shape
    nb = c_all.shape[0]
    tn = 1536
    return pl.pallas_call(
        _mod_kernel,
        grid=(depth, n // tn),
        in_specs=[
            pl.BlockSpec((nb, d), lambda l, j: (0, 0)),
            pl.BlockSpec((None, d, tn), lambda l, j: (l, 0, j)),
            pl.BlockSpec((None, 1, tn), lambda l, j: (l, 0, j)),
        ],
        out_specs=pl.BlockSpec((None, nb, tn), lambda l, j: (l, 0, j)),
        out_shape=jax.ShapeDtypeStruct((depth, nb, n), F32),
        compiler_params=_params(("arbitrary", "arbitrary")),
        name="adaln_mod",
    )(c_all, w_ada, b_ada.reshape(depth, 1, n))


def _normmm_kernel(x_ref, nw_ref, sc_ref, sh_ref, w_ref, ws_ref, wt_ref, wk_ref, wv_ref, kbuf_ref, vbuf_ref,
                   o_ref, os_ref, ot_ref, kt_ref, vt_ref, h_scr):
    del kbuf_ref, vbuf_ref

    @pl.when(pl.program_id(1) == 0)
    def _():
        x = x_ref[...]
        y = x * lax.rsqrt(jnp.mean(x * x, axis=-1, keepdims=True) + EPS)
        h = y * nw_ref[...] * (1.0 + sc_ref[...]) + sh_ref[...]
        hb = h.astype(BF16)
        h_scr[...] = hb
        os_ref[...] = _mm(hb, ws_ref[...])
        ot_ref[...] = _nt(wt_ref[...], hb)
        kt_ref[...] = _nt(wk_ref[...], hb)
        vt_ref[...] = _nt(wv_ref[...], hb)

    o_ref[...] = _mm(h_scr[...], w_ref[...]).astype(o_ref.dtype)


def _norm_proj(x, nw, sc, sh, weights, kbuf, vbuf, layer, *, tm, rows_per_group, main_dtype):
    w_main, w_small, w_small_t, w_kt, w_vt = weights
    t, d = x.shape
    n = w_main.shape[-1]
    tn = MAIN_TN
    r = sc.shape[1]
    tiles_per_group = rows_per_group // tm
    grp = lambda i, j: (i // tiles_per_group, 0, 0)
    kv_rows = w_kt.shape[1]
    kv_map = lambda i, j: (layer, i // tiles_per_group, 0, i % tiles_per_group)
    wmap = lambda i, j: (layer, 0, 0)
    return pl.pallas_call(
        _normmm_kernel,
        grid=(t // tm, n // tn),
        in_specs=[
            pl.BlockSpec((tm, d), lambda i, j: (i, 0)),
            pl.BlockSpec((None, 1, d), wmap),
            pl.BlockSpec((None, r, d), grp),
            pl.BlockSpec((None, r, d), grp),
            pl.BlockSpec((None, d, tn), lambda i, j: (layer, 0, j)),
            pl.BlockSpec((None, d, SMALL_W), wmap),
            pl.BlockSpec((None, SMALL_T_ROWS, d), wmap),
            pl.BlockSpec((None, kv_rows, d), wmap),
            pl.BlockSpec((None, kv_rows, d), wmap),
            pl.BlockSpec(memory_space=pl.ANY),
            pl.BlockSpec(memory_space=pl.ANY),
        ],
        out_specs=[
            pl.BlockSpec((tm, tn), lambda i, j: (i, j)),
            pl.BlockSpec((tm, SMALL_W), lambda i, j: (i, 0)),
            pl.BlockSpec((SMALL_T_ROWS, tm), lambda i, j: (0, i)),
            pl.BlockSpec((None, None, kv_rows, tm), kv_map),
            pl.BlockSpec((None, None, kv_rows, tm), kv_map),
        ],
        out_shape=[
            jax.ShapeDtypeStruct((t, n), main_dtype),
            jax.ShapeDtypeStruct((t, SMALL_W), F32),
            jax.ShapeDtypeStruct((SMALL_T_ROWS, t), F32),
            jax.ShapeDtypeStruct(kbuf.shape, F32),
            jax.ShapeDtypeStruct(vbuf.shape, F32),
        ],
        input_output_aliases={9: 3, 10: 4},
        scratch_shapes=[pltpu.VMEM((tm, d), BF16)],
        compiler_params=_params(("arbitrary", "arbitrary")),
        name="norm_in_proj",
    )(x, nw, sc, sh, w_main, w_small, w_small_t, w_kt, w_vt, kbuf, vbuf)


def _mlstm_kernel(q_ref, k_ref, v_ref, o_ref, sc_ref, st_ref, brow_ref, bcol_ref, nw_ref, c0_ref, n0_ref, m0_ref,
                  hn_ref, c_ref, n_ref, m_ref, *, L):
    @pl.when(pl.program_id(1) == 0)
    def _():
        c_ref[...] = c0_ref[...]
        n_ref[...] = n0_ref[...]
        m_ref[...] = m0_ref[...]

    row = lax.broadcasted_iota(jnp.int32, (L, L), 0)
    col = lax.broadcasted_iota(jnp.int32, (L, L), 1)
    causal = col <= row
    small = sc_ref[...] + brow_ref[...]
    small_t = st_ref[...] + bcol_ref[...]
    bc_all = _cumsum_mm(causal, _log_sigmoid(small), tri_first=True)
    br_all = _cumsum_mm(row <= col, _log_sigmoid(small_t), tri_first=False)
    q = q_ref[...].astype(F32) * (ML_DK ** -0.5)
    k = k_ref[...].astype(F32)
    v = v_ref[...].astype(F32)
    og = o_ref[...].astype(F32)
    nw = nw_ref[...]
    heads = range(ML_HEADS)
    ks = [slice(h * ML_DK, (h + 1) * ML_DK) for h in heads]
    vs = [slice(h * ML_DV, (h + 1) * ML_DV) for h in heads]
    qb = [q[:, ks[h]].astype(BF16) for h in heads]
    vb = [v[:, vs[h]].astype(BF16) for h in heads]
    qk = [_nt(qb[h], k[:, ks[h]].astype(BF16)) for h in heads]
    b_col = [bc_all[:, ML_HEADS + h:ML_HEADS + h + 1] for h in heads]
    b_row = [br_all[ML_HEADS + h:ML_HEADS + h + 1, :] for h in heads]
    m_prev = [m_ref[h:h + 1, :] for h in heads]
    c_st = [c_ref[h] for h in heads]
    n_st = [n_ref[h:h + 1, :] for h in heads]
    dmat = [jnp.where(causal, b_col[h] - b_row[h] + small_t[h:h + 1, :], -jnp.inf) for h in heads]
    inter = [b_col[h] + m_prev[h] for h in heads]
    m_t = [jnp.maximum(inter[h], jnp.max(dmat[h], axis=-1, keepdims=True)) for h in heads]
    s = [qk[h] * jnp.exp(dmat[h] - m_t[h]) for h in heads]
    e_inter = [jnp.exp(inter[h] - m_t[h]) for h in heads]
    qc = [_mm(qb[h], c_st[h].astype(BF16)) for h in heads]
    num = [_mm(s[h].astype(BF16), vb[h]) + e_inter[h] * qc[h] for h in heads]
    den = [jnp.sum(s[h], axis=-1, keepdims=True)
           + e_inter[h] * jnp.sum(q[:, ks[h]] * n_st[h], axis=-1, keepdims=True) for h in heads]
    hh = [num[h] / jnp.maximum(jnp.abs(den[h]), jnp.exp(-m_t[h])) for h in heads]
    for h in heads:
        m_new = m_t[h][L - 1:L, :]
        b_last = b_col[h][L - 1:L, :]
        wk = jnp.exp(b_last - b_col[h] + small[:, h:h + 1] - m_new)
        decay = jnp.exp(b_last + m_prev[h] - m_new)
        kw = k[:, ks[h]] * wk
        c_ref[h] = decay * c_st[h] + _tn(kw.astype(BF16), vb[h])
        n_ref[h:h + 1, :] = decay * n_st[h] + jnp.sum(kw, axis=0, keepdims=True)
        m_ref[h:h + 1, :] = m_new
    for h in heads:
        y = hh[h] * lax.rsqrt(jnp.mean(hh[h] * hh[h], axis=-1, keepdims=True) + EPS) * nw[:, vs[h]]
        hn_ref[:, vs[h]] = (y * jax.nn.sigmoid(og[:, vs[h]])).astype(hn_ref.dtype)


def _small_t_spec(L, nc, per_seq):
    if per_seq:
        return pl.BlockSpec((None, SMALL_T_ROWS, L), lambda b, c: (b, 0, 0))
    return pl.BlockSpec((None, SMALL_T_ROWS, L), lambda b, c: (0, 0, b * nc + c))


def _mlstm(main, small, small_t3, b_row, b_col, nw, c0, n0, m0, *, batch, seq, L, out_dtype, per_seq):
    nc = seq // L
    t = batch * seq
    tok = lambda blk: (lambda b, c: (b * nc + c, blk))
    st = lambda b, c: (b, 0, 0)
    kern = functools.partial(_mlstm_kernel, L=L)
    return pl.pallas_call(
        kern,
        grid=(batch, nc),
        in_specs=[
            pl.BlockSpec((L, 256), tok(BLK_ML_Q)),
            pl.BlockSpec((L, 256), tok(BLK_ML_K)),
            pl.BlockSpec((L, 512), tok(BLK_ML_V)),
            pl.BlockSpec((L, 512), tok(BLK_ML_O)),
            pl.BlockSpec((L, SMALL_W), tok(0)),
            _small_t_spec(L, nc, per_seq),
            pl.BlockSpec((1, SMALL_W), lambda b, c: (0, 0)),
            pl.BlockSpec((SMALL_T_ROWS, 1), lambda b, c: (0, 0)),
            pl.BlockSpec((1, 512), lambda b, c: (0, 0)),
            pl.BlockSpec((None, ML_HEADS, ML_DK, ML_DV), lambda b, c: (b, 0, 0, 0)),
            pl.BlockSpec((None, ML_HEADS, ML_DK), st),
            pl.BlockSpec((None, ML_HEADS, 1), st),
        ],
        out_specs=[
            pl.BlockSpec((L, 512), tok(0)),
            pl.BlockSpec((None, ML_HEADS, ML_DK, ML_DV), lambda b, c: (b, 0, 0, 0)),
            pl.BlockSpec((None, ML_HEADS, ML_DK), st),
            pl.BlockSpec((None, ML_HEADS, 1), st),
        ],
        out_shape=[
            jax.ShapeDtypeStruct((t, 512), out_dtype),
            jax.ShapeDtypeStruct((batch, ML_HEADS, ML_DK, ML_DV), F32),
            jax.ShapeDtypeStruct((batch, ML_HEADS, ML_DK), F32),
            jax.ShapeDtypeStruct((batch, ML_HEADS, 1), F32),
        ],
        compiler_params=_params(("arbitrary", "arbitrary")),
        name="mlstm_scan",
    )(main, main, main, main, small, small_t3, b_row, b_col, nw, c0, n0, m0)


def _gla_kernel(q_ref, k_ref, v_ref, o_ref, sc_ref, wg_ref, bg_ref, nw_ref, s0_ref, hn_ref, s_ref, *, L, LS):
    @pl.when(pl.program_id(1) == 0)
    def _():
        s_ref[...] = s0_ref[...]

    row = lax.broadcasted_iota(jnp.int32, (LS, LS), 0)
    col = lax.broadcasted_iota(jnp.int32, (LS, LS), 1)
    causal = col <= row
    la_all = _log_sigmoid(_mm(sc_ref[...].astype(BF16), wg_ref[...]) + bg_ref[...]) * (1.0 / GLA_TAU)
    nw = nw_ref[...]
    heads = range(GLA_HEADS)
    ks = [slice(h * GLA_DK, (h + 1) * GLA_DK) for h in heads]
    vs = [slice(h * GLA_DV, (h + 1) * GLA_DV) for h in heads]
    pre = []
    for c in range(L // LS):
        rs = slice(c * LS, (c + 1) * LS)
        bc = _cumsum_mm(causal, la_all[rs, :], tri_first=True)
        ref_row = bc[LS // 2:LS // 2 + 1, :]
        last = bc[LS - 1:LS, :]
        q = q_ref[rs, :].astype(F32) * (GLA_DK ** -0.5)
        k = k_ref[rs, :].astype(F32)
        qe = (q * jnp.exp(bc - ref_row)).astype(BF16)
        ke = (k * jnp.exp(ref_row - bc)).astype(BF16)
        qs = (q * jnp.exp(bc)).astype(BF16)
        kl = (k * jnp.exp(last - bc)).astype(BF16)
        el = jnp.exp(last)
        v = v_ref[rs, :].astype(F32)
        vb = [v[:, vs[h]].astype(BF16) for h in heads]
        av = [_mm(jnp.where(causal, _nt(qe[:, ks[h]], ke[:, ks[h]]), 0.0).astype(BF16), vb[h]) for h in heads]
        pre.append((rs, qs, kl, el, vb, av))
    state = [s_ref[h] for h in heads]
    outs = []
    for rs, qs, kl, el, vb, av in pre:
        o = [av[h] + _nt(qs[:, ks[h]], state[h].astype(BF16)) for h in heads]
        state = [el[:, ks[h]] * state[h] + _tn(vb[h], kl[:, ks[h]]) for h in heads]
        outs.append((rs, o))
    for h in heads:
        s_ref[h] = state[h]
    for rs, o in outs:
        og = o_ref[rs, :].astype(F32)
        for h in heads:
            y = o[h] * lax.rsqrt(jnp.mean(o[h] * o[h], axis=-1, keepdims=True) + EPS) * nw[:, vs[h]]
            g = og[:, vs[h]]
            hn_ref[rs, vs[h]] = (y * (g * jax.nn.sigmoid(g))).astype(hn_ref.dtype)


def _gla(main, small, wg_pad, bg, nw, s0t, layer, *, batch, seq, L, out_dtype):
    nc = seq // L
    t = batch * seq
    ls = math.gcd(L, SCAN_CHUNK)
    tok = lambda blk: (lambda b, c: (b * nc + c, blk))
    kern = functools.partial(_gla_kernel, L=L, LS=ls)
    return pl.pallas_call(
        kern,
        grid=(batch, nc),
        in_specs=[
            pl.BlockSpec((L, 256), tok(BLK_GLA_Q)),
            pl.BlockSpec((L, 256), tok(BLK_GLA_K)),
            pl.BlockSpec((L, 512), tok(BLK_GLA_V)),
            pl.BlockSpec((L, 512), tok(BLK_GLA_O)),
            pl.BlockSpec((L, SMALL_W), tok(0)),
            pl.BlockSpec((None, SMALL_W, 256), lambda b, c: (layer, 0, 0)),
            pl.BlockSpec((None, 1, 256), lambda b, c: (layer, 0, 0)),
            pl.BlockSpec((None, 1, 512), lambda b, c: (layer, 0, 0)),
            pl.BlockSpec((None, GLA_HEADS, GLA_DV, GLA_DK), lambda b, c: (b, 0, 0, 0)),
        ],
        out_specs=[
            pl.BlockSpec((L, 512), tok(0)),
            pl.BlockSpec((None, GLA_HEADS, GLA_DV, GLA_DK), lambda b, c: (b, 0, 0, 0)),
        ],
        out_shape=[
            jax.ShapeDtypeStruct((t, 512), out_dtype),
            jax.ShapeDtypeStruct((batch, GLA_HEADS, GLA_DV, GLA_DK), F32),
        ],
        compiler_params=_params(("arbitrary", "arbitrary")),
        name="gla_scan",
    )(main, main, main, main, small, wg_pad, bg, nw, s0t)


SB_HEAD_GROUP = 2


def _sb_prompt_kernel(qi_ref, kj_ref, q_ref, kt_ref, vt_ref, bias_ref, o_ref, acc_ref, r_ref, *, TQ, TK):
    p = pl.program_id(1)
    i = qi_ref[p]
    j = kj_ref[p]
    ratio = TQ // TK

    @pl.when(j == ratio * i + ratio - 1)
    def _():
        acc_ref[...] = jnp.zeros_like(acc_ref)
        r_ref[...] = jnp.zeros_like(r_ref)

    row = lax.broadcasted_iota(jnp.int32, (TK, TK), 0)
    col = lax.broadcasted_iota(jnp.int32, (TK, TK), 1)
    suffix = (row >= col).astype(BF16)
    q = (q_ref[...].astype(F32) * (SB_DH ** -0.5 * LOG2E)).astype(BF16)
    bias = bias_ref[...] * LOG2E
    head = [slice(h * SB_DH, (h + 1) * SB_DH) for h in range(SB_HEADS)]

    def run(masked):
        if masked:
            qpos = lax.broadcasted_iota(jnp.int32, (TQ, TK), 0) + i * TQ
            kpos = lax.broadcasted_iota(jnp.int32, (TQ, TK), 1) + j * TK
            valid = kpos < qpos
        for g in range(0, SB_HEADS, SB_HEAD_GROUP):
            hh = range(g, g + SB_HEAD_GROUP)
            zs = [_mm(q[:, head[h]], kt_ref[head[h], :].astype(BF16)) + bias[:, h * SB_DH:h * SB_DH + 1] for h in hh]
            sps = [_softplus2(z) for z in zs]
            if masked:
                sps = [jnp.where(valid, sp, 0.0) for sp in sps]
            rests = [_mm(sp.astype(BF16), suffix) for sp in sps]
            rs = [r_ref[:, h:h + 1] for h in hh]
            probs = [jnp.exp2(z - rest - r) for z, rest, r in zip(zs, rests, rs)]
            if masked:
                probs = [jnp.where(valid, a, 0.0) for a in probs]
            for h, a, r, rest in zip(hh, probs, rs, rests):
                acc_ref[h] += _nt(a.astype(BF16), vt_ref[head[h], :].astype(BF16))
                r_ref[:, h:h + 1] = r + rest[:, 0:1]

    on_diagonal = j >= ratio * i
    pl.when(on_diagonal)(lambda: run(True))
    pl.when(jnp.logical_not(on_diagonal))(lambda: run(False))

    @pl.when(j == 0)
    def _():
        for h in range(SB_HEADS):
            o_ref[:, head[h]] = acc_ref[h].astype(o_ref.dtype)


def _sb_prompt(main, kbuf, vbuf, bias_row, layer, *, batch, seq, tq, tk):
    nq = seq // tq
    ratio = tq // tk
    qi = np.concatenate([np.full(ratio * (i + 1), i) for i in range(nq)]).astype(np.int32)
    kj = np.concatenate([np.arange(ratio * (i + 1) - 1, -1, -1) for i in range(nq)]).astype(np.int32)
    t = batch * seq
    kern = functools.partial(_sb_prompt_kernel, TQ=tq, TK=tk)
    kv_spec = pl.BlockSpec((None, None, SB_HEADS * SB_DH, tk), lambda b, p, qi, kj: (layer, b, 0, kj[p]))
    grid_spec = pltpu.PrefetchScalarGridSpec(
        num_scalar_prefetch=2,
        grid=(batch, len(qi)),
        in_specs=[
            pl.BlockSpec((tq, 512), lambda b, p, qi, kj: (b * nq + qi[p], BLK_SB_Q)),
            kv_spec,
            kv_spec,
            pl.BlockSpec((1, 512), lambda b, p, qi, kj: (0, 0)),
        ],
        out_specs=pl.BlockSpec((tq, 512), lambda b, p, qi, kj: (b * nq + qi[p], 0)),
        scratch_shapes=[pltpu.VMEM((SB_HEADS, tq, SB_DH), F32), pltpu.VMEM((tq, LANES), F32)],
    )
    return pl.pallas_call(
        kern,
        grid_spec=grid_spec,
        out_shape=jax.ShapeDtypeStruct((t, 512), BF16),
        compiler_params=_params(("arbitrary", "arbitrary")),
        name="sb_prompt",
    )(jnp.asarray(qi), jnp.asarray(kj), main, kbuf, vbuf, bias_row)


def _sb_sample_kernel(pt_ref, q_ref, bias_ref, kn_ref, vn_ref, *rest, PP, PS, NQ):
    kp = rest[:PP]
    vp = rest[PP:2 * PP]
    o_ref, acc_ref, r_ref = rest[2 * PP:]
    s = pl.program_id(1)
    hq = SB_HEADS * NQ
    q = (q_ref[...] * (SB_DH ** -0.5 * LOG2E)).astype(BF16)
    bias = bias_ref[...] * LOG2E
    group = 2 if PP % 2 == 0 else 1
    row = lax.broadcasted_iota(jnp.int32, (group * PS, group * PS), 0)
    col = lax.broadcasted_iota(jnp.int32, (group * PS, group * PS), 1)
    suffix = (row >= col).astype(BF16)

    def logits(get_kt):
        return jnp.concatenate(
            [_mm(q[:, h * SB_DH:(h + 1) * SB_DH], get_kt(h).astype(BF16)) for h in range(SB_HEADS)], axis=0) + bias

    def suffix_sum(sp):
        nk = sp.shape[1]
        return _mm(sp.astype(BF16), suffix[:nk, :nk])

    def attend(z, rest_, r, get_vt, valid):
        a = jnp.exp2(z - rest_ - r)
        if valid is not None:
            a = jnp.where(valid, a, 0.0)
        for h in range(SB_HEADS):
            rows = slice(h * NQ, (h + 1) * NQ)
            acc_ref[rows, :] += _nt(a[rows, :].astype(BF16), get_vt(h).astype(BF16))
        return r + rest_[:, 0:1]

    @pl.when(s == 0)
    def _():
        acc_ref[...] = jnp.zeros_like(acc_ref)
        key = lax.broadcasted_iota(jnp.int32, (hq, PS), 1)
        qpos = lax.broadcasted_iota(jnp.int32, (hq, PS), 0) % NQ
        valid = key < qpos
        z = logits(lambda h: kn_ref[h * SB_DH:(h + 1) * SB_DH, :])
        rest_ = suffix_sum(jnp.where(valid, _softplus2(z), 0.0))
        r = attend(z, rest_, 0.0, lambda h: vn_ref[h * SB_DH:(h + 1) * SB_DH, :], valid)
        r_ref[...] = jnp.broadcast_to(r, r_ref.shape)

    def pages(refs, p0):
        return lambda h: jnp.concatenate([refs[p][h] for p in range(p0 + group - 1, p0 - 1, -1)], axis=1)

    starts = range(0, PP, group)
    zs = [logits(pages(kp, p0)) for p0 in starts]
    rests = [suffix_sum(_softplus2(z)) for z in zs]
    r = r_ref[:, 0:1]
    for z, rest_, p0 in zip(zs, rests, starts):
        r = attend(z, rest_, r, pages(vp, p0), None)
    r_ref[...] = jnp.broadcast_to(r, r_ref.shape)

    @pl.when(s == pl.num_programs(1) - 1)
    def _():
        for h in range(SB_HEADS):
            o_ref[:, h * SB_DH:(h + 1) * SB_DH] = acc_ref[h * NQ:(h + 1) * NQ, :]


def _sb_sample(page_table, q, bias_col, k_new_pad, v_new_pad, cache_k, cache_v, layer, *, pages_per_step):
    bs, n_pages = page_table.shape
    ps = cache_k.shape[-1]
    nq = q.shape[1]
    pp = pages_per_step
    steps = n_pages // pp

    def page_spec(r):
        def idx(b, s, pt):
            return (layer, pt[b * n_pages + (n_pages - 1 - (s * pp + r))], 0, 0, 0)
        return pl.BlockSpec((None, None, SB_HEADS, SB_DH, ps), idx)

    kern = functools.partial(_sb_sample_kernel, PP=pp, PS=ps, NQ=nq)
    grid_spec = pltpu.PrefetchScalarGridSpec(
        num_scalar_prefetch=1,
        grid=(bs, steps),
        in_specs=[
            pl.BlockSpec((None, nq, 512), lambda b, s, pt: (b, 0, 0)),
            pl.BlockSpec((SB_HEADS * nq, 1), lambda b, s, pt: (0, 0)),
            pl.BlockSpec((None, 512, ps), lambda b, s, pt: (b, 0, 0)),
            pl.BlockSpec((None, 512, ps), lambda b, s, pt: (b, 0, 0)),
        ] + [page_spec(r) for r in range(pp)] + [page_spec(r) for r in range(pp)],
        out_specs=pl.BlockSpec((None, nq, 512), lambda b, s, pt: (b, 0, 0)),
        scratch_shapes=[pltpu.VMEM((SB_HEADS * nq, SB_DH), F32), pltpu.VMEM((SB_HEADS * nq, LANES), F32)],
    )
    return pl.pallas_call(
        kern,
        grid_spec=grid_spec,
        out_shape=jax.ShapeDtypeStruct((bs, nq, 512), F32),
        compiler_params=_params(("arbitrary", "arbitrary")),
        name="sb_sample",
    )(page_table.reshape(-1), q, bias_col, k_new_pad, v_new_pad, *([cache_k] * pp), *([cache_v] * pp))


def _merge_kernel(hm_ref, hg_ref, hs_ref, g0_ref, g1_ref, g2_ref, x_ref, gate_ref, wb_ref, wo_ref, o_ref):
    merged = jax.nn.sigmoid(g0_ref[...].astype(F32)) * _mm(hm_ref[...].astype(BF16), wb_ref[0])
    merged += jax.nn.sigmoid(g1_ref[...].astype(F32)) * _mm(hg_ref[...].astype(BF16), wb_ref[1])
    merged += jax.nn.sigmoid(g2_ref[...].astype(F32)) * _mm(hs_ref[...].astype(BF16), wb_ref[2])
    mix = _mm(merged.astype(BF16), wo_ref[...])
    o_ref[...] = x_ref[...] + gate_ref[...] * mix


def _merge(hm, hg, hs, main, x, gate, w_branch, w_out, layer, *, tm, rows_per_group):
    t, d = x.shape
    r = gate.shape[1]
    tiles_per_group = rows_per_group // tm
    tok = lambda blk: (lambda i: (i, blk))
    return pl.pallas_call(
        _merge_kernel,
        grid=(t // tm,),
        in_specs=[
            pl.BlockSpec((tm, 512), tok(0)),
            pl.BlockSpec((tm, 512), tok(0)),
            pl.BlockSpec((tm, 512), tok(0)),
            pl.BlockSpec((tm, d), tok(0)),
            pl.BlockSpec((tm, d), tok(1)),
            pl.BlockSpec((tm, d), tok(2)),
            pl.BlockSpec((tm, d), tok(0)),
            pl.BlockSpec((None, r, d), lambda i: (i // tiles_per_group, 0, 0)),
            pl.BlockSpec((None, N_BRANCH, BRANCH_W, d), lambda i: (layer, 0, 0, 0)),
            pl.BlockSpec((None, d, d), lambda i: (layer, 0, 0)),
        ],
        out_specs=pl.BlockSpec((tm, d), tok(0)),
        out_shape=jax.ShapeDtypeStruct((t, d), F32),
        compiler_params=_params(("arbitrary",)),
        name="branch_merge",
    )(hm, hg, hs, main, main, main, x, gate, w_branch, w_out)


FF_CHUNK = 256


def _ffn_kernel(x_ref, nw_ref, sc_ref, sh_ref, gate_ref, wu_ref, cw_ref, cb_ref, wd_ref, p1_ref, p2_ref,
                o_ref, u_ref, carry_ref, acc_ref, *, TM, SEQ_IN_TILE):
    x = x_ref[...]
    y = x * lax.rsqrt(jnp.mean(x * x, axis=-1, keepdims=True) + EPS)
    hb = (y * nw_ref[...] * (1.0 + sc_ref[...]) + sh_ref[...]).astype(BF16)
    row = lax.broadcasted_iota(jnp.int32, (TM, FF_CHUNK), 0)
    if SEQ_IN_TILE is None:
        @pl.when(pl.program_id(1) == 0)
        def _():
            carry_ref[...] = p1_ref[...]
        pos = row
    else:
        pos = row % SEQ_IN_TILE
    acc_ref[...] = jnp.zeros_like(acc_ref)
    for c in range(D_FF // FF_CHUNK):
        cs = slice(c * FF_CHUNK, (c + 1) * FF_CHUNK)
        vs = slice(D_FF + c * FF_CHUNK, D_FF + (c + 1) * FF_CHUNK)
        u = _mm(hb, wu_ref[:, cs])
        val = _mm(hb, wu_ref[:, vs])
        if SEQ_IN_TILE is None:
            prev = carry_ref[:, cs]
            prev1 = prev[SUBLANES - 1:SUBLANES, :]
            prev2 = jnp.where(row == 0, prev[SUBLANES - 2:SUBLANES - 1, :], prev1)
            carry_ref[:, cs] = u[TM - SUBLANES:TM, :]
            u_ref[:, cs] = u[TM - SUBLANES:TM, :]
        else:
            prev1 = p1_ref[:, cs]
            prev2 = p2_ref[:, cs]
            u_ref[:, cs] = u
        u_m1 = jnp.where(pos >= 1, pltpu.roll(u, 1, axis=0), prev1)
        u_m2 = jnp.where(pos >= 2, pltpu.roll(u, 2, axis=0), prev2)
        cw = cw_ref[:, cs]
        conv = cb_ref[:, cs] + cw[0:1, :] * u_m2 + cw[1:2, :] * u_m1 + cw[2:3, :] * u
        act = 0.5 * conv * (1.0 + lax.erf(conv * (2.0 ** -0.5))) * val
        acc_ref[...] += _mm(act.astype(BF16), wd_ref[cs, :])
    o_ref[...] = x + gate_ref[...] * acc_ref[...]


def _ffn(x, nw, sc, sh, gate, w_up, conv_w, conv_b, w_down, p1, p2, layer, *, batch, seq, tm, seq_in_tile):
    t, d = x.shape
    r = sc.shape[1]
    per_tile = seq_in_tile is not None
    if per_tile:
        grid = (1, t // tm)
        rowmap = lambda b, i: (i, 0)
        grp = lambda b, i: (i, 0, 0)
        pspec = pl.BlockSpec((tm, D_FF), rowmap)
        uspec = pl.BlockSpec((tm, D_FF), rowmap)
        ushape = jax.ShapeDtypeStruct((t, D_FF), F32)
    else:
        nt = seq // tm
        grid = (batch, nt)
        rowmap = lambda b, i: (b * nt + i, 0)
        grp = lambda b, i: (b, 0, 0)
        pspec = pl.BlockSpec((None, SUBLANES, D_FF), lambda b, i: (b, 0, 0))
        uspec = pl.BlockSpec((None, SUBLANES, D_FF), lambda b, i: (b, 0, 0))
        ushape = jax.ShapeDtypeStruct((batch, SUBLANES, D_FF), F32)
    const3 = lambda b, i: (layer, 0, 0)
    once = pl.Buffered(1)
    kern = functools.partial(_ffn_kernel, TM=tm, SEQ_IN_TILE=seq_in_tile)
    return pl.pallas_call(
        kern,
        grid=grid,
        in_specs=[
            pl.BlockSpec((tm, d), rowmap),
            pl.BlockSpec((None, 1, d), const3),
            pl.BlockSpec((None, r, d), grp),
            pl.BlockSpec((None, r, d), grp),
            pl.BlockSpec((None, r, d), grp),
            pl.BlockSpec((None, d, 2 * D_FF), const3, pipeline_mode=once),
            pl.BlockSpec((None, CONV_W, D_FF), const3),
            pl.BlockSpec((None, 1, D_FF), const3),
            pl.BlockSpec((None, D_FF, d), const3, pipeline_mode=once),
            pspec,
            pspec,
        ],
        out_specs=[pl.BlockSpec((tm, d), rowmap), uspec],
        out_shape=[jax.ShapeDtypeStruct((t, d), F32), ushape],
        scratch_shapes=[pltpu.VMEM((SUBLANES, D_FF), F32), pltpu.VMEM((tm, d), F32)],
        compiler_params=_params(("arbitrary", "arbitrary")),
        name="conv_ffn",
    )(x, nw, sc, sh, gate, w_up, conv_w, conv_b, w_down, p1, p2)


def _final_norm_kernel(x_ref, w_ref, o_ref):
    x = x_ref[...]
    o_ref[...] = x * lax.rsqrt(jnp.mean(x * x, axis=-1, keepdims=True) + EPS) * w_ref[...]


def _final_norm(x, w, *, tm):
    t, d = x.shape
    return pl.pallas_call(
        _final_norm_kernel,
        grid=(t // tm,),
        in_specs=[pl.BlockSpec((tm, d), lambda i: (i, 0)), pl.BlockSpec((1, d), lambda i: (0, 0))],
        out_specs=pl.BlockSpec((tm, d), lambda i: (i, 0)),
        out_shape=jax.ShapeDtypeStruct((t, d), F32),
        compiler_params=_params(("arbitrary",)),
        name="final_norm",
    )(x, w)


def _prep_weights(w_in, gla_w_gate, w_branch, w_out, w_up, w_down):
    widths = (256, 256, 512, 4, 4, 512, 256, 256, 512, GLA_RANK, 512, 512, 512, 512, N_BRANCH * D_MODEL)
    offs = np.concatenate([[0], np.cumsum(widths)])
    names = ("ml_q", "ml_k", "ml_v", "ml_i", "ml_f", "ml_o", "gla_q", "gla_k", "gla_v", "gla_lr", "gla_o",
             "sb_q", "sb_k", "sb_v", "gates")
    part = {nm: w_in[:, :, int(offs[i]):int(offs[i + 1])] for i, nm in enumerate(names)}
    order = ("gates", "ml_v", "ml_o", "gla_v", "gla_o", "sb_q", "ml_q", "ml_k", "gla_q", "gla_k")
    w_main = jnp.concatenate([part[nm] for nm in order], axis=-1).astype(BF16)
    w_kt = jnp.swapaxes(part["sb_k"], 1, 2).astype(BF16)
    w_vt = jnp.swapaxes(part["sb_v"], 1, 2).astype(BF16)
    small = jnp.concatenate([part["ml_i"], part["ml_f"], part["gla_lr"]], axis=-1)
    w_small = jnp.pad(small, ((0, 0), (0, 0), (0, SMALL_W - small.shape[-1]))).astype(BF16)
    gates_t = jnp.swapaxes(jnp.concatenate([part["ml_i"], part["ml_f"]], axis=-1), 1, 2)
    w_small_t = jnp.pad(gates_t, ((0, 0), (0, SMALL_T_ROWS - gates_t.shape[1]), (0, 0))).astype(BF16)
    wg_pad = jnp.pad(gla_w_gate, ((0, 0), (2 * ML_HEADS, SMALL_W - 2 * ML_HEADS - GLA_RANK), (0, 0))).astype(BF16)
    return ((w_main, w_small, w_small_t, w_kt, w_vt), wg_pad, w_branch.astype(BF16), w_out.astype(BF16),
            w_up.astype(BF16), w_down.astype(BF16))


def _run_group(x, mod, states, paged, weights, params, *, tm, scan_len, sb_block):
    (w_proj, wg_pad, w_branch, w_out, w_up, w_down) = weights
    (norm1_w, norm2_w, ml_b_i, ml_b_f, ml_norm_w, gla_b_gate, gla_norm_w, sb_bias, conv_w, conv_b) = params
    ml_c0, ml_n0, ml_m0, gla_s0, conv0 = states
    batch, seq, d = x.shape
    depth = w_proj[0].shape[0]
    t = batch * seq
    per_seq = seq < tm
    xt = x.reshape(t, d)
    scan_dtype = F32 if per_seq else BF16
    kv_groups = 1 if per_seq else batch
    kbuf = jnp.zeros((depth, kv_groups, SB_HEADS * SB_DH, t // kv_groups), F32)
    vbuf = jnp.zeros((depth, kv_groups, SB_HEADS * SB_DH, t // kv_groups), F32)
    mcs, mns, mms, gss, bufs = [], [], [], [], []
    for l in range(depth):
        def modrow(i):
            m = mod[l, :, i, :]
            if per_seq:
                return jnp.repeat(m, seq, axis=0).reshape(t // tm, tm, d)
            return m.reshape(batch, 1, d)
        sh1, sc1, g1, sh2, sc2, g2 = [modrow(i) for i in range(6)]
        rows_per_group = tm if per_seq else seq
        tm_proj = tm if per_seq else math.gcd(seq, 2 * tm)
        main, small, small_t, kbuf, vbuf = _norm_proj(xt, norm1_w, sc1, sh1, w_proj, kbuf, vbuf, l,
                                                      tm=tm_proj, rows_per_group=rows_per_group,
                                                      main_dtype=scan_dtype)
        if per_seq:
            small_t3 = small_t.reshape(SMALL_T_ROWS, batch, seq).transpose(1, 0, 2)
        else:
            small_t3 = small_t.reshape(1, SMALL_T_ROWS, t)
        b_row = jnp.zeros((1, SMALL_W), F32).at[0, 0:ML_HEADS].set(ml_b_i[l]).at[0, ML_HEADS:2 * ML_HEADS].set(ml_b_f[l])
        b_col = jnp.zeros((SMALL_T_ROWS, 1), F32).at[0:ML_HEADS, 0].set(ml_b_i[l]).at[ML_HEADS:2 * ML_HEADS, 0].set(ml_b_f[l])
        hm, mc, mn, mm = _mlstm(main, small, small_t3, b_row, b_col, ml_norm_w[l].reshape(1, 512),
                                ml_c0[l], ml_n0[l], ml_m0[l].reshape(batch, ML_HEADS, 1),
                                batch=batch, seq=seq, L=scan_len, out_dtype=scan_dtype, per_seq=per_seq)
        hg, gst = _gla(main, small, wg_pad, gla_b_gate, gla_norm_w, jnp.swapaxes(gla_s0[l], -1, -2), l,
                       batch=batch, seq=seq, L=scan_len, out_dtype=scan_dtype)
        if paged is None:
            bias_row = jnp.repeat(sb_bias[l], SB_DH).reshape(1, 512)
            hs = _sb_prompt(main, kbuf, vbuf, bias_row, l, batch=batch, seq=seq, tq=sb_block[0], tk=sb_block[1])
        else:
            cache_kt, cache_vt, page_table = paged
            ps = cache_kt.shape[-1]
            q = main[:, BLK_SB_Q * 512:(BLK_SB_Q + 1) * 512].reshape(batch, seq, 512)
            bias_col = jnp.repeat(sb_bias[l], seq).reshape(SB_HEADS * seq, 1)
            new_t = lambda buf: jnp.pad(buf[l, 0].reshape(-1, batch, seq).transpose(1, 0, 2),
                                        ((0, 0), (0, 0), (0, ps - seq)))
            hs = _sb_sample(page_table, q, bias_col, new_t(kbuf), new_t(vbuf), cache_kt, cache_vt, l,
                            pages_per_step=math.gcd(page_table.shape[1], 16)).reshape(t, 512)
        xt = _merge(hm, hg, hs, main, xt, g1, w_branch, w_out, l, tm=tm, rows_per_group=rows_per_group)
        if per_seq:
            c0 = conv0[l]
            zero = jnp.zeros((batch, seq - 1, D_FF), F32)
            p1 = jnp.concatenate([c0[:, 1:2], zero], axis=1).reshape(t, D_FF)
            p2 = jnp.concatenate([c0[:, 0:1], c0[:, 1:2], zero[:, 1:]], axis=1).reshape(t, D_FF)
            xt, u = _ffn(xt, norm2_w, sc2, sh2, g2, w_up, conv_w, conv_b, w_down, p1, p2, l,
                         batch=batch, seq=seq, tm=tm, seq_in_tile=seq)
            full = jnp.concatenate([conv0[l], u.reshape(batch, seq, D_FF)], axis=1)
            buf = full[:, seq:]
        else:
            p = jnp.pad(conv0[l], ((0, 0), (SUBLANES - (CONV_W - 1), 0), (0, 0)))
            xt, tail = _ffn(xt, norm2_w, sc2, sh2, g2, w_up, conv_w, conv_b, w_down, p, p, l,
                            batch=batch, seq=seq, tm=tm, seq_in_tile=None)
            buf = tail[:, SUBLANES - (CONV_W - 1):]
        mcs.append(mc)
        mns.append(mn)
        mms.append(mm.reshape(batch, ML_HEADS))
        gss.append(jnp.swapaxes(gst, -1, -2))
        bufs.append(buf)
    st = jnp.stack

    def kv_out(buf):
        b6 = buf.reshape(depth, kv_groups, SB_HEADS, SB_DH, batch // kv_groups, seq)
        return b6.transpose(0, 1, 4, 5, 2, 3).reshape(depth, batch, seq, SB_HEADS, SB_DH)

    return xt, kv_out(kbuf), kv_out(vbuf), st(mcs), st(mns), st(mms), st(gss), st(bufs)


def kernel(x_prompt, x_sample, cache_k, cache_v, state_mlstm_c, state_mlstm_n, state_mlstm_m, state_gla, state_conv, page_table, c_prompt, c_sample, w_ada, b_ada, norm1_w, norm2_w, w_in, ml_b_i, ml_b_f, ml_norm_w, gla_w_gate, gla_b_gate, gla_norm_w, sb_bias, w_branch, w_out, w_up, conv_w, conv_b, w_down, final_norm_w):
    depth = w_in.shape[0]
    bp, sp, d = x_prompt.shape
    bs, ss, _ = x_sample.shape
    weights = _prep_weights(w_in, gla_w_gate, w_branch, w_out, w_up, w_down)
    params = (norm1_w.reshape(depth, 1, d), norm2_w.reshape(depth, 1, d), ml_b_i, ml_b_f, ml_norm_w,
              gla_b_gate.reshape(depth, 1, -1), gla_norm_w.reshape(depth, 1, -1), sb_bias, conv_w,
              conv_b.reshape(depth, 1, -1))
    mod = _modulation(jnp.concatenate([c_prompt, c_sample], axis=0), w_ada, b_ada)
    mod_p = mod[:, :bp].reshape(depth, bp, 6, d)
    mod_s = mod[:, bp:].reshape(depth, bs, 6, d)

    zeros = lambda *shape: jnp.zeros((depth, bp) + shape, F32)
    states_p = (zeros(ML_HEADS, ML_DK, ML_DV), zeros(ML_HEADS, ML_DK),
                jnp.full((depth, bp, ML_HEADS), NEG_BIG, F32), zeros(GLA_HEADS, GLA_DK, GLA_DV),
                zeros(CONV_W - 1, D_FF))
    tm_p = math.gcd(sp, 512)
    scan_p = math.gcd(sp, 256)
    out_p = _run_group(x_prompt, mod_p, states_p, None, weights, params, tm=tm_p, scan_len=scan_p,
                       sb_block=(math.gcd(sp, 512), math.gcd(sp, 256)))

    states_s = (state_mlstm_c, state_mlstm_n, state_mlstm_m, state_gla, state_conv)
    cache_kt = jnp.transpose(cache_k, (0, 1, 3, 4, 2))
    cache_vt = jnp.transpose(cache_v, (0, 1, 3, 4, 2))
    out_s = _run_group(x_sample, mod_s, states_s, (cache_kt, cache_vt, page_table), weights, params,
                       tm=bs * ss, scan_len=ss, sb_block=None)

    y_p = _final_norm(out_p[0], final_norm_w.reshape(1, d), tm=tm_p).reshape(bp, sp, d)
    y_s = _final_norm(out_s[0], final_norm_w.reshape(1, d), tm=bs * ss).reshape(bs, ss, d)
    (_, k_p, v_p, mc_p, mn_p, mm_p, g_p, cb_p) = out_p
    (_, k_s, v_s, mc_s, mn_s, mm_s, g_s, cb_s) = out_s
    return (y_p, y_s, k_p, v_p, k_s, v_s, mc_p, mn_p, mm_p, mc_s, mn_s, mm_s, g_p, g_s, cb_p, cb_s)
```

```python
import functools
import math

import numpy as np
import jax
import jax.numpy as jnp
from jax import lax
from jax.experimental import pallas as pl
from jax.experimental.pallas import tpu as pltpu

F32 = jnp.float32
BF16 = jnp.bfloat16

D_MODEL = 1024
ML_HEADS, ML_DK, ML_DV = 4, 64, 128
GLA_HEADS, GLA_DK, GLA_DV = 4, 64, 128
GLA_RANK = 16
GLA_TAU = 16.0
SB_HEADS, SB_DH = 8, 64
BRANCH_W = 512
N_BRANCH = 3
D_FF = 2816
CONV_W = 3
EPS = 1e-6
NEG_BIG = -1e30
SCAN_CHUNK = 64

LANES = 128
SUBLANES = 8
VMEM_LIMIT = 56 * 1024 * 1024

N_MAIN = 6656
BLK_ML_V, BLK_ML_O, BLK_GLA_V, BLK_GLA_O, BLK_SB_Q = 6, 7, 8, 9, 10
BLK_ML_Q, BLK_ML_K, BLK_GLA_Q, BLK_GLA_K = 22, 23, 24, 25
MAIN_TN = 1664
LOG2E = 1.4426950408889634
SMALL_W = 128
SMALL_T_ROWS = 32


def _nt(a, b, precision=None):
    return lax.dot_general(a, b, (((1,), (1,)), ((), ())), preferred_element_type=F32, precision=precision)


def _tn(a, b):
    return lax.dot_general(a, b, (((0,), (0,)), ((), ())), preferred_element_type=F32)


def _mm(a, b, precision=None):
    return jnp.dot(a, b, preferred_element_type=F32, precision=precision)


def _cumsum_mm(tri, x, *, tri_first):
    t = tri.astype(BF16)
    hi = x.astype(BF16)
    r1 = x - hi.astype(F32)
    mid = r1.astype(BF16)
    lo = (r1 - mid.astype(F32)).astype(BF16)
    parts = (hi, mid, lo)
    terms = [_mm(t, p) if tri_first else _mm(p, t) for p in parts]
    return terms[0] + (terms[1] + terms[2])


def _log_sigmoid(x):
    return jnp.minimum(x, 0.0) - jnp.log1p(jnp.exp(-jnp.abs(x)))


def _softplus2(x):
    return jnp.maximum(x, 0.0) + jnp.log2(1.0 + jnp.exp2(-jnp.abs(x)))


def _params(sem):
    return pltpu.CompilerParams(dimension_semantics=sem, vmem_limit_bytes=VMEM_LIMIT)


def _mod_kernel(c_ref, w_ref, b_ref, o_ref):
    c = c_ref[...]
    s = c * jax.nn.sigmoid(c)
    o_ref[...] = _mm(s.astype(BF16), w_ref[...].astype(BF16)) + b_ref[...]


def _modulation(c_all, w_ada, b_ada):
    depth, d, n = w_ada.shape
    nb = c_all.shape[0]
    tn = 1536
    return pl.pallas_call(
        _mod_kernel,
        grid=(depth, n // tn),
        in_specs=[
            pl.BlockSpec((nb, d), lambda l, j: (0, 0)),
            pl.BlockSpec((None, d, tn), lambda l, j: (l, 0, j)),
            pl.BlockSpec((None, 1, tn), lambda l, j: (l, 0, j)),
        ],
        out_specs=pl.BlockSpec((None, nb, tn), lambda l, j: (l, 0, j)),
        out_shape=jax.ShapeDtypeStruct((depth, nb, n), F32),
        compiler_params=_params(("arbitrary", "arbitrary")),
        name="adaln_mod",
    )(c_all, w_ada, b_ada.reshape(depth, 1, n))


def _normmm_kernel(x_ref, nw_ref, sc_ref, sh_ref, w_ref, ws_ref, wt_ref, wk_ref, wv_ref, kbuf_ref, vbuf_ref,
                   o_ref, os_ref, ot_ref, kt_ref, vt_ref, h_scr):
    del kbuf_ref, vbuf_ref

    @pl.when(pl.program_id(1) == 0)
    def _():
        x = x_ref[...]
        y = x * lax.rsqrt(jnp.mean(x * x, axis=-1, keepdims=True) + EPS)
        h = y * nw_ref[...] * (1.0 + sc_ref[...]) + sh_ref[...]
        hb = h.astype(BF16)
        h_scr[...] = hb
        os_ref[...] = _mm(hb, ws_ref[...])
        ot_ref[...] = _nt(wt_ref[...], hb)
        kt_ref[...] = _nt(wk_ref[...], hb)
        vt_ref[...] = _nt(wv_ref[...], hb)

    o_ref[...] = _mm(h_scr[...], w_ref[...]).astype(o_ref.dtype)


def _norm_proj(x, nw, sc, sh, weights, kbuf, vbuf, layer, *, tm, rows_per_group, main_dtype):
    w_main, w_small, w_small_t, w_kt, w_vt = weights
    t, d = x.shape
    n = w_main.shape[-1]
    tn = MAIN_TN
    r = sc.shape[1]
    tiles_per_group = rows_per_group // tm
    grp = lambda i, j: (i // tiles_per_group, 0, 0)
    kv_rows = w_kt.shape[1]
    kv_map = lambda i, j: (layer, i // tiles_per_group, 0, i % tiles_per_group)
    wmap = lambda i, j: (layer, 0, 0)
    return pl.pallas_call(
        _normmm_kernel,
        grid=(t // tm, n // tn),
        in_specs=[
            pl.BlockSpec((tm, d), lambda i, j: (i, 0)),
            pl.BlockSpec((None, 1, d), wmap),
            pl.BlockSpec((None, r, d), grp),
            pl.BlockSpec((None, r, d), grp),
            pl.BlockSpec((None, d, tn), lambda i, j: (layer, 0, j)),
            pl.BlockSpec((None, d, SMALL_W), wmap),
            pl.BlockSpec((None, SMALL_T_ROWS, d), wmap),
            pl.BlockSpec((None, kv_rows, d), wmap),
            pl.BlockSpec((None, kv_rows, d), wmap),
            pl.BlockSpec(memory_space=pl.ANY),
            pl.BlockSpec(memory_space=pl.ANY),
        ],
        out_specs=[
            pl.BlockSpec((tm, tn), lambda i, j: (i, j)),
            pl.BlockSpec((tm, SMALL_W), lambda i, j: (i, 0)),
            pl.BlockSpec((SMALL_T_ROWS, tm), lambda i, j: (0, i)),
            pl.BlockSpec((None, None, kv_rows, tm), kv_map),
            pl.BlockSpec((None, None, kv_rows, tm), kv_map),
        ],
        out_shape=[
            jax.ShapeDtypeStruct((t, n), main_dtype),
            jax.ShapeDtypeStruct((t, SMALL_W), F32),
            jax.ShapeDtypeStruct((SMALL_T_ROWS, t), F32),
            jax.ShapeDtypeStruct(kbuf.shape, F32),
            jax.ShapeDtypeStruct(vbuf.shape, F32),
        ],
        input_output_aliases={9: 3, 10: 4},
        scratch_shapes=[pltpu.VMEM((tm, d), BF16)],
        compiler_params=_params(("arbitrary", "arbitrary")),
        name="norm_in_proj",
    )(x, nw, sc, sh, w_main, w_small, w_small_t, w_kt, w_vt, kbuf, vbuf)


def _mlstm_kernel(q_ref, k_ref, v_ref, o_ref, sc_ref, st_ref, brow_ref, bcol_ref, nw_ref, c0_ref, n0_ref, m0_ref,
                  hn_ref, c_ref, n_ref, m_ref, *, L):
    @pl.when(pl.program_id(1) == 0)
    def _():
        c_ref[...] = c0_ref[...]
        n_ref[...] = n0_ref[...]
        m_ref[...] = m0_ref[...]

    row = lax.broadcasted_iota(jnp.int32, (L, L), 0)
    col = lax.broadcasted_iota(jnp.int32, (L, L), 1)
    causal = col <= row
    small = sc_ref[...] + brow_ref[...]
    small_t = st_ref[...] + bcol_ref[...]
    bc_all = _cumsum_mm(causal, _log_sigmoid(small), tri_first=True)
    br_all = _cumsum_mm(row <= col, _log_sigmoid(small_t), tri_first=False)
    q = q_ref[...].astype(F32) * (ML_DK ** -0.5)
    k = k_ref[...].astype(F32)
    v = v_ref[...].astype(F32)
    og = o_ref[...].astype(F32)
    nw = nw_ref[...]
    heads = range(ML_HEADS)
    ks = [slice(h * ML_DK, (h + 1) * ML_DK) for h in heads]
    vs = [slice(h * ML_DV, (h + 1) * ML_DV) for h in heads]
    qb = [q[:, ks[h]].astype(BF16) for h in heads]
    vb = [v[:, vs[h]].astype(BF16) for h in heads]
    qk = [_nt(qb[h], k[:, ks[h]].astype(BF16)) for h in heads]
    b_col = [bc_all[:, ML_HEADS + h:ML_HEADS + h + 1] for h in heads]
    b_row = [br_all[ML_HEADS + h:ML_HEADS + h + 1, :] for h in heads]
    m_prev = [m_ref[h:h + 1, :] for h in heads]
    c_st = [c_ref[h] for h in heads]
    n_st = [n_ref[h:h + 1, :] for h in heads]
    dmat = [jnp.where(causal, b_col[h] - b_row[h] + small_t[h:h + 1, :], -jnp.inf) for h in heads]
    inter = [b_col[h] + m_prev[h] for h in heads]
    m_t = [jnp.maximum(inter[h], jnp.max(dmat[h], axis=-1, keepdims=True)) for h in heads]
    s = [qk[h] * jnp.exp(dmat[h] - m_t[h]) for h in heads]
    e_inter = [jnp.exp(inter[h] - m_t[h]) for h in heads]
    qc = [_mm(qb[h], c_st[h].astype(BF16)) for h in heads]
    num = [_mm(s[h].astype(BF16), vb[h]) + e_inter[h] * qc[h] for h in heads]
    den = [jnp.sum(s[h], axis=-1, keepdims=True)
           + e_inter[h] * jnp.sum(q[:, ks[h]] * n_st[h], axis=-1, keepdims=True) for h in heads]
    hh = [num[h] / jnp.maximum(jnp.abs(den[h]), jnp.exp(-m_t[h])) for h in heads]
    for h in heads:
        m_new = m_t[h][L - 1:L, :]
        b_last = b_col[h][L - 1:L, :]
        wk = jnp.exp(b_last - b_col[h] + small[:, h:h + 1] - m_new)
        decay = jnp.exp(b_last + m_prev[h] - m_new)
        kw = k[:, ks[h]] * wk
        c_ref[h] = decay * c_st[h] + _tn(kw.astype(BF16), vb[h])
        n_ref[h:h + 1, :] = decay * n_st[h] + jnp.sum(kw, axis=0, keepdims=True)
        m_ref[h:h + 1, :] = m_new
    for h in heads:
        y = hh[h] * lax.rsqrt(jnp.mean(hh[h] * hh[h], axis=-1, keepdims=True) + EPS) * nw[:, vs[h]]
        hn_ref[:, vs[h]] = (y * jax.nn.sigmoid(og[:, vs[h]])).astype(hn_ref.dtype)


def _small_t_spec(L, nc, per_seq):
    if per_seq:
        return pl.BlockSpec((None, SMALL_T_ROWS, L), lambda b, c: (b, 0, 0))
    return pl.BlockSpec((None, SMALL_T_ROWS, L), lambda b, c: (0, 0, b * nc + c))


def _mlstm(main, small, small_t3, b_row, b_col, nw, c0, n0, m0, *, batch, seq, L, out_dtype, per_seq):
    nc = seq // L
    t = batch * seq
    tok = lambda blk: (lambda b, c: (b * nc + c, blk))
    st = lambda b, c: (b, 0, 0)
    return dict(
        in_specs=[
            pl.BlockSpec((L, 256), tok(BLK_ML_Q)),
            pl.BlockSpec((L, 256), tok(BLK_ML_K)),
            pl.BlockSpec((L, 512), tok(BLK_ML_V)),
            pl.BlockSpec((L, 512), tok(BLK_ML_O)),
            pl.BlockSpec((L, SMALL_W), tok(0)),
            _small_t_spec(L, nc, per_seq),
            pl.BlockSpec((1, SMALL_W), lambda b, c: (0, 0)),
            pl.BlockSpec((SMALL_T_ROWS, 1), lambda b, c: (0, 0)),
            pl.BlockSpec((1, 512), lambda b, c: (0, 0)),
            pl.BlockSpec((None, ML_HEADS, ML_DK, ML_DV), lambda b, c: (b, 0, 0, 0)),
            pl.BlockSpec((None, ML_HEADS, ML_DK), st),
            pl.BlockSpec((None, ML_HEADS, 1), st),
        ],
        out_specs=[
            pl.BlockSpec((L, 512), tok(0)),
            pl.BlockSpec((None, ML_HEADS, ML_DK, ML_DV), lambda b, c: (b, 0, 0, 0)),
            pl.BlockSpec((None, ML_HEADS, ML_DK), st),
            pl.BlockSpec((None, ML_HEADS, 1), st),
        ],
        out_shape=[
            jax.ShapeDtypeStruct((t, 512), out_dtype),
            jax.ShapeDtypeStruct((batch, ML_HEADS, ML_DK, ML_DV), F32),
            jax.ShapeDtypeStruct((batch, ML_HEADS, ML_DK), F32),
            jax.ShapeDtypeStruct((batch, ML_HEADS, 1), F32),
        ],
        operands=(main, main, main, main, small, small_t3, b_row, b_col, nw, c0, n0, m0),
    )


def _gla_kernel(q_ref, k_ref, v_ref, o_ref, sc_ref, wg_ref, bg_ref, nw_ref, s0_ref, hn_ref, s_ref, *, L, LS):
    @pl.when(pl.program_id(1) == 0)
    def _():
        s_ref[...] = s0_ref[...]

    row = lax.broadcasted_iota(jnp.int32, (LS, LS), 0)
    col = lax.broadcasted_iota(jnp.int32, (LS, LS), 1)
    causal = col <= row
    la_all = _log_sigmoid(_mm(sc_ref[...].astype(BF16), wg_ref[...]) + bg_ref[...]) * (1.0 / GLA_TAU)
    nw = nw_ref[...]
    heads = range(GLA_HEADS)
    ks = [slice(h * GLA_DK, (h + 1) * GLA_DK) for h in heads]
    vs = [slice(h * GLA_DV, (h + 1) * GLA_DV) for h in heads]
    pre = []
    for c in range(L // LS):
        rs = slice(c * LS, (c + 1) * LS)
        bc = _cumsum_mm(causal, la_all[rs, :], tri_first=True)
        ref_row = bc[LS // 2:LS // 2 + 1, :]
        last = bc[LS - 1:LS, :]
        q = q_ref[rs, :].astype(F32) * (GLA_DK ** -0.5)
        k = k_ref[rs, :].astype(F32)
        qe = (q * jnp.exp(bc - ref_row)).astype(BF16)
        ke = (k * jnp.exp(ref_row - bc)).astype(BF16)
        qs = (q * jnp.exp(bc)).astype(BF16)
        kl = (k * jnp.exp(last - bc)).astype(BF16)
        el = jnp.exp(last)
        v = v_ref[rs, :].astype(F32)
        vb = [v[:, vs[h]].astype(BF16) for h in heads]
        av = [_mm(jnp.where(causal, _nt(qe[:, ks[h]], ke[:, ks[h]]), 0.0).astype(BF16), vb[h]) for h in heads]
        pre.append((rs, qs, kl, el, vb, av))
    state = [s_ref[h] for h in heads]
    outs = []
    for rs, qs, kl, el, vb, av in pre:
        o = [av[h] + _nt(qs[:, ks[h]], state[h].astype(BF16)) for h in heads]
        state = [el[:, ks[h]] * state[h] + _tn(vb[h], kl[:, ks[h]]) for h in heads]
        outs.append((rs, o))
    for h in heads:
        s_ref[h] = state[h]
    for rs, o in outs:
        og = o_ref[rs, :].astype(F32)
        for h in heads:
            y = o[h] * lax.rsqrt(jnp.mean(o[h] * o[h], axis=-1, keepdims=True) + EPS) * nw[:, vs[h]]
            g = og[:, vs[h]]
            hn_ref[rs, vs[h]] = (y * (g * jax.nn.sigmoid(g))).astype(hn_ref.dtype)


def _gla(main, small, wg_pad, bg, nw, s0t, layer, *, batch, seq, L, out_dtype):
    nc = seq // L
    t = batch * seq
    tok = lambda blk: (lambda b, c: (b * nc + c, blk))
    return dict(
        in_specs=[
            pl.BlockSpec((L, 256), tok(BLK_GLA_Q)),
            pl.BlockSpec((L, 256), tok(BLK_GLA_K)),
            pl.BlockSpec((L, 512), tok(BLK_GLA_V)),
            pl.BlockSpec((L, 512), tok(BLK_GLA_O)),
            pl.BlockSpec((L, SMALL_W), tok(0)),
            pl.BlockSpec((None, SMALL_W, 256), lambda b, c: (layer, 0, 0)),
            pl.BlockSpec((None, 1, 256), lambda b, c: (layer, 0, 0)),
            pl.BlockSpec((None, 1, 512), lambda b, c: (layer, 0, 0)),
            pl.BlockSpec((None, GLA_HEADS, GLA_DV, GLA_DK), lambda b, c: (b, 0, 0, 0)),
        ],
        out_specs=[
            pl.BlockSpec((L, 512), tok(0)),
            pl.BlockSpec((None, GLA_HEADS, GLA_DV, GLA_DK), lambda b, c: (b, 0, 0, 0)),
        ],
        out_shape=[
            jax.ShapeDtypeStruct((t, 512), out_dtype),
            jax.ShapeDtypeStruct((batch, GLA_HEADS, GLA_DV, GLA_DK), F32),
        ],
        operands=(main, main, main, main, small, wg_pad, bg, nw, s0t),
    )


def _scan_kernel(*refs, N_ML_IN, N_GLA_IN, N_ML_OUT, L, LS):
    ml_in = refs[:N_ML_IN]
    gla_in = refs[N_ML_IN:N_ML_IN + N_GLA_IN]
    outs = refs[N_ML_IN + N_GLA_IN:]
    _mlstm_kernel(*ml_in, *outs[:N_ML_OUT], L=L)
    _gla_kernel(*gla_in, *outs[N_ML_OUT:], L=L, LS=LS)


def _scans(ml, gla, *, batch, seq, L):
    kern = functools.partial(_scan_kernel, N_ML_IN=len(ml["in_specs"]), N_GLA_IN=len(gla["in_specs"]),
                             N_ML_OUT=len(ml["out_specs"]), L=L, LS=math.gcd(L, SCAN_CHUNK))
    return pl.pallas_call(
        kern,
        grid=(batch, seq // L),
        in_specs=ml["in_specs"] + gla["in_specs"],
        out_specs=ml["out_specs"] + gla["out_specs"],
        out_shape=ml["out_shape"] + gla["out_shape"],
        compiler_params=_params(("arbitrary", "arbitrary")),
        name="mlstm_gla_scan",
    )(*ml["operands"], *gla["operands"])


SB_HEAD_GROUP = 2


def _sb_prompt_kernel(qi_ref, kj_ref, q_ref, kt_ref, vt_ref, bias_ref, o_ref, acc_ref, r_ref, *, TQ, TK):
    p = pl.program_id(1)
    i = qi_ref[p]
    j = kj_ref[p]
    ratio = TQ // TK

    @pl.when(j == ratio * i + ratio - 1)
    def _():
        acc_ref[...] = jnp.zeros_like(acc_ref)
        r_ref[...] = jnp.zeros_like(r_ref)

    row = lax.broadcasted_iota(jnp.int32, (TK, TK), 0)
    col = lax.broadcasted_iota(jnp.int32, (TK, TK), 1)
    suffix = (row >= col).astype(BF16)
    q = (q_ref[...].astype(F32) * (SB_DH ** -0.5 * LOG2E)).astype(BF16)
    bias = bias_ref[...] * LOG2E
    head = [slice(h * SB_DH, (h + 1) * SB_DH) for h in range(SB_HEADS)]

    def run(masked):
        if masked:
            qpos = lax.broadcasted_iota(jnp.int32, (TQ, TK), 0) + i * TQ
            kpos = lax.broadcasted_iota(jnp.int32, (TQ, TK), 1) + j * TK
            valid = kpos < qpos
        for g in range(0, SB_HEADS, SB_HEAD_GROUP):
            hh = range(g, g + SB_HEAD_GROUP)
            zs = [_mm(q[:, head[h]], kt_ref[head[h], :].astype(BF16)) + bias[:, h * SB_DH:h * SB_DH + 1] for h in hh]
            sps = [_softplus2(z) for z in zs]
            if masked:
                sps = [jnp.where(valid, sp, 0.0) for sp in sps]
            rests = [_mm(sp.astype(BF16), suffix) for sp in sps]
            rs = [r_ref[:, h:h + 1] for h in hh]
            probs = [jnp.exp2(z - rest - r) for z, rest, r in zip(zs, rests, rs)]
            if masked:
                probs = [jnp.where(valid, a, 0.0) for a in probs]
            for h, a, r, rest in zip(hh, probs, rs, rests):
                acc_ref[h] += _nt(a.astype(BF16), vt_ref[head[h], :].astype(BF16))
                r_ref[:, h:h + 1] = r + rest[:, 0:1]

    on_diagonal = j >= ratio * i
    pl.when(on_diagonal)(lambda: run(True))
    pl.when(jnp.logical_not(on_diagonal))(lambda: run(False))

    @pl.when(j == 0)
    def _():
        for h in range(SB_HEADS):
            o_ref[:, head[h]] = acc_ref[h].astype(o_ref.dtype)


def _sb_prompt(main, kbuf, vbuf, bias_row, layer, *, batch, seq, tq, tk):
    nq = seq // tq
    ratio = tq // tk
    qi = np.concatenate([np.full(ratio * (i + 1), i) for i in range(nq)]).astype(np.int32)
    kj = np.concatenate([np.arange(ratio * (i + 1) - 1, -1, -1) for i in range(nq)]).astype(np.int32)
    t = batch * seq
    kern = functools.partial(_sb_prompt_kernel, TQ=tq, TK=tk)
    kv_spec = pl.BlockSpec((None, None, SB_HEADS * SB_DH, tk), lambda b, p, qi, kj: (layer, b, 0, kj[p]))
    grid_spec = pltpu.PrefetchScalarGridSpec(
        num_scalar_prefetch=2,
        grid=(batch, len(qi)),
        in_specs=[
            pl.BlockSpec((tq, 512), lambda b, p, qi, kj: (b * nq + qi[p], BLK_SB_Q)),
            kv_spec,
            kv_spec,
            pl.BlockSpec((1, 512), lambda b, p, qi, kj: (0, 0)),
        ],
        out_specs=pl.BlockSpec((tq, 512), lambda b, p, qi, kj: (b * nq + qi[p], 0)),
        scratch_shapes=[pltpu.VMEM((SB_HEADS, tq, SB_DH), F32), pltpu.VMEM((tq, LANES), F32)],
    )
    return pl.pallas_call(
        kern,
        grid_spec=grid_spec,
        out_shape=jax.ShapeDtypeStruct((t, 512), BF16),
        compiler_params=_params(("arbitrary", "arbitrary")),
        name="sb_prompt",
    )(jnp.asarray(qi), jnp.asarray(kj), main, kbuf, vbuf, bias_row)


def _sb_sample_kernel(pt_ref, q_ref, bias_ref, kn_ref, vn_ref, *rest, PP, PS, NQ):
    kp = rest[:PP]
    vp = rest[PP:2 * PP]
    o_ref, acc_ref, r_ref = rest[2 * PP:]
    s = pl.program_id(1)
    hq = SB_HEADS * NQ
    q = (q_ref[...] * (SB_DH ** -0.5 * LOG2E)).astype(BF16)
    bias = bias_ref[...] * LOG2E
    group = 2 if PP % 2 == 0 else 1
    row = lax.broadcasted_iota(jnp.int32, (group * PS, group * PS), 0)
    col = lax.broadcasted_iota(jnp.int32, (group * PS, group * PS), 1)
    suffix = (row >= col).astype(BF16)

    def logits(get_kt):
        return jnp.concatenate(
            [_mm(q[:, h * SB_DH:(h + 1) * SB_DH], get_kt(h).astype(BF16)) for h in range(SB_HEADS)], axis=0) + bias

    def suffix_sum(sp):
        nk = sp.shape[1]
        return _mm(sp.astype(BF16), suffix[:nk, :nk])

    def attend(z, rest_, r, get_vt, valid):
        a = jnp.exp2(z - rest_ - r)
        if valid is not None:
            a = jnp.where(valid, a, 0.0)
        for h in range(SB_HEADS):
            rows = slice(h * NQ, (h + 1) * NQ)
            acc_ref[rows, :] += _nt(a[rows, :].astype(BF16), get_vt(h).astype(BF16))
        return r + rest_[:, 0:1]

    @pl.when(s == 0)
    def _():
        acc_ref[...] = jnp.zeros_like(acc_ref)
        key = lax.broadcasted_iota(jnp.int32, (hq, PS), 1)
        qpos = lax.broadcasted_iota(jnp.int32, (hq, PS), 0) % NQ
        valid = key < qpos
        z = logits(lambda h: kn_ref[h * SB_DH:(h + 1) * SB_DH, :])
        rest_ = suffix_sum(jnp.where(valid, _softplus2(z), 0.0))
        r = attend(z, rest_, 0.0, lambda h: vn_ref[h * SB_DH:(h + 1) * SB_DH, :], valid)
        r_ref[...] = jnp.broadcast_to(r, r_ref.shape)

    def pages(refs, p0):
        return lambda h: jnp.concatenate([refs[p][h] for p in range(p0 + group - 1, p0 - 1, -1)], axis=1)

    starts = range(0, PP, group)
    zs = [logits(pages(kp, p0)) for p0 in starts]
    rests = [suffix_sum(_softplus2(z)) for z in zs]
    r = r_ref[:, 0:1]
    for z, rest_, p0 in zip(zs, rests, starts):
        r = attend(z, rest_, r, pages(vp, p0), None)
    r_ref[...] = jnp.broadcast_to(r, r_ref.shape)

    @pl.when(s == pl.num_programs(1) - 1)
    def _():
        for h in range(SB_HEADS):
            o_ref[:, h * SB_DH:(h + 1) * SB_DH] = acc_ref[h * NQ:(h + 1) * NQ, :]


def _sb_sample(page_table, q, bias_col, k_new_pad, v_new_pad, cache_k, cache_v, layer, *, pages_per_step):
    bs, n_pages = page_table.shape
    ps = cache_k.shape[-1]
    nq = q.shape[1]
    pp = pages_per_step
    steps = n_pages // pp

    def page_spec(r):
        def idx(b, s, pt):
            return (layer, pt[b * n_pages + (n_pages - 1 - (s * pp + r))], 0, 0, 0)
        return pl.BlockSpec((None, None, SB_HEADS, SB_DH, ps), idx)

    kern = functools.partial(_sb_sample_kernel, PP=pp, PS=ps, NQ=nq)
    grid_spec = pltpu.PrefetchScalarGridSpec(
        num_scalar_prefetch=1,
        grid=(bs, steps),
        in_specs=[
            pl.BlockSpec((None, nq, 512), lambda b, s, pt: (b, 0, 0)),
            pl.BlockSpec((SB_HEADS * nq, 1), lambda b, s, pt: (0, 0)),
            pl.BlockSpec((None, 512, ps), lambda b, s, pt: (b, 0, 0)),
            pl.BlockSpec((None, 512, ps), lambda b, s, pt: (b, 0, 0)),
        ] + [page_spec(r) for r in range(pp)] + [page_spec(r) for r in range(pp)],
        out_specs=pl.BlockSpec((None, nq, 512), lambda b, s, pt: (b, 0, 0)),
        scratch_shapes=[pltpu.VMEM((SB_HEADS * nq, SB_DH), F32), pltpu.VMEM((SB_HEADS * nq, LANES), F32)],
    )
    return pl.pallas_call(
        kern,
        grid_spec=grid_spec,
        out_shape=jax.ShapeDtypeStruct((bs, nq, 512), F32),
        compiler_params=_params(("arbitrary", "arbitrary")),
        name="sb_sample",
    )(page_table.reshape(-1), q, bias_col, k_new_pad, v_new_pad, *([cache_k] * pp), *([cache_v] * pp))


def _merge_kernel(hm_ref, hg_ref, hs_ref, g0_ref, g1_ref, g2_ref, x_ref, gate_ref, wb_ref, wo_ref, o_ref):
    merged = jax.nn.sigmoid(g0_ref[...].astype(F32)) * _mm(hm_ref[...].astype(BF16), wb_ref[0])
    merged += jax.nn.sigmoid(g1_ref[...].astype(F32)) * _mm(hg_ref[...].astype(BF16), wb_ref[1])
    merged += jax.nn.sigmoid(g2_ref[...].astype(F32)) * _mm(hs_ref[...].astype(BF16), wb_ref[2])
    mix = _mm(merged.astype(BF16), wo_ref[...])
    o_ref[...] = x_ref[...] + gate_ref[...] * mix


def _merge(hm, hg, hs, main, x, gate, w_branch, w_out, layer, *, tm, rows_per_group):
    t, d = x.shape
    r = gate.shape[1]
    tiles_per_group = rows_per_group // tm
    tok = lambda blk: (lambda i: (i, blk))
    return pl.pallas_call(
        _merge_kernel,
        grid=(t // tm,),
        in_specs=[
            pl.BlockSpec((tm, 512), tok(0)),
            pl.BlockSpec((tm, 512), tok(0)),
            pl.BlockSpec((tm, 512), tok(0)),
            pl.BlockSpec((tm, d), tok(0)),
            pl.BlockSpec((tm, d), tok(1)),
            pl.BlockSpec((tm, d), tok(2)),
            pl.BlockSpec((tm, d), tok(0)),
            pl.BlockSpec((None, r, d), lambda i: (i // tiles_per_group, 0, 0)),
            pl.BlockSpec((None, N_BRANCH, BRANCH_W, d), lambda i: (layer, 0, 0, 0)),
            pl.BlockSpec((None, d, d), lambda i: (layer, 0, 0)),
        ],
        out_specs=pl.BlockSpec((tm, d), tok(0)),
        out_shape=jax.ShapeDtypeStruct((t, d), F32),
        compiler_params=_params(("arbitrary",)),
        name="branch_merge",
    )(hm, hg, hs, main, main, main, x, gate, w_branch, w_out)


FF_CHUNK = 256


def _ffn_kernel(x_ref, nw_ref, sc_ref, sh_ref, gate_ref, wu_ref, cw_ref, cb_ref, wd_ref, p1_ref, p2_ref, fw_ref,
                o_ref, u_ref, carry_ref, act_ref, *, TM, SEQ_IN_TILE, FINAL_NORM):
    x = x_ref[...]
    y = x * lax.rsqrt(jnp.mean(x * x, axis=-1, keepdims=True) + EPS)
    hb = (y * nw_ref[...] * (1.0 + sc_ref[...]) + sh_ref[...]).astype(BF16)
    row = lax.broadcasted_iota(jnp.int32, (TM, FF_CHUNK), 0)
    if SEQ_IN_TILE is None:
        @pl.when(pl.program_id(1) == 0)
        def _():
            carry_ref[...] = p1_ref[...]
        pos = row
    else:
        pos = row % SEQ_IN_TILE
    for c in range(D_FF // FF_CHUNK):
        cs = slice(c * FF_CHUNK, (c + 1) * FF_CHUNK)
        vs = slice(D_FF + c * FF_CHUNK, D_FF + (c + 1) * FF_CHUNK)
        u = _mm(hb, wu_ref[:, cs])
        val = _mm(hb, wu_ref[:, vs])
        if SEQ_IN_TILE is None:
            prev = carry_ref[:, cs]
            prev1 = prev[SUBLANES - 1:SUBLANES, :]
            prev2 = jnp.where(row == 0, prev[SUBLANES - 2:SUBLANES - 1, :], prev1)
            carry_ref[:, cs] = u[TM - SUBLANES:TM, :]
            u_ref[:, cs] = u[TM - SUBLANES:TM, :]
        else:
            prev1 = p1_ref[:, cs]
            prev2 = p2_ref[:, cs]
            u_ref[:, cs] = u
        u_m1 = jnp.where(pos >= 1, pltpu.roll(u, 1, axis=0), prev1)
        u_m2 = jnp.where(pos >= 2, pltpu.roll(u, 2, axis=0), prev2)
        cw = cw_ref[:, cs]
        conv = cb_ref[:, cs] + cw[0:1, :] * u_m2 + cw[1:2, :] * u_m1 + cw[2:3, :] * u
        act = 0.5 * conv * (1.0 + lax.erf(conv * (2.0 ** -0.5))) * val
        act_ref[:, cs] = act.astype(BF16)
    out = x + gate_ref[...] * _mm(act_ref[...], wd_ref[...])
    if FINAL_NORM:
        out = out * lax.rsqrt(jnp.mean(out * out, axis=-1, keepdims=True) + EPS) * fw_ref[...]
    o_ref[...] = out


def _ffn(x, nw, sc, sh, gate, w_up, conv_w, conv_b, w_down, p1, p2, final_w, layer, *, batch, seq, tm, seq_in_tile,
         final_norm):
    t, d = x.shape
    r = sc.shape[1]
    per_tile = seq_in_tile is not None
    if per_tile:
        grid = (1, t // tm)
        rowmap = lambda b, i: (i, 0)
        grp = lambda b, i: (i, 0, 0)
        pspec = pl.BlockSpec((tm, D_FF), rowmap)
        uspec = pl.BlockSpec((tm, D_FF), rowmap)
        ushape = jax.ShapeDtypeStruct((t, D_FF), F32)
    else:
        nt = seq // tm
        grid = (batch, nt)
        rowmap = lambda b, i: (b * nt + i, 0)
        grp = lambda b, i: (b, 0, 0)
        pspec = pl.BlockSpec((None, SUBLANES, D_FF), lambda b, i: (b, 0, 0))
        uspec = pl.BlockSpec((None, SUBLANES, D_FF), lambda b, i: (b, 0, 0))
        ushape = jax.ShapeDtypeStruct((batch, SUBLANES, D_FF), F32)
    const3 = lambda b, i: (layer, 0, 0)
    once = pl.Buffered(1)
    kern = functools.partial(_ffn_kernel, TM=tm, SEQ_IN_TILE=seq_in_tile, FINAL_NORM=final_norm)
    return pl.pallas_call(
        kern,
        grid=grid,
        in_specs=[
            pl.BlockSpec((tm, d), rowmap),
            pl.BlockSpec((None, 1, d), const3),
            pl.BlockSpec((None, r, d), grp),
            pl.BlockSpec((None, r, d), grp),
            pl.BlockSpec((None, r, d), grp),
            pl.BlockSpec((None, d, 2 * D_FF), const3, pipeline_mode=once),
            pl.BlockSpec((None, CONV_W, D_FF), const3),
            pl.BlockSpec((None, 1, D_FF), const3),
            pl.BlockSpec((None, D_FF, d), const3, pipeline_mode=once),
            pspec,
            pspec,
            pl.BlockSpec((1, d), lambda b, i: (0, 0)),
        ],
        out_specs=[pl.BlockSpec((tm, d), rowmap), uspec],
        out_shape=[jax.ShapeDtypeStruct((t, d), F32), ushape],
        scratch_shapes=[pltpu.VMEM((SUBLANES, D_FF), F32), pltpu.VMEM((tm, D_FF), BF16)],
        compiler_params=_params(("arbitrary", "arbitrary")),
        name="conv_ffn",
    )(x, nw, sc, sh, gate, w_up, conv_w, conv_b, w_down, p1, p2, final_w)


def _prep_weights(w_in, gla_w_gate, w_branch, w_out, w_up, w_down):
    widths = (256, 256, 512, 4, 4, 512, 256, 256, 512, GLA_RANK, 512, 512, 512, 512, N_BRANCH * D_MODEL)
    offs = np.concatenate([[0], np.cumsum(widths)])
    names = ("ml_q", "ml_k", "ml_v", "ml_i", "ml_f", "ml_o", "gla_q", "gla_k", "gla_v", "gla_lr", "gla_o",
             "sb_q", "sb_k", "sb_v", "gates")
    part = {nm: w_in[:, :, int(offs[i]):int(offs[i + 1])] for i, nm in enumerate(names)}
    order = ("gates", "ml_v", "ml_o", "gla_v", "gla_o", "sb_q", "ml_q", "ml_k", "gla_q", "gla_k")
    w_main = jnp.concatenate([part[nm] for nm in order], axis=-1).astype(BF16)
    w_kt = jnp.swapaxes(part["sb_k"], 1, 2).astype(BF16)
    w_vt = jnp.swapaxes(part["sb_v"], 1, 2).astype(BF16)
    small = jnp.concatenate([part["ml_i"], part["ml_f"], part["gla_lr"]], axis=-1)
    w_small = jnp.pad(small, ((0, 0), (0, 0), (0, SMALL_W - small.shape[-1]))).astype(BF16)
    gates_t = jnp.swapaxes(jnp.concatenate([part["ml_i"], part["ml_f"]], axis=-1), 1, 2)
    w_small_t = jnp.pad(gates_t, ((0, 0), (0, SMALL_T_ROWS - gates_t.shape[1]), (0, 0))).astype(BF16)
    wg_pad = jnp.pad(gla_w_gate, ((0, 0), (2 * ML_HEADS, SMALL_W - 2 * ML_HEADS - GLA_RANK), (0, 0))).astype(BF16)
    return ((w_main, w_small, w_small_t, w_kt, w_vt), wg_pad, w_branch.astype(BF16), w_out.astype(BF16),
            w_up.astype(BF16), w_down.astype(BF16))


def _run_group(x, mod, states, paged, weights, params, *, tm, scan_len, sb_block):
    (w_proj, wg_pad, w_branch, w_out, w_up, w_down) = weights
    (norm1_w, norm2_w, ml_b_i, ml_b_f, ml_norm_w, gla_b_gate, gla_norm_w, sb_bias, conv_w, conv_b, final_w) = params
    ml_c0, ml_n0, ml_m0, gla_s0, conv0 = states
    batch, seq, d = x.shape
    depth = w_proj[0].shape[0]
    t = batch * seq
    per_seq = seq < tm
    xt = x.reshape(t, d)
    scan_dtype = F32 if per_seq else BF16
    kv_groups = 1 if per_seq else batch
    kbuf = jnp.zeros((depth, kv_groups, SB_HEADS * SB_DH, t // kv_groups), F32)
    vbuf = jnp.zeros((depth, kv_groups, SB_HEADS * SB_DH, t // kv_groups), F32)
    mcs, mns, mms, gss, bufs = [], [], [], [], []
    for l in range(depth):
        def modrow(i):
            m = mod[l, :, i, :]
            if per_seq:
                return jnp.repeat(m, seq, axis=0).reshape(t // tm, tm, d)
            return m.reshape(batch, 1, d)
        sh1, sc1, g1, sh2, sc2, g2 = [modrow(i) for i in range(6)]
        rows_per_group = tm if per_seq else seq
        tm_proj = tm if per_seq else math.gcd(seq, 2 * tm)
        main, small, small_t, kbuf, vbuf = _norm_proj(xt, norm1_w, sc1, sh1, w_proj, kbuf, vbuf, l,
                                                      tm=tm_proj, rows_per_group=rows_per_group,
                                                      main_dtype=scan_dtype)
        if per_seq:
            small_t3 = small_t.reshape(SMALL_T_ROWS, batch, seq).transpose(1, 0, 2)
        else:
            small_t3 = small_t.reshape(1, SMALL_T_ROWS, t)
        b_row = jnp.zeros((1, SMALL_W), F32).at[0, 0:ML_HEADS].set(ml_b_i[l]).at[0, ML_HEADS:2 * ML_HEADS].set(ml_b_f[l])
        b_col = jnp.zeros((SMALL_T_ROWS, 1), F32).at[0:ML_HEADS, 0].set(ml_b_i[l]).at[ML_HEADS:2 * ML_HEADS, 0].set(ml_b_f[l])
        ml = _mlstm(main, small, small_t3, b_row, b_col, ml_norm_w[l].reshape(1, 512),
                    ml_c0[l], ml_n0[l], ml_m0[l].reshape(batch, ML_HEADS, 1),
                    batch=batch, seq=seq, L=scan_len, out_dtype=scan_dtype, per_seq=per_seq)
        gla = _gla(main, small, wg_pad, gla_b_gate, gla_norm_w, jnp.swapaxes(gla_s0[l], -1, -2), l,
                   batch=batch, seq=seq, L=scan_len, out_dtype=scan_dtype)
        hm, mc, mn, mm, hg, gst = _scans(ml, gla, batch=batch, seq=seq, L=scan_len)
        if paged is None:
            bias_row = jnp.repeat(sb_bias[l], SB_DH).reshape(1, 512)
            hs = _sb_prompt(main, kbuf, vbuf, bias_row, l, batch=batch, seq=seq, tq=sb_block[0], tk=sb_block[1])
        else:
            cache_kt, cache_vt, page_table = paged
            ps = cache_kt.shape[-1]
            q = main[:, BLK_SB_Q * 512:(BLK_SB_Q + 1) * 512].reshape(batch, seq, 512)
            bias_col = jnp.repeat(sb_bias[l], seq).reshape(SB_HEADS * seq, 1)
            new_t = lambda buf: jnp.pad(buf[l, 0].reshape(-1, batch, seq).transpose(1, 0, 2),
                                        ((0, 0), (0, 0), (0, ps - seq)))
            hs = _sb_sample(page_table, q, bias_col, new_t(kbuf), new_t(vbuf), cache_kt, cache_vt, l,
                            pages_per_step=math.gcd(page_table.shape[1], 16)).reshape(t, 512)
        xt = _merge(hm, hg, hs, main, xt, g1, w_branch, w_out, l, tm=tm, rows_per_group=rows_per_group)
        if per_seq:
            c0 = conv0[l]
            zero = jnp.zeros((batch, seq - 1, D_FF), F32)
            p1 = jnp.concatenate([c0[:, 1:2], zero], axis=1).reshape(t, D_FF)
            p2 = jnp.concatenate([c0[:, 0:1], c0[:, 1:2], zero[:, 1:]], axis=1).reshape(t, D_FF)
            xt, u = _ffn(xt, norm2_w, sc2, sh2, g2, w_up, conv_w, conv_b, w_down, p1, p2, final_w, l,
                         batch=batch, seq=seq, tm=tm, seq_in_tile=seq, final_norm=l == depth - 1)
            full = jnp.concatenate([conv0[l], u.reshape(batch, seq, D_FF)], axis=1)
            buf = full[:, seq:]
        else:
            p = jnp.pad(conv0[l], ((0, 0), (SUBLANES - (CONV_W - 1), 0), (0, 0)))
            xt, tail = _ffn(xt, norm2_w, sc2, sh2, g2, w_up, conv_w, conv_b, w_down, p, p, final_w, l,
                            batch=batch, seq=seq, tm=tm, seq_in_tile=None, final_norm=l == depth - 1)
            buf = tail[:, SUBLANES - (CONV_W - 1):]
        mcs.append(mc)
        mns.append(mn)
        mms.append(mm.reshape(batch, ML_HEADS))
        gss.append(jnp.swapaxes(gst, -1, -2))
        bufs.append(buf)
    st = jnp.stack

    def kv_out(buf):
        b6 = buf.reshape(depth, kv_groups, SB_HEADS, SB_DH, batch // kv_groups, seq)
        return b6.transpose(0, 1, 4, 5, 2, 3).reshape(depth, batch, seq, SB_HEADS, SB_DH)

    return xt, kv_out(kbuf), kv_out(vbuf), st(mcs), st(mns), st(mms), st(gss), st(bufs)


def kernel(x_prompt, x_sample, cache_k, cache_v, state_mlstm_c, state_mlstm_n, state_mlstm_m, state_gla, state_conv, page_table, c_prompt, c_sample, w_ada, b_ada, norm1_w, norm2_w, w_in, ml_b_i, ml_b_f, ml_norm_w, gla_w_gate, gla_b_gate, gla_norm_w, sb_bias, w_branch, w_out, w_up, conv_w, conv_b, w_down, final_norm_w):
    depth = w_in.shape[0]
    bp, sp, d = x_prompt.shape
    bs, ss, _ = x_sample.shape
    weights = _prep_weights(w_in, gla_w_gate, w_branch, w_out, w_up, w_down)
    params = (norm1_w.reshape(depth, 1, d), norm2_w.reshape(depth, 1, d), ml_b_i, ml_b_f, ml_norm_w,
              gla_b_gate.reshape(depth, 1, -1), gla_norm_w.reshape(depth, 1, -1), sb_bias, conv_w,
              conv_b.reshape(depth, 1, -1), final_norm_w.reshape(1, d))
    mod = _modulation(jnp.concatenate([c_prompt, c_sample], axis=0), w_ada, b_ada)
    mod_p = mod[:, :bp].reshape(depth, bp, 6, d)
    mod_s = mod[:, bp:].reshape(depth, bs, 6, d)

    zeros = lambda *shape: jnp.zeros((depth, bp) + shape, F32)
    states_p = (zeros(ML_HEADS, ML_DK, ML_DV), zeros(ML_HEADS, ML_DK),
                jnp.full((depth, bp, ML_HEADS), NEG_BIG, F32), zeros(GLA_HEADS, GLA_DK, GLA_DV),
                zeros(CONV_W - 1, D_FF))
    tm_p = math.gcd(sp, 512)
    scan_p = math.gcd(sp, 512)
    out_p = _run_group(x_prompt, mod_p, states_p, None, weights, params, tm=tm_p, scan_len=scan_p,
                       sb_block=(math.gcd(sp, 512), math.gcd(sp, 256)))

    states_s = (state_mlstm_c, state_mlstm_n, state_mlstm_m, state_gla, state_conv)
    cache_kt = jnp.transpose(cache_k, (0, 1, 3, 4, 2))
    cache_vt = jnp.transpose(cache_v, (0, 1, 3, 4, 2))
    out_s = _run_group(x_sample, mod_s, states_s, (cache_kt, cache_vt, page_table), weights, params,
                       tm=bs * ss, scan_len=ss, sb_block=None)

    y_p = out_p[0].reshape(bp, sp, d)
    y_s = out_s[0].reshape(bs, ss, d)
    (_, k_p, v_p, mc_p, mn_p, mm_p, g_p, cb_p) = out_p
    (_, k_s, v_s, mc_s, mn_s, mm_s, g_s, cb_s) = out_s
    return (y_p, y_s, k_p, v_p, k_s, v_s, mc_p, mn_p, mm_p, mc_s, mn_s, mm_s, g_p, g_s, cb_p, cb_s)
```

```python
import functools
import math

import numpy as np
import jax
import jax.numpy as jnp
from jax import lax
from jax.experimental import pallas as pl
from jax.experimental.pallas import tpu as pltpu

F32 = jnp.float32
BF16 = jnp.bfloat16

D_MODEL = 1024
ML_HEADS, ML_DK, ML_DV = 4, 64, 128
GLA_HEADS, GLA_DK, GLA_DV = 4, 64, 128
GLA_RANK = 16
GLA_TAU = 16.0
SB_HEADS, SB_DH = 8, 64
BRANCH_W = 512
N_BRANCH = 3
D_FF = 2816
CONV_W = 3
EPS = 1e-6
NEG_BIG = -1e30
SCAN_CHUNK = 64

LANES = 128
SUBLANES = 8
VMEM_LIMIT = 56 * 1024 * 1024

N_MAIN = 6656
BLK_ML_V, BLK_ML_O, BLK_GLA_V, BLK_GLA_O, BLK_SB_Q = 6, 7, 8, 9, 10
BLK_ML_Q, BLK_ML_K, BLK_GLA_Q, BLK_GLA_K = 22, 23, 24, 25
MAIN_TN = 1664
LOG2E = 1.4426950408889634
SMALL_W = 128
SMALL_T_ROWS = 32


def _nt(a, b, precision=None):
    return lax.dot_general(a, b, (((1,), (1,)), ((), ())), preferred_element_type=F32, precision=precision)


def _tn(a, b):
    return lax.dot_general(a, b, (((0,), (0,)), ((), ())), preferred_element_type=F32)


def _mm(a, b, precision=None):
    return jnp.dot(a, b, preferred_element_type=F32, precision=precision)


def _cumsum_mm(tri, x, *, tri_first):
    t = tri.astype(BF16)
    hi = x.astype(BF16)
    r1 = x - hi.astype(F32)
    mid = r1.astype(BF16)
    lo = (r1 - mid.astype(F32)).astype(BF16)
    parts = (hi, mid, lo)
    terms = [_mm(t, p) if tri_first else _mm(p, t) for p in parts]
    return terms[0] + (terms[1] + terms[2])


def _log_sigmoid(x):
    return jnp.minimum(x, 0.0) - jnp.log1p(jnp.exp(-jnp.abs(x)))


def _softplus2(x):
    return jnp.maximum(x, 0.0) + jnp.log2(1.0 + jnp.exp2(-jnp.abs(x)))


def _params(sem):
    return pltpu.CompilerParams(dimension_semantics=sem, vmem_limit_bytes=VMEM_LIMIT)


def _mod_kernel(c_ref, w_ref, b_ref, o_ref):
    c = c_ref[...]
    s = c * jax.nn.sigmoid(c)
    o_ref[...] = _mm(s.astype(BF16), w_ref[...].astype(BF16)) + b_ref[...]


def _modulation(c_all, w_ada, b_ada):
    depth, d, n = w_ada.shape
    nb = c_all.shape[0]
    tn = 1536
    return pl.pallas_call(
        _mod_kernel,
        grid=(depth, n // tn),
        in_specs=[
            pl.BlockSpec((nb, d), lambda l, j: (0, 0)),
            pl.BlockSpec((None, d, tn), lambda l, j: (l, 0, j)),
            pl.BlockSpec((None, 1, tn), lambda l, j: (l, 0, j)),
        ],
        out_specs=pl.BlockSpec((None, nb, tn), lambda l, j: (l, 0, j)),
        out_shape=jax.ShapeDtypeStruct((depth, nb, n), F32),
        compiler_params=_params(("arbitrary", "arbitrary")),
        name="adaln_mod",
    )(c_all, w_ada, b_ada.reshape(depth, 1, n))


def _mod_spec(mods, k, layer, group_of):
    arr, depth = mods
    groups = arr.shape[0] // (6 * depth)
    base = (k * depth + layer) * groups
    return pl.BlockSpec((None,) + arr.shape[1:], lambda *g: (base + group_of(*g), 0, 0))


def _normmm_kernel(x_ref, nw_ref, sc_ref, sh_ref, w_ref, ws_ref, wk_ref, wv_ref, kbuf_ref, vbuf_ref,
                   o_ref, os_ref, ot_ref, kt_ref, vt_ref, h_scr):
    del kbuf_ref, vbuf_ref

    @pl.when(pl.program_id(1) == 0)
    def _():
        x = x_ref[...]
        y = x * lax.rsqrt(jnp.mean(x * x, axis=-1, keepdims=True) + EPS)
        h = y * nw_ref[...] * (1.0 + sc_ref[...]) + sh_ref[...]
        hb = h.astype(BF16)
        h_scr[...] = hb
        os_ref[...] = _nt(hb, ws_ref[...])
        ot_ref[...] = _nt(ws_ref[0:SMALL_T_ROWS, :], hb)
        kt_ref[...] = _nt(wk_ref[...], hb)
        vt_ref[...] = _nt(wv_ref[...], hb)

    o_ref[...] = _nt(h_scr[...], w_ref[...]).astype(o_ref.dtype)


def _norm_proj(x, nw, mods, weights, kbuf, vbuf, layer, *, tm, rows_per_group, main_dtype):
    w_main, w_small, w_kt, w_vt = weights
    t, d = x.shape
    n = w_main.shape[1]
    tn = MAIN_TN
    tiles_per_group = rows_per_group // tm
    grp = lambda i, j: i // tiles_per_group
    kv_rows = w_kt.shape[1]
    kv_map = lambda i, j: (layer, i // tiles_per_group, 0, i % tiles_per_group)
    wmap = lambda i, j: (layer, 0, 0)
    return pl.pallas_call(
        _normmm_kernel,
        grid=(t // tm, n // tn),
        in_specs=[
            pl.BlockSpec((tm, d), lambda i, j: (i, 0)),
            pl.BlockSpec((None, 1, d), wmap),
            _mod_spec(mods, 1, layer, grp),
            _mod_spec(mods, 0, layer, grp),
            pl.BlockSpec((None, tn, d), lambda i, j: (layer, j, 0)),
            pl.BlockSpec((None, SMALL_W, d), wmap),
            pl.BlockSpec((None, kv_rows, d), wmap),
            pl.BlockSpec((None, kv_rows, d), wmap),
            pl.BlockSpec(memory_space=pl.ANY),
            pl.BlockSpec(memory_space=pl.ANY),
        ],
        out_specs=[
            pl.BlockSpec((tm, tn), lambda i, j: (i, j)),
            pl.BlockSpec((tm, SMALL_W), lambda i, j: (i, 0)),
            pl.BlockSpec((SMALL_T_ROWS, tm), lambda i, j: (0, i)),
            pl.BlockSpec((None, None, kv_rows, tm), kv_map),
            pl.BlockSpec((None, None, kv_rows, tm), kv_map),
        ],
        out_shape=[
            jax.ShapeDtypeStruct((t, n), main_dtype),
            jax.ShapeDtypeStruct((t, SMALL_W), F32),
            jax.ShapeDtypeStruct((SMALL_T_ROWS, t), F32),
            jax.ShapeDtypeStruct(kbuf.shape, F32),
            jax.ShapeDtypeStruct(vbuf.shape, F32),
        ],
        input_output_aliases={8: 3, 9: 4},
        scratch_shapes=[pltpu.VMEM((tm, d), BF16)],
        compiler_params=_params(("arbitrary", "arbitrary")),
        name="norm_in_proj",
    )(x, nw, mods[0], mods[0], w_main, w_small, w_kt, w_vt, kbuf, vbuf)


def _mlstm_kernel(q_ref, k_ref, v_ref, o_ref, sc_ref, st_ref, brow_ref, bcol_ref, nw_ref, c0_ref, n0_ref, m0_ref,
                  hn_ref, c_ref, n_ref, m_ref, *, L):
    @pl.when(pl.program_id(1) == 0)
    def _():
        c_ref[...] = c0_ref[...]
        n_ref[...] = n0_ref[...]
        m_ref[...] = m0_ref[...]

    row = lax.broadcasted_iota(jnp.int32, (L, L), 0)
    col = lax.broadcasted_iota(jnp.int32, (L, L), 1)
    causal = col <= row
    small = sc_ref[...] + brow_ref[...]
    small_t = st_ref[...] + bcol_ref[...]
    bc_all = _cumsum_mm(causal, _log_sigmoid(small), tri_first=True)
    br_all = _cumsum_mm(row <= col, _log_sigmoid(small_t), tri_first=False)
    q = q_ref[...].astype(F32) * (ML_DK ** -0.5)
    k = k_ref[...].astype(F32)
    v = v_ref[...].astype(F32)
    og = o_ref[...].astype(F32)
    nw = nw_ref[...]
    heads = range(ML_HEADS)
    ks = [slice(h * ML_DK, (h + 1) * ML_DK) for h in heads]
    vs = [slice(h * ML_DV, (h + 1) * ML_DV) for h in heads]
    qb = [q[:, ks[h]].astype(BF16) for h in heads]
    vb = [v[:, vs[h]].astype(BF16) for h in heads]
    qk = [_nt(qb[h], k[:, ks[h]].astype(BF16)) for h in heads]
    b_col = [bc_all[:, ML_HEADS + h:ML_HEADS + h + 1] for h in heads]
    b_row = [br_all[ML_HEADS + h:ML_HEADS + h + 1, :] for h in heads]
    m_prev = [m_ref[h:h + 1, :] for h in heads]
    c_st = [c_ref[h] for h in heads]
    n_st = [n_ref[h:h + 1, :] for h in heads]
    dmat = [jnp.where(causal, b_col[h] - b_row[h] + small_t[h:h + 1, :], -jnp.inf) for h in heads]
    inter = [b_col[h] + m_prev[h] for h in heads]
    m_t = [jnp.maximum(inter[h], jnp.max(dmat[h], axis=-1, keepdims=True)) for h in heads]
    s = [qk[h] * jnp.exp(dmat[h] - m_t[h]) for h in heads]
    e_inter = [jnp.exp(inter[h] - m_t[h]) for h in heads]
    qc = [_mm(qb[h], c_st[h].astype(BF16)) for h in heads]
    num = [_mm(s[h].astype(BF16), vb[h]) + e_inter[h] * qc[h] for h in heads]
    den = [jnp.sum(s[h], axis=-1, keepdims=True)
           + e_inter[h] * jnp.sum(q[:, ks[h]] * n_st[h], axis=-1, keepdims=True) for h in heads]
    hh = [num[h] / jnp.maximum(jnp.abs(den[h]), jnp.exp(-m_t[h])) for h in heads]
    for h in heads:
        m_new = m_t[h][L - 1:L, :]
        b_last = b_col[h][L - 1:L, :]
        wk = jnp.exp(b_last - b_col[h] + small[:, h:h + 1] - m_new)
        decay = jnp.exp(b_last + m_prev[h] - m_new)
        kw = k[:, ks[h]] * wk
        c_ref[h] = decay * c_st[h] + _tn(kw.astype(BF16), vb[h])
        n_ref[h:h + 1, :] = decay * n_st[h] + jnp.sum(kw, axis=0, keepdims=True)
        m_ref[h:h + 1, :] = m_new
    for h in heads:
        y = hh[h] * lax.rsqrt(jnp.mean(hh[h] * hh[h], axis=-1, keepdims=True) + EPS) * nw[:, vs[h]]
        hn_ref[:, vs[h]] = (y * jax.nn.sigmoid(og[:, vs[h]])).astype(hn_ref.dtype)


def _small_t_spec(L, nc, per_seq):
    if per_seq:
        return pl.BlockSpec((None, SMALL_T_ROWS, L), lambda b, c: (b, 0, 0))
    return pl.BlockSpec((None, SMALL_T_ROWS, L), lambda b, c: (0, 0, b * nc + c))


def _mlstm(main, small, small_t3, b_row, b_col, nw, c0, n0, m0, layer, *, batch, seq, L, out_dtype, per_seq):
    nc = seq // L
    t = batch * seq
    tok = lambda blk: (lambda b, c: (b * nc + c, blk))
    st = lambda b, c: (b, 0, 0)
    return dict(
        in_specs=[
            pl.BlockSpec((L, 256), tok(BLK_ML_Q)),
            pl.BlockSpec((L, 256), tok(BLK_ML_K)),
            pl.BlockSpec((L, 512), tok(BLK_ML_V)),
            pl.BlockSpec((L, 512), tok(BLK_ML_O)),
            pl.BlockSpec((L, SMALL_W), tok(0)),
            _small_t_spec(L, nc, per_seq),
            pl.BlockSpec((None, 1, SMALL_W), lambda b, c: (layer, 0, 0)),
            pl.BlockSpec((None, SMALL_T_ROWS, 1), lambda b, c: (layer, 0, 0)),
            pl.BlockSpec((None, 1, 512), lambda b, c: (layer, 0, 0)),
            pl.BlockSpec((None, None, ML_HEADS, ML_DK, ML_DV), lambda b, c: (layer, b, 0, 0, 0)),
            pl.BlockSpec((None, None, ML_HEADS, ML_DK), lambda b, c: (layer, b, 0, 0)),
            pl.BlockSpec((None, None, ML_HEADS, 1), lambda b, c: (layer, b, 0, 0)),
        ],
        out_specs=[
            pl.BlockSpec((L, 512), tok(0)),
            pl.BlockSpec((None, ML_HEADS, ML_DK, ML_DV), lambda b, c: (b, 0, 0, 0)),
            pl.BlockSpec((None, ML_HEADS, ML_DK), st),
            pl.BlockSpec((None, ML_HEADS, 1), st),
        ],
        out_shape=[
            jax.ShapeDtypeStruct((t, 512), out_dtype),
            jax.ShapeDtypeStruct((batch, ML_HEADS, ML_DK, ML_DV), F32),
            jax.ShapeDtypeStruct((batch, ML_HEADS, ML_DK), F32),
            jax.ShapeDtypeStruct((batch, ML_HEADS, 1), F32),
        ],
        operands=(main, main, main, main, small, small_t3, b_row, b_col, nw, c0, n0, m0),
    )


def _gla_kernel(q_ref, k_ref, v_ref, o_ref, sc_ref, wg_ref, bg_ref, nw_ref, s0_ref, hn_ref, s_ref, *, L, LS):
    @pl.when(pl.program_id(1) == 0)
    def _():
        s_ref[...] = s0_ref[...]

    row = lax.broadcasted_iota(jnp.int32, (LS, LS), 0)
    col = lax.broadcasted_iota(jnp.int32, (LS, LS), 1)
    causal = col <= row
    la_all = _log_sigmoid(_mm(sc_ref[...].astype(BF16), wg_ref[...]) + bg_ref[...]) * (1.0 / GLA_TAU)
    nw = nw_ref[...]
    heads = range(GLA_HEADS)
    ks = [slice(h * GLA_DK, (h + 1) * GLA_DK) for h in heads]
    vs = [slice(h * GLA_DV, (h + 1) * GLA_DV) for h in heads]
    pre = []
    for c in range(L // LS):
        rs = slice(c * LS, (c + 1) * LS)
        bc = _cumsum_mm(causal, la_all[rs, :], tri_first=True)
        ref_row = bc[LS // 2:LS // 2 + 1, :]
        last = bc[LS - 1:LS, :]
        q = q_ref[rs, :].astype(F32) * (GLA_DK ** -0.5)
        k = k_ref[rs, :].astype(F32)
        qe = (q * jnp.exp(bc - ref_row)).astype(BF16)
        ke = (k * jnp.exp(ref_row - bc)).astype(BF16)
        qs = (q * jnp.exp(bc)).astype(BF16)
        kl = (k * jnp.exp(last - bc)).astype(BF16)
        el = jnp.exp(last)
        v = v_ref[rs, :].astype(F32)
        vb = [v[:, vs[h]].astype(BF16) for h in heads]
        av = [_mm(jnp.where(causal, _nt(qe[:, ks[h]], ke[:, ks[h]]), 0.0).astype(BF16), vb[h]) for h in heads]
        pre.append((rs, qs, kl, el, vb, av))
    state = [s_ref[h] for h in heads]
    outs = []
    for rs, qs, kl, el, vb, av in pre:
        o = [av[h] + _nt(qs[:, ks[h]], state[h].astype(BF16)) for h in heads]
        state = [el[:, ks[h]] * state[h] + _tn(vb[h], kl[:, ks[h]]) for h in heads]
        outs.append((rs, o))
    for h in heads:
        s_ref[h] = state[h]
    for rs, o in outs:
        og = o_ref[rs, :].astype(F32)
        for h in heads:
            y = o[h] * lax.rsqrt(jnp.mean(o[h] * o[h], axis=-1, keepdims=True) + EPS) * nw[:, vs[h]]
            g = og[:, vs[h]]
            hn_ref[rs, vs[h]] = (y * (g * jax.nn.sigmoid(g))).astype(hn_ref.dtype)


def _gla(main, small, wg_pad, bg, nw, s0t, layer, *, batch, seq, L, out_dtype):
    nc = seq // L
    t = batch * seq
    tok = lambda blk: (lambda b, c: (b * nc + c, blk))
    return dict(
        in_specs=[
            pl.BlockSpec((L, 256), tok(BLK_GLA_Q)),
            pl.BlockSpec((L, 256), tok(BLK_GLA_K)),
            pl.BlockSpec((L, 512), tok(BLK_GLA_V)),
            pl.BlockSpec((L, 512), tok(BLK_GLA_O)),
            pl.BlockSpec((L, SMALL_W), tok(0)),
            pl.BlockSpec((None, SMALL_W, 256), lambda b, c: (layer, 0, 0)),
            pl.BlockSpec((None, 1, 256), lambda b, c: (layer, 0, 0)),
            pl.BlockSpec((None, 1, 512), lambda b, c: (layer, 0, 0)),
            pl.BlockSpec((None, None, GLA_HEADS, GLA_DV, GLA_DK), lambda b, c: (layer, b, 0, 0, 0)),
        ],
        out_specs=[
            pl.BlockSpec((L, 512), tok(0)),
            pl.BlockSpec((None, GLA_HEADS, GLA_DV, GLA_DK), lambda b, c: (b, 0, 0, 0)),
        ],
        out_shape=[
            jax.ShapeDtypeStruct((t, 512), out_dtype),
            jax.ShapeDtypeStruct((batch, GLA_HEADS, GLA_DV, GLA_DK), F32),
        ],
        operands=(main, main, main, main, small, wg_pad, bg, nw, s0t),
    )


def _scan_kernel(*refs, N_ML_IN, N_GLA_IN, N_ML_OUT, L, LS):
    ml_in = refs[:N_ML_IN]
    gla_in = refs[N_ML_IN:N_ML_IN + N_GLA_IN]
    outs = refs[N_ML_IN + N_GLA_IN:]
    _mlstm_kernel(*ml_in, *outs[:N_ML_OUT], L=L)
    _gla_kernel(*gla_in, *outs[N_ML_OUT:], L=L, LS=LS)


def _scans(ml, gla, *, batch, seq, L):
    kern = functools.partial(_scan_kernel, N_ML_IN=len(ml["in_specs"]), N_GLA_IN=len(gla["in_specs"]),
                             N_ML_OUT=len(ml["out_specs"]), L=L, LS=math.gcd(L, SCAN_CHUNK))
    return pl.pallas_call(
        kern,
        grid=(batch, seq // L),
        in_specs=ml["in_specs"] + gla["in_specs"],
        out_specs=ml["out_specs"] + gla["out_specs"],
        out_shape=ml["out_shape"] + gla["out_shape"],
        compiler_params=_params(("arbitrary", "arbitrary")),
        name="mlstm_gla_scan",
    )(*ml["operands"], *gla["operands"])


SB_HEAD_GROUP = 2


def _sb_prompt_kernel(qi_ref, kj_ref, q_ref, kt_ref, vt_ref, bias_ref, o_ref, acc_ref, r_ref, *, TQ, TK):
    p = pl.program_id(1)
    i = qi_ref[p]
    j = kj_ref[p]
    ratio = TQ // TK

    @pl.when(j == ratio * i + ratio - 1)
    def _():
        acc_ref[...] = jnp.zeros_like(acc_ref)
        r_ref[...] = jnp.zeros_like(r_ref)

    row = lax.broadcasted_iota(jnp.int32, (TK, TK), 0)
    col = lax.broadcasted_iota(jnp.int32, (TK, TK), 1)
    suffix = (row >= col).astype(BF16)
    q = (q_ref[...].astype(F32) * (SB_DH ** -0.5 * LOG2E)).astype(BF16)
    bias = bias_ref[...] * LOG2E
    head = [slice(h * SB_DH, (h + 1) * SB_DH) for h in range(SB_HEADS)]

    def run(rows, diagonal):
        valid = col < row
        for g in range(0, SB_HEADS, SB_HEAD_GROUP):
            hh = range(g, g + SB_HEAD_GROUP)
            zs = [_mm(q[rows, head[h]], kt_ref[head[h], :].astype(BF16)) + bias[:, h * SB_DH:h * SB_DH + 1]
                  for h in hh]
            sps = [_softplus2(z) for z in zs]
            if diagonal:
                sps = [jnp.where(valid, sp, 0.0) for sp in sps]
            rests = [_mm(sp.astype(BF16), suffix) for sp in sps]
            rs = [r_ref[rows, h:h + 1] for h in hh]
            probs = [jnp.exp2(z - rest - r) for z, rest, r in zip(zs, rests, rs)]
            if diagonal:
                probs = [jnp.where(valid, a, 0.0) for a in probs]
            for h, a, r, rest in zip(hh, probs, rs, rests):
                acc_ref[h, rows, :] += _nt(a.astype(BF16), vt_ref[head[h], :].astype(BF16))
                r_ref[rows, h:h + 1] = r + rest[:, 0:1]

    for part in range(ratio):
        rows = slice(part * TK, (part + 1) * TK)
        pl.when(j == ratio * i + part)(functools.partial(run, rows, True))
        if part > 0:
            pl.when(jnp.logical_and(j < ratio * i + part, j >= ratio * i))(functools.partial(run, rows, False))
    pl.when(j < ratio * i)(functools.partial(run, slice(0, TQ), False))

    @pl.when(j == 0)
    def _():
        for h in range(SB_HEADS):
            o_ref[:, head[h]] = acc_ref[h].astype(o_ref.dtype)


def _sb_prompt(main, kbuf, vbuf, bias_row, layer, *, batch, seq, tq, tk):
    nq = seq // tq
    ratio = tq // tk
    qi = np.concatenate([np.full(ratio * (i + 1), i) for i in range(nq)]).astype(np.int32)
    kj = np.concatenate([np.arange(ratio * (i + 1) - 1, -1, -1) for i in range(nq)]).astype(np.int32)
    t = batch * seq
    kern = functools.partial(_sb_prompt_kernel, TQ=tq, TK=tk)
    kv_spec = pl.BlockSpec((None, None, SB_HEADS * SB_DH, tk), lambda b, p, qi, kj: (layer, b, 0, kj[p]))
    grid_spec = pltpu.PrefetchScalarGridSpec(
        num_scalar_prefetch=2,
        grid=(batch, len(qi)),
        in_specs=[
            pl.BlockSpec((tq, 512), lambda b, p, qi, kj: (b * nq + qi[p], BLK_SB_Q)),
            kv_spec,
            kv_spec,
            pl.BlockSpec((None, 1, 512), lambda b, p, qi, kj: (layer, 0, 0)),
        ],
        out_specs=pl.BlockSpec((tq, 512), lambda b, p, qi, kj: (b * nq + qi[p], 0)),
        scratch_shapes=[pltpu.VMEM((SB_HEADS, tq, SB_DH), F32), pltpu.VMEM((tq, LANES), F32)],
    )
    return pl.pallas_call(
        kern,
        grid_spec=grid_spec,
        out_shape=jax.ShapeDtypeStruct((t, 512), BF16),
        compiler_params=_params(("arbitrary", "arbitrary")),
        name="sb_prompt",
    )(jnp.asarray(qi), jnp.asarray(kj), main, kbuf, vbuf, bias_row)


def _sb_sample_kernel(pt_ref, q_ref, bias_ref, kn_ref, vn_ref, *rest, PP, PS, NQ):
    kp = rest[:PP]
    vp = rest[PP:2 * PP]
    o_ref, acc_ref, r_ref = rest[2 * PP:]
    s = pl.program_id(1)
    hq = SB_HEADS * NQ
    q = (q_ref[...] * (SB_DH ** -0.5 * LOG2E)).astype(BF16)
    bias = bias_ref[...] * LOG2E
    group = 2 if PP % 2 == 0 else 1
    row = lax.broadcasted_iota(jnp.int32, (group * PS, group * PS), 0)
    col = lax.broadcasted_iota(jnp.int32, (group * PS, group * PS), 1)
    suffix = (row >= col).astype(BF16)

    def logits(get_kt):
        return jnp.concatenate(
            [_mm(q[:, h * SB_DH:(h + 1) * SB_DH], get_kt(h).astype(BF16)) for h in range(SB_HEADS)], axis=0) + bias

    def suffix_sum(sp):
        nk = sp.shape[1]
        return _mm(sp.astype(BF16), suffix[:nk, :nk])

    def attend(z, rest_, r, get_vt, valid):
        a = jnp.exp2(z - rest_ - r)
        if valid is not None:
            a = jnp.where(valid, a, 0.0)
        for h in range(SB_HEADS):
            rows = slice(h * NQ, (h + 1) * NQ)
            acc_ref[rows, :] += _nt(a[rows, :].astype(BF16), get_vt(h).astype(BF16))
        return r + rest_[:, 0:1]

    @pl.when(s == 0)
    def _():
        acc_ref[...] = jnp.zeros_like(acc_ref)
        key = lax.broadcasted_iota(jnp.int32, (hq, PS), 1)
        qpos = lax.broadcasted_iota(jnp.int32, (hq, PS), 0) % NQ
        valid = key < qpos
        z = logits(lambda h: kn_ref[h * SB_DH:(h + 1) * SB_DH, :])
        rest_ = suffix_sum(jnp.where(valid, _softplus2(z), 0.0))
        r = attend(z, rest_, 0.0, lambda h: vn_ref[h * SB_DH:(h + 1) * SB_DH, :], valid)
        r_ref[...] = jnp.broadcast_to(r, r_ref.shape)

    def pages(refs, p0):
        return lambda h: jnp.concatenate([refs[p][h] for p in range(p0 + group - 1, p0 - 1, -1)], axis=1)

    starts = range(0, PP, group)
    zs = [logits(pages(kp, p0)) for p0 in starts]
    rests = [suffix_sum(_softplus2(z)) for z in zs]
    r = r_ref[:, 0:1]
    for z, rest_, p0 in zip(zs, rests, starts):
        r = attend(z, rest_, r, pages(vp, p0), None)
    r_ref[...] = jnp.broadcast_to(r, r_ref.shape)

    @pl.when(s == pl.num_programs(1) - 1)
    def _():
        for h in range(SB_HEADS):
            o_ref[:, h * SB_DH:(h + 1) * SB_DH] = acc_ref[h * NQ:(h + 1) * NQ, :]


def _sb_sample(page_table, q, bias_col, k_new_pad, v_new_pad, cache_k, cache_v, layer, *, pages_per_step):
    bs, n_pages = page_table.shape
    ps = cache_k.shape[-1]
    nq = q.shape[1]
    pp = pages_per_step
    steps = n_pages // pp

    def page_spec(r):
        def idx(b, s, pt):
            return (layer, pt[b * n_pages + (n_pages - 1 - (s * pp + r))], 0, 0, 0)
        return pl.BlockSpec((None, None, SB_HEADS, SB_DH, ps), idx)

    kern = functools.partial(_sb_sample_kernel, PP=pp, PS=ps, NQ=nq)
    grid_spec = pltpu.PrefetchScalarGridSpec(
        num_scalar_prefetch=1,
        grid=(bs, steps),
        in_specs=[
            pl.BlockSpec((None, nq, 512), lambda b, s, pt: (b, 0, 0)),
            pl.BlockSpec((None, SB_HEADS * nq, 1), lambda b, s, pt: (layer, 0, 0)),
            pl.BlockSpec((None, 512, ps), lambda b, s, pt: (b, 0, 0)),
            pl.BlockSpec((None, 512, ps), lambda b, s, pt: (b, 0, 0)),
        ] + [page_spec(r) for r in range(pp)] + [page_spec(r) for r in range(pp)],
        out_specs=pl.BlockSpec((None, nq, 512), lambda b, s, pt: (b, 0, 0)),
        scratch_shapes=[pltpu.VMEM((SB_HEADS * nq, SB_DH), F32), pltpu.VMEM((SB_HEADS * nq, LANES), F32)],
    )
    return pl.pallas_call(
        kern,
        grid_spec=grid_spec,
        out_shape=jax.ShapeDtypeStruct((bs, nq, 512), F32),
        compiler_params=_params(("arbitrary", "arbitrary")),
        name="sb_sample",
    )(page_table.reshape(-1), q, bias_col, k_new_pad, v_new_pad, *([cache_k] * pp), *([cache_v] * pp))


def _merge_kernel(hm_ref, hg_ref, hs_ref, g0_ref, g1_ref, g2_ref, x_ref, gate_ref, wb_ref, wo_ref, o_ref):
    merged = jax.nn.sigmoid(g0_ref[...].astype(F32)) * _mm(hm_ref[...].astype(BF16), wb_ref[0])
    merged += jax.nn.sigmoid(g1_ref[...].astype(F32)) * _mm(hg_ref[...].astype(BF16), wb_ref[1])
    merged += jax.nn.sigmoid(g2_ref[...].astype(F32)) * _mm(hs_ref[...].astype(BF16), wb_ref[2])
    mix = _mm(merged.astype(BF16), wo_ref[...])
    o_ref[...] = x_ref[...] + gate_ref[...] * mix


def _merge(hm, hg, hs, main, x, mods, w_branch, w_out, layer, *, tm, rows_per_group):
    t, d = x.shape
    tiles_per_group = rows_per_group // tm
    tok = lambda blk: (lambda i: (i, blk))
    return pl.pallas_call(
        _merge_kernel,
        grid=(t // tm,),
        in_specs=[
            pl.BlockSpec((tm, 512), tok(0)),
            pl.BlockSpec((tm, 512), tok(0)),
            pl.BlockSpec((tm, 512), tok(0)),
            pl.BlockSpec((tm, d), tok(0)),
            pl.BlockSpec((tm, d), tok(1)),
            pl.BlockSpec((tm, d), tok(2)),
            pl.BlockSpec((tm, d), tok(0)),
            _mod_spec(mods, 2, layer, lambda i: i // tiles_per_group),
            pl.BlockSpec((None, N_BRANCH, BRANCH_W, d), lambda i: (layer, 0, 0, 0)),
            pl.BlockSpec((None, d, d), lambda i: (layer, 0, 0)),
        ],
        out_specs=pl.BlockSpec((tm, d), tok(0)),
        out_shape=jax.ShapeDtypeStruct((t, d), F32),
        compiler_params=_params(("arbitrary",)),
        name="branch_merge",
    )(hm, hg, hs, main, main, main, x, mods[0], w_branch, w_out)


FF_CHUNK = 256


def _ffn_kernel(x_ref, nw_ref, sc_ref, sh_ref, gate_ref, wu_ref, cw_ref, cb_ref, wd_ref, p1_ref, p2_ref, fw_ref,
                o_ref, u_ref, carry_ref, act_ref, *, TM, SEQ_IN_TILE, FINAL_NORM):
    x = x_ref[...]
    y = x * lax.rsqrt(jnp.mean(x * x, axis=-1, keepdims=True) + EPS)
    hb = (y * nw_ref[...] * (1.0 + sc_ref[...]) + sh_ref[...]).astype(BF16)
    row = lax.broadcasted_iota(jnp.int32, (TM, FF_CHUNK), 0)
    if SEQ_IN_TILE is None:
        @pl.when(pl.program_id(1) == 0)
        def _():
            carry_ref[...] = p1_ref[...]
        pos = row
    else:
        pos = row % SEQ_IN_TILE
    for c in range(D_FF // FF_CHUNK):
        cs = slice(c * FF_CHUNK, (c + 1) * FF_CHUNK)
        vs = slice(D_FF + c * FF_CHUNK, D_FF + (c + 1) * FF_CHUNK)
        u = _mm(hb, wu_ref[:, cs])
        val = _mm(hb, wu_ref[:, vs])
        if SEQ_IN_TILE is None:
            prev = carry_ref[:, cs]
            prev1 = prev[SUBLANES - 1:SUBLANES, :]
            prev2 = jnp.where(row == 0, prev[SUBLANES - 2:SUBLANES - 1, :], prev1)
            carry_ref[:, cs] = u[TM - SUBLANES:TM, :]
            u_ref[:, cs] = u[TM - SUBLANES:TM, :]
        else:
            prev1 = p1_ref[:, cs]
            prev2 = p2_ref[:, cs]
            u_ref[:, cs] = u
        u_m1 = jnp.where(pos >= 1, pltpu.roll(u, 1, axis=0), prev1)
        u_m2 = jnp.where(pos >= 2, pltpu.roll(u, 2, axis=0), prev2)
        cw = cw_ref[:, cs]
        conv = cb_ref[:, cs] + cw[0:1, :] * u_m2 + cw[1:2, :] * u_m1 + cw[2:3, :] * u
        act = 0.5 * conv * (1.0 + lax.erf(conv * (2.0 ** -0.5))) * val
        act_ref[:, cs] = act.astype(BF16)
    out = x + gate_ref[...] * _mm(act_ref[...], wd_ref[...])
    if FINAL_NORM:
        out = out * lax.rsqrt(jnp.mean(out * out, axis=-1, keepdims=True) + EPS) * fw_ref[...]
    o_ref[...] = out


def _ffn(x, nw, mods, w_up, conv_w, conv_b, w_down, p1, p2, final_w, layer, *, batch, seq, tm, seq_in_tile,
         final_norm):
    t, d = x.shape
    per_tile = seq_in_tile is not None
    if per_tile:
        grid = (1, t // tm)
        rowmap = lambda b, i: (i, 0)
        grp = lambda b, i: i
        pspec = pl.BlockSpec((None, tm, D_FF), lambda b, i: (layer, i, 0))
        uspec = pl.BlockSpec((tm, D_FF), rowmap)
        ushape = jax.ShapeDtypeStruct((t, D_FF), F32)
    else:
        nt = seq // tm
        grid = (batch, nt)
        rowmap = lambda b, i: (b * nt + i, 0)
        grp = lambda b, i: b
        pspec = pl.BlockSpec((None, None, SUBLANES, D_FF), lambda b, i: (layer, b, 0, 0))
        uspec = pl.BlockSpec((None, SUBLANES, D_FF), lambda b, i: (b, 0, 0))
        ushape = jax.ShapeDtypeStruct((batch, SUBLANES, D_FF), F32)
    const3 = lambda b, i: (layer, 0, 0)
    once = pl.Buffered(1)
    kern = functools.partial(_ffn_kernel, TM=tm, SEQ_IN_TILE=seq_in_tile, FINAL_NORM=final_norm)
    return pl.pallas_call(
        kern,
        grid=grid,
        in_specs=[
            pl.BlockSpec((tm, d), rowmap),
            pl.BlockSpec((None, 1, d), const3),
            _mod_spec(mods, 4, layer, grp),
            _mod_spec(mods, 3, layer, grp),
            _mod_spec(mods, 5, layer, grp),
            pl.BlockSpec((None, d, 2 * D_FF), const3, pipeline_mode=once),
            pl.BlockSpec((None, CONV_W, D_FF), const3),
            pl.BlockSpec((None, 1, D_FF), const3),
            pl.BlockSpec((None, D_FF, d), const3, pipeline_mode=once),
            pspec,
            pspec,
            pl.BlockSpec((1, d), lambda b, i: (0, 0)),
        ],
        out_specs=[pl.BlockSpec((tm, d), rowmap), uspec],
        out_shape=[jax.ShapeDtypeStruct((t, d), F32), ushape],
        scratch_shapes=[pltpu.VMEM((SUBLANES, D_FF), F32), pltpu.VMEM((tm, D_FF), BF16)],
        compiler_params=_params(("arbitrary", "arbitrary")),
        name="conv_ffn",
    )(x, nw, mods[0], mods[0], mods[0], w_up, conv_w, conv_b, w_down, p1, p2, final_w)


def _prep_weights(w_in, gla_w_gate, w_branch, w_out, w_up, w_down):
    widths = (256, 256, 512, 4, 4, 512, 256, 256, 512, GLA_RANK, 512, 512, 512, 512, N_BRANCH * D_MODEL)
    offs = np.concatenate([[0], np.cumsum(widths)])
    names = ("ml_q", "ml_k", "ml_v", "ml_i", "ml_f", "ml_o", "gla_q", "gla_k", "gla_v", "gla_lr", "gla_o",
             "sb_q", "sb_k", "sb_v", "gates")
    w_t = jnp.swapaxes(w_in, 1, 2)
    part = {nm: w_t[:, int(offs[i]):int(offs[i + 1]), :] for i, nm in enumerate(names)}
    order = ("gates", "ml_v", "ml_o", "gla_v", "gla_o", "sb_q", "ml_q", "ml_k", "gla_q", "gla_k")
    w_main = jnp.concatenate([part[nm] for nm in order], axis=1).astype(BF16)
    w_kt = part["sb_k"].astype(BF16)
    w_vt = part["sb_v"].astype(BF16)
    small = jnp.concatenate([part["ml_i"], part["ml_f"], part["gla_lr"]], axis=1)
    w_small = jnp.pad(small, ((0, 0), (0, SMALL_W - small.shape[1]), (0, 0))).astype(BF16)
    wg_pad = jnp.pad(gla_w_gate, ((0, 0), (2 * ML_HEADS, SMALL_W - 2 * ML_HEADS - GLA_RANK), (0, 0))).astype(BF16)
    return ((w_main, w_small, w_kt, w_vt), wg_pad, w_branch.astype(BF16), w_out.astype(BF16),
            w_up.astype(BF16), w_down.astype(BF16))


def _run_group(x, mod, states, paged, weights, params, *, tm, scan_len, sb_block):
    (w_proj, wg_pad, w_branch, w_out, w_up, w_down) = weights
    (norm1_w, norm2_w, ml_b_i, ml_b_f, ml_norm_w, gla_b_gate, gla_norm_w, sb_bias, conv_w, conv_b, final_w) = params
    ml_c0, ml_n0, ml_m0, gla_s0, conv0 = states
    batch, seq, d = x.shape
    depth = w_proj[0].shape[0]
    t = batch * seq
    per_seq = seq < tm
    xt = x.reshape(t, d)
    scan_dtype = F32 if per_seq else BF16
    kv_groups = 1 if per_seq else batch
    kbuf = jnp.zeros((depth, kv_groups, SB_HEADS * SB_DH, t // kv_groups), F32)
    vbuf = jnp.zeros((depth, kv_groups, SB_HEADS * SB_DH, t // kv_groups), F32)
    mcs, mns, mms, gss, bufs = [], [], [], [], []
    rows_per_group = tm if per_seq else seq
    tm_proj = tm if per_seq else math.gcd(seq, 2 * tm)
    if per_seq:
        mods = (jnp.repeat(mod, seq, axis=1).transpose(2, 0, 1, 3).reshape(6 * depth * (t // tm), tm, d), depth)
    else:
        mods = (mod.transpose(2, 0, 1, 3).reshape(6 * depth * batch, 1, d), depth)
    gate_bias = jnp.concatenate([ml_b_i, ml_b_f], axis=1)
    b_row = jnp.pad(gate_bias, ((0, 0), (0, SMALL_W - 2 * ML_HEADS))).reshape(depth, 1, SMALL_W)
    b_col = jnp.pad(gate_bias, ((0, 0), (0, SMALL_T_ROWS - 2 * ML_HEADS))).reshape(depth, SMALL_T_ROWS, 1)
    ml_m0 = ml_m0.reshape(depth, batch, ML_HEADS, 1)
    gla_s0t = jnp.swapaxes(gla_s0, -1, -2)
    if paged is None:
        sb_bias_all = jnp.repeat(sb_bias, SB_DH, axis=1).reshape(depth, 1, SB_HEADS * SB_DH)
        conv_p1 = conv_p2 = jnp.pad(conv0, ((0, 0), (0, 0), (SUBLANES - (CONV_W - 1), 0), (0, 0)))
    else:
        sb_bias_all = jnp.repeat(sb_bias, seq, axis=1).reshape(depth, SB_HEADS * seq, 1)
        zero = jnp.zeros((depth, batch, seq - 1, D_FF), F32)
        conv_p1 = jnp.concatenate([conv0[:, :, 1:2], zero], axis=2).reshape(depth, t, D_FF)
        conv_p2 = jnp.concatenate([conv0, zero[:, :, 1:]], axis=2).reshape(depth, t, D_FF)
    for l in range(depth):
        last = l == depth - 1
        main, small, small_t, kbuf, vbuf = _norm_proj(xt, norm1_w, mods, w_proj, kbuf, vbuf, l,
                                                      tm=tm_proj, rows_per_group=rows_per_group,
                                                      main_dtype=scan_dtype)
        if per_seq:
            small_t3 = small_t.reshape(SMALL_T_ROWS, batch, seq).transpose(1, 0, 2)
        else:
            small_t3 = small_t.reshape(1, SMALL_T_ROWS, t)
        ml = _mlstm(main, small, small_t3, b_row, b_col, ml_norm_w, ml_c0, ml_n0, ml_m0, l,
                    batch=batch, seq=seq, L=scan_len, out_dtype=scan_dtype, per_seq=per_seq)
        gla = _gla(main, small, wg_pad, gla_b_gate, gla_norm_w, gla_s0t, l,
                   batch=batch, seq=seq, L=scan_len, out_dtype=scan_dtype)
        hm, mc, mn, mm, hg, gst = _scans(ml, gla, batch=batch, seq=seq, L=scan_len)
        if paged is None:
            hs = _sb_prompt(main, kbuf, vbuf, sb_bias_all, l, batch=batch, seq=seq, tq=sb_block[0], tk=sb_block[1])
        else:
            cache_kt, cache_vt, page_table = paged
            ps = cache_kt.shape[-1]
            q = main[:, BLK_SB_Q * 512:(BLK_SB_Q + 1) * 512].reshape(batch, seq, 512)
            new_t = lambda buf: jnp.pad(buf[l, 0].reshape(-1, batch, seq).transpose(1, 0, 2),
                                        ((0, 0), (0, 0), (0, ps - seq)))
            hs = _sb_sample(page_table, q, sb_bias_all, new_t(kbuf), new_t(vbuf), cache_kt, cache_vt, l,
                            pages_per_step=math.gcd(page_table.shape[1], 16)).reshape(t, 512)
        xt = _merge(hm, hg, hs, main, xt, mods, w_branch, w_out, l, tm=tm, rows_per_group=rows_per_group)
        if per_seq:
            xt, u = _ffn(xt, norm2_w, mods, w_up, conv_w, conv_b, w_down, conv_p1, conv_p2, final_w, l,
                         batch=batch, seq=seq, tm=tm, seq_in_tile=seq, final_norm=last)
            full = jnp.concatenate([conv0[l], u.reshape(batch, seq, D_FF)], axis=1)
            buf = full[:, seq:]
        else:
            xt, tail = _ffn(xt, norm2_w, mods, w_up, conv_w, conv_b, w_down, conv_p1, conv_p2, final_w, l,
                            batch=batch, seq=seq, tm=tm, seq_in_tile=None, final_norm=last)
            buf = tail[:, SUBLANES - (CONV_W - 1):]
        mcs.append(mc)
        mns.append(mn)
        mms.append(mm.reshape(batch, ML_HEADS))
        gss.append(jnp.swapaxes(gst, -1, -2))
        bufs.append(buf)
    st = jnp.stack

    def kv_out(buf):
        b6 = buf.reshape(depth, kv_groups, SB_HEADS, SB_DH, batch // kv_groups, seq)
        return b6.transpose(0, 1, 4, 5, 2, 3).reshape(depth, batch, seq, SB_HEADS, SB_DH)

    return xt, kv_out(kbuf), kv_out(vbuf), st(mcs), st(mns), st(mms), st(gss), st(bufs)


def kernel(x_prompt, x_sample, cache_k, cache_v, state_mlstm_c, state_mlstm_n, state_mlstm_m, state_gla, state_conv, page_table, c_prompt, c_sample, w_ada, b_ada, norm1_w, norm2_w, w_in, ml_b_i, ml_b_f, ml_norm_w, gla_w_gate, gla_b_gate, gla_norm_w, sb_bias, w_branch, w_out, w_up, conv_w, conv_b, w_down, final_norm_w):
    depth = w_in.shape[0]
    bp, sp, d = x_prompt.shape
    bs, ss, _ = x_sample.shape
    weights = _prep_weights(w_in, gla_w_gate, w_branch, w_out, w_up, w_down)
    params = (norm1_w.reshape(depth, 1, d), norm2_w.reshape(depth, 1, d), ml_b_i, ml_b_f,
              ml_norm_w.reshape(depth, 1, -1),
              gla_b_gate.reshape(depth, 1, -1), gla_norm_w.reshape(depth, 1, -1), sb_bias, conv_w,
              conv_b.reshape(depth, 1, -1), final_norm_w.reshape(1, d))
    mod = _modulation(jnp.concatenate([c_prompt, c_sample], axis=0), w_ada, b_ada)
    mod_p = mod[:, :bp].reshape(depth, bp, 6, d)
    mod_s = mod[:, bp:].reshape(depth, bs, 6, d)

    zeros = lambda *shape: jnp.zeros((depth, bp) + shape, F32)
    states_p = (zeros(ML_HEADS, ML_DK, ML_DV), zeros(ML_HEADS, ML_DK),
                jnp.full((depth, bp, ML_HEADS), NEG_BIG, F32), zeros(GLA_HEADS, GLA_DK, GLA_DV),
                zeros(CONV_W - 1, D_FF))
    tm_p = math.gcd(sp, 512)
    scan_p = math.gcd(sp, 512)
    out_p = _run_group(x_prompt, mod_p, states_p, None, weights, params, tm=tm_p, scan_len=scan_p,
                       sb_block=(math.gcd(sp, 512), math.gcd(sp, 256)))

    states_s = (state_mlstm_c, state_mlstm_n, state_mlstm_m, state_gla, state_conv)
    cache_kt = jnp.transpose(cache_k, (0, 1, 3, 4, 2))
    cache_vt = jnp.transpose(cache_v, (0, 1, 3, 4, 2))
    out_s = _run_group(x_sample, mod_s, states_s, (cache_kt, cache_vt, page_table), weights, params,
                       tm=bs * ss, scan_len=ss, sb_block=None)

    y_p = out_p[0].reshape(bp, sp, d)
    y_s = out_s[0].reshape(bs, ss, d)
    (_, k_p, v_p, mc_p, mn_p, mm_p, g_p, cb_p) = out_p
    (_, k_s, v_s, mc_s, mn_s, mm_s, g_s, cb_s) = out_s
    return (y_p, y_s, k_p, v_p, k_s, v_s, mc_p, mn_p, mm_p, mc_s, mn_s, mm_s, g_p, g_s, cb_p, cb_s)
```

```python
import functools
import math

import numpy as np
import jax
import jax.numpy as jnp
from jax import lax
from jax.experimental import pallas as pl
from jax.experimental.pallas import tpu as pltpu

F32 = jnp.float32
BF16 = jnp.bfloat16

D_MODEL = 1024
ML_HEADS, ML_DK, ML_DV = 4, 64, 128
GLA_HEADS, GLA_DK, GLA_DV = 4, 64, 128
GLA_RANK = 16
GLA_TAU = 16.0
SB_HEADS, SB_DH = 8, 64
BRANCH_W = 512
N_BRANCH = 3
D_FF = 2816
CONV_W = 3
EPS = 1e-6
NEG_BIG = -1e30
SCAN_CHUNK = 64

LANES = 128
SUBLANES = 8
VMEM_LIMIT = 56 * 1024 * 1024

N_MAIN = 6656
BLK_ML_V, BLK_ML_O, BLK_GLA_V, BLK_GLA_O, BLK_SB_Q = 6, 7, 8, 9, 10
BLK_ML_Q, BLK_ML_K, BLK_GLA_Q, BLK_GLA_K = 22, 23, 24, 25
MAIN_TN = 1664
LOG2E = 1.4426950408889634
SMALL_W = 128
SMALL_T_ROWS = 32


def _nt(a, b, precision=None):
    return lax.dot_general(a, b, (((1,), (1,)), ((), ())), preferred_element_type=F32, precision=precision)


def _tn(a, b):
    return lax.dot_general(a, b, (((0,), (0,)), ((), ())), preferred_element_type=F32)


def _mm(a, b, precision=None):
    return jnp.dot(a, b, preferred_element_type=F32, precision=precision)


def _cumsum_mm(tri, x, *, tri_first):
    t = tri.astype(BF16)
    hi = x.astype(BF16)
    r1 = x - hi.astype(F32)
    mid = r1.astype(BF16)
    lo = (r1 - mid.astype(F32)).astype(BF16)
    parts = (hi, mid, lo)
    terms = [_mm(t, p) if tri_first else _mm(p, t) for p in parts]
    return terms[0] + (terms[1] + terms[2])


def _log_sigmoid(x):
    return jnp.minimum(x, 0.0) - jnp.log1p(jnp.exp(-jnp.abs(x)))


def _softplus2(x):
    neg_abs = lax.bitcast_convert_type(lax.bitcast_convert_type(x, jnp.uint32) | jnp.uint32(0x80000000), F32)
    return jnp.maximum(x, 0.0) + jnp.log2(1.0 + jnp.exp2(neg_abs))


def _params(sem):
    return pltpu.CompilerParams(dimension_semantics=sem, vmem_limit_bytes=VMEM_LIMIT)


def _mod_kernel(c_ref, w_ref, b_ref, o_ref):
    c = c_ref[...]
    s = c * jax.nn.sigmoid(c)
    o_ref[...] = _mm(s.astype(BF16), w_ref[...].astype(BF16)) + b_ref[...]


def _modulation(c_all, w_ada, b_ada):
    depth, d, n = w_ada.shape
    nb = c_all.shape[0]
    tn = 1536
    return pl.pallas_call(
        _mod_kernel,
        grid=(depth, n // tn),
        in_specs=[
            pl.BlockSpec((nb, d), lambda l, j: (0, 0)),
            pl.BlockSpec((None, d, tn), lambda l, j: (l, 0, j)),
            pl.BlockSpec((None, 1, tn), lambda l, j: (l, 0, j)),
        ],
        out_specs=pl.BlockSpec((None, nb, tn), lambda l, j: (l, 0, j)),
        out_shape=jax.ShapeDtypeStruct((depth, nb, n), F32),
        compiler_params=_params(("arbitrary", "arbitrary")),
        name="adaln_mod",
    )(c_all, w_ada, b_ada.reshape(depth, 1, n))


def _mod_spec(mods, k, layer, group_of):
    arr, depth = mods
    groups = arr.shape[0] // (6 * depth)
    base = (k * depth + layer) * groups
    return pl.BlockSpec((None,) + arr.shape[1:], lambda *g: (base + group_of(*g), 0, 0))


def _normmm_kernel(x_ref, nw_ref, sc_ref, sh_ref, w_ref, ws_ref, wk_ref, wv_ref, kbuf_ref, vbuf_ref,
                   o_ref, os_ref, ot_ref, kt_ref, vt_ref, h_scr):
    del kbuf_ref, vbuf_ref

    @pl.when(pl.program_id(1) == 0)
    def _():
        x = x_ref[...]
        y = x * lax.rsqrt(jnp.mean(x * x, axis=-1, keepdims=True) + EPS)
        h = y * nw_ref[...] * (1.0 + sc_ref[...]) + sh_ref[...]
        hb = h.astype(BF16)
        h_scr[...] = hb
        os_ref[...] = _nt(hb, ws_ref[...])
        ot_ref[...] = _nt(ws_ref[0:SMALL_T_ROWS, :], hb)
        kt_ref[...] = _nt(wk_ref[...], hb)
        vt_ref[...] = _nt(wv_ref[...], hb)

    o_ref[...] = _nt(h_scr[...], w_ref[...]).astype(o_ref.dtype)


def _norm_proj(x, nw, mods, weights, kbuf, vbuf, layer, *, tm, rows_per_group, main_dtype):
    w_main, w_small, w_kt, w_vt = weights
    t, d = x.shape
    n = w_main.shape[1]
    tn = MAIN_TN
    tiles_per_group = rows_per_group // tm
    grp = lambda i, j: i // tiles_per_group
    kv_rows = w_kt.shape[1]
    kv_map = lambda i, j: (layer, i // tiles_per_group, 0, i % tiles_per_group)
    wmap = lambda i, j: (layer, 0, 0)
    return pl.pallas_call(
        _normmm_kernel,
        grid=(t // tm, n // tn),
        in_specs=[
            pl.BlockSpec((tm, d), lambda i, j: (i, 0)),
            pl.BlockSpec((None, 1, d), wmap),
            _mod_spec(mods, 1, layer, grp),
            _mod_spec(mods, 0, layer, grp),
            pl.BlockSpec((None, tn, d), lambda i, j: (layer, j, 0)),
            pl.BlockSpec((None, SMALL_W, d), wmap),
            pl.BlockSpec((None, kv_rows, d), wmap),
            pl.BlockSpec((None, kv_rows, d), wmap),
            pl.BlockSpec(memory_space=pl.ANY),
            pl.BlockSpec(memory_space=pl.ANY),
        ],
        out_specs=[
            pl.BlockSpec((tm, tn), lambda i, j: (i, j)),
            pl.BlockSpec((tm, SMALL_W), lambda i, j: (i, 0)),
            pl.BlockSpec((SMALL_T_ROWS, tm), lambda i, j: (0, i)),
            pl.BlockSpec((None, None, kv_rows, tm), kv_map),
            pl.BlockSpec((None, None, kv_rows, tm), kv_map),
        ],
        out_shape=[
            jax.ShapeDtypeStruct((t, n), main_dtype),
            jax.ShapeDtypeStruct((t, SMALL_W), F32),
            jax.ShapeDtypeStruct((SMALL_T_ROWS, t), F32),
            jax.ShapeDtypeStruct(kbuf.shape, F32),
            jax.ShapeDtypeStruct(vbuf.shape, F32),
        ],
        input_output_aliases={8: 3, 9: 4},
        scratch_shapes=[pltpu.VMEM((tm, d), BF16)],
        compiler_params=_params(("arbitrary", "arbitrary")),
        name="norm_in_proj",
    )(x, nw, mods[0], mods[0], w_main, w_small, w_kt, w_vt, kbuf, vbuf)


def _mlstm_kernel(q_ref, k_ref, v_ref, o_ref, sc_ref, st_ref, brow_ref, bcol_ref, nw_ref, c0_ref, n0_ref, m0_ref,
                  hn_ref, c_ref, n_ref, m_ref, *, L):
    @pl.when(pl.program_id(1) == 0)
    def _():
        c_ref[...] = c0_ref[...]
        n_ref[...] = n0_ref[...]
        m_ref[...] = m0_ref[...]

    row = lax.broadcasted_iota(jnp.int32, (L, L), 0)
    col = lax.broadcasted_iota(jnp.int32, (L, L), 1)
    causal = col <= row
    small = sc_ref[...] + brow_ref[...]
    small_t = st_ref[...] + bcol_ref[...]
    bc_all = _cumsum_mm(causal, _log_sigmoid(small), tri_first=True)
    br_all = _cumsum_mm(row <= col, _log_sigmoid(small_t), tri_first=False)
    q = q_ref[...].astype(F32) * (ML_DK ** -0.5)
    k = k_ref[...].astype(F32)
    v = v_ref[...].astype(F32)
    og = o_ref[...].astype(F32)
    nw = nw_ref[...]
    heads = range(ML_HEADS)
    ks = [slice(h * ML_DK, (h + 1) * ML_DK) for h in heads]
    vs = [slice(h * ML_DV, (h + 1) * ML_DV) for h in heads]
    qb = [q[:, ks[h]].astype(BF16) for h in heads]
    vb = [v[:, vs[h]].astype(BF16) for h in heads]
    qk = [_nt(qb[h], k[:, ks[h]].astype(BF16)) for h in heads]
    b_col = [bc_all[:, ML_HEADS + h:ML_HEADS + h + 1] for h in heads]
    b_row = [br_all[ML_HEADS + h:ML_HEADS + h + 1, :] for h in heads]
    m_prev = [m_ref[h:h + 1, :] for h in heads]
    c_st = [c_ref[h] for h in heads]
    n_st = [n_ref[h:h + 1, :] for h in heads]
    dmat = [jnp.where(causal, b_col[h] - b_row[h] + small_t[h:h + 1, :], -jnp.inf) for h in heads]
    inter = [b_col[h] + m_prev[h] for h in heads]
    m_t = [jnp.maximum(inter[h], jnp.max(dmat[h], axis=-1, keepdims=True)) for h in heads]
    s = [qk[h] * jnp.exp(dmat[h] - m_t[h]) for h in heads]
    e_inter = [jnp.exp(inter[h] - m_t[h]) for h in heads]
    qc = [_mm(qb[h], c_st[h].astype(BF16)) for h in heads]
    num = [_mm(s[h].astype(BF16), vb[h]) + e_inter[h] * qc[h] for h in heads]
    den = [jnp.sum(s[h], axis=-1, keepdims=True)
           + e_inter[h] * jnp.sum(q[:, ks[h]] * n_st[h], axis=-1, keepdims=True) for h in heads]
    hh = [num[h] / jnp.maximum(jnp.abs(den[h]), jnp.exp(-m_t[h])) for h in heads]
    for h in heads:
        m_new = m_t[h][L - 1:L, :]
        b_last = b_col[h][L - 1:L, :]
        wk = jnp.exp(b_last - b_col[h] + small[:, h:h + 1] - m_new)
        decay = jnp.exp(b_last + m_prev[h] - m_new)
        kw = k[:, ks[h]] * wk
        c_ref[h] = decay * c_st[h] + _tn(kw.astype(BF16), vb[h])
        n_ref[h:h + 1, :] = decay * n_st[h] + jnp.sum(kw, axis=0, keepdims=True)
        m_ref[h:h + 1, :] = m_new
    for h in heads:
        y = hh[h] * lax.rsqrt(jnp.mean(hh[h] * hh[h], axis=-1, keepdims=True) + EPS) * nw[:, vs[h]]
        hn_ref[:, vs[h]] = (y * jax.nn.sigmoid(og[:, vs[h]])).astype(hn_ref.dtype)


def _small_t_spec(L, nc, per_seq):
    if per_seq:
        return pl.BlockSpec((None, SMALL_T_ROWS, L), lambda b, c: (b, 0, 0))
    return pl.BlockSpec((None, SMALL_T_ROWS, L), lambda b, c: (0, 0, b * nc + c))


def _mlstm(main, small, small_t3, b_row, b_col, nw, c0, n0, m0, layer, *, batch, seq, L, out_dtype, per_seq):
    nc = seq // L
    t = batch * seq
    tok = lambda blk: (lambda b, c: (b * nc + c, blk))
    st = lambda b, c: (b, 0, 0)
    return dict(
        in_specs=[
            pl.BlockSpec((L, 256), tok(BLK_ML_Q)),
            pl.BlockSpec((L, 256), tok(BLK_ML_K)),
            pl.BlockSpec((L, 512), tok(BLK_ML_V)),
            pl.BlockSpec((L, 512), tok(BLK_ML_O)),
            pl.BlockSpec((L, SMALL_W), tok(0)),
            _small_t_spec(L, nc, per_seq),
            pl.BlockSpec((None, 1, SMALL_W), lambda b, c: (layer, 0, 0)),
            pl.BlockSpec((None, SMALL_T_ROWS, 1), lambda b, c: (layer, 0, 0)),
            pl.BlockSpec((None, 1, 512), lambda b, c: (layer, 0, 0)),
            pl.BlockSpec((None, None, ML_HEADS, ML_DK, ML_DV), lambda b, c: (layer, b, 0, 0, 0)),
            pl.BlockSpec((None, None, ML_HEADS, ML_DK), lambda b, c: (layer, b, 0, 0)),
            pl.BlockSpec((None, None, ML_HEADS, 1), lambda b, c: (layer, b, 0, 0)),
        ],
        out_specs=[
            pl.BlockSpec((L, 512), tok(0)),
            pl.BlockSpec((None, ML_HEADS, ML_DK, ML_DV), lambda b, c: (b, 0, 0, 0)),
            pl.BlockSpec((None, ML_HEADS, ML_DK), st),
            pl.BlockSpec((None, ML_HEADS, 1), st),
        ],
        out_shape=[
            jax.ShapeDtypeStruct((t, 512), out_dtype),
            jax.ShapeDtypeStruct((batch, ML_HEADS, ML_DK, ML_DV), F32),
            jax.ShapeDtypeStruct((batch, ML_HEADS, ML_DK), F32),
            jax.ShapeDtypeStruct((batch, ML_HEADS, 1), F32),
        ],
        operands=(main, main, main, main, small, small_t3, b_row, b_col, nw, c0, n0, m0),
    )


def _gla_kernel(q_ref, k_ref, v_ref, o_ref, sc_ref, wg_ref, bg_ref, nw_ref, s0_ref, hn_ref, s_ref, *, L, LS):
    @pl.when(pl.program_id(1) == 0)
    def _():
        s_ref[...] = s0_ref[...]

    row = lax.broadcasted_iota(jnp.int32, (LS, LS), 0)
    col = lax.broadcasted_iota(jnp.int32, (LS, LS), 1)
    causal = col <= row
    la_all = _log_sigmoid(_mm(sc_ref[...].astype(BF16), wg_ref[...]) + bg_ref[...]) * (1.0 / GLA_TAU)
    nw = nw_ref[...]
    heads = range(GLA_HEADS)
    ks = [slice(h * GLA_DK, (h + 1) * GLA_DK) for h in heads]
    vs = [slice(h * GLA_DV, (h + 1) * GLA_DV) for h in heads]
    pre = []
    for c in range(L // LS):
        rs = slice(c * LS, (c + 1) * LS)
        bc = _cumsum_mm(causal, la_all[rs, :], tri_first=True)
        ref_row = bc[LS // 2:LS // 2 + 1, :]
        last = bc[LS - 1:LS, :]
        q = q_ref[rs, :].astype(F32) * (GLA_DK ** -0.5)
        k = k_ref[rs, :].astype(F32)
        qe = (q * jnp.exp(bc - ref_row)).astype(BF16)
        ke = (k * jnp.exp(ref_row - bc)).astype(BF16)
        qs = (q * jnp.exp(bc)).astype(BF16)
        kl = (k * jnp.exp(last - bc)).astype(BF16)
        el = jnp.exp(last)
        v = v_ref[rs, :].astype(F32)
        vb = [v[:, vs[h]].astype(BF16) for h in heads]
        av = [_mm(jnp.where(causal, _nt(qe[:, ks[h]], ke[:, ks[h]]), 0.0).astype(BF16), vb[h]) for h in heads]
        pre.append((rs, qs, kl, el, vb, av))
    state = [s_ref[h] for h in heads]
    outs = []
    for rs, qs, kl, el, vb, av in pre:
        o = [av[h] + _nt(qs[:, ks[h]], state[h].astype(BF16)) for h in heads]
        state = [el[:, ks[h]] * state[h] + _tn(vb[h], kl[:, ks[h]]) for h in heads]
        outs.append((rs, o))
    for h in heads:
        s_ref[h] = state[h]
    for rs, o in outs:
        og = o_ref[rs, :].astype(F32)
        for h in heads:
            y = o[h] * lax.rsqrt(jnp.mean(o[h] * o[h], axis=-1, keepdims=True) + EPS) * nw[:, vs[h]]
            g = og[:, vs[h]]
            hn_ref[rs, vs[h]] = (y * (g * jax.nn.sigmoid(g))).astype(hn_ref.dtype)


def _gla(main, small, wg_pad, bg, nw, s0t, layer, *, batch, seq, L, out_dtype):
    nc = seq // L
    t = batch * seq
    tok = lambda blk: (lambda b, c: (b * nc + c, blk))
    return dict(
        in_specs=[
            pl.BlockSpec((L, 256), tok(BLK_GLA_Q)),
            pl.BlockSpec((L, 256), tok(BLK_GLA_K)),
            pl.BlockSpec((L, 512), tok(BLK_GLA_V)),
            pl.BlockSpec((L, 512), tok(BLK_GLA_O)),
            pl.BlockSpec((L, SMALL_W), tok(0)),
            pl.BlockSpec((None, SMALL_W, 256), lambda b, c: (layer, 0, 0)),
            pl.BlockSpec((None, 1, 256), lambda b, c: (layer, 0, 0)),
            pl.BlockSpec((None, 1, 512), lambda b, c: (layer, 0, 0)),
            pl.BlockSpec((None, None, GLA_HEADS, GLA_DV, GLA_DK), lambda b, c: (layer, b, 0, 0, 0)),
        ],
        out_specs=[
            pl.BlockSpec((L, 512), tok(0)),
            pl.BlockSpec((None, GLA_HEADS, GLA_DV, GLA_DK), lambda b, c: (b, 0, 0, 0)),
        ],
        out_shape=[
            jax.ShapeDtypeStruct((t, 512), out_dtype),
            jax.ShapeDtypeStruct((batch, GLA_HEADS, GLA_DV, GLA_DK), F32),
        ],
        operands=(main, main, main, main, small, wg_pad, bg, nw, s0t),
    )


def _scan_kernel(*refs, N_ML_IN, N_GLA_IN, N_ML_OUT, L, LS):
    ml_in = refs[:N_ML_IN]
    gla_in = refs[N_ML_IN:N_ML_IN + N_GLA_IN]
    outs = refs[N_ML_IN + N_GLA_IN:]
    _mlstm_kernel(*ml_in, *outs[:N_ML_OUT], L=L)
    _gla_kernel(*gla_in, *outs[N_ML_OUT:], L=L, LS=LS)


def _scans(ml, gla, *, batch, seq, L):
    kern = functools.partial(_scan_kernel, N_ML_IN=len(ml["in_specs"]), N_GLA_IN=len(gla["in_specs"]),
                             N_ML_OUT=len(ml["out_specs"]), L=L, LS=math.gcd(L, SCAN_CHUNK))
    return pl.pallas_call(
        kern,
        grid=(batch, seq // L),
        in_specs=ml["in_specs"] + gla["in_specs"],
        out_specs=ml["out_specs"] + gla["out_specs"],
        out_shape=ml["out_shape"] + gla["out_shape"],
        compiler_params=_params(("arbitrary", "arbitrary")),
        name="mlstm_gla_scan",
    )(*ml["operands"], *gla["operands"])


SB_HEAD_GROUP = 2


def _sb_prompt_kernel(qi_ref, kj_ref, q_ref, kt_ref, vt_ref, bias_ref, o_ref, acc_ref, r_ref, qa_ref, *, TQ, TK):
    p = pl.program_id(1)
    i = qi_ref[p]
    j = kj_ref[p]
    ratio = TQ // TK
    head = [slice(h * SB_DH, (h + 1) * SB_DH) for h in range(SB_HEADS)]

    @pl.when(j == ratio * i + ratio - 1)
    def _():
        acc_ref[...] = jnp.zeros_like(acc_ref)
        r_ref[...] = jnp.zeros_like(r_ref)
        qf = q_ref[...].astype(F32) * (SB_DH ** -0.5 * LOG2E)
        lane = lax.broadcasted_iota(jnp.int32, (TQ, LANES), 1)
        ones = jnp.where(lane < SB_DH + 2, 1.0, 0.0)
        for h in range(SB_HEADS):
            pair = qf[:, (h // 2) * LANES:(h // 2 + 1) * LANES]
            if h % 2:
                pair = pltpu.roll(pair, SB_DH, axis=1)
            qa_ref[h] = jnp.where(lane < SB_DH, pair, ones).astype(BF16)

    row = lax.broadcasted_iota(jnp.int32, (TK, TK), 0)
    col = lax.broadcasted_iota(jnp.int32, (TK, TK), 1)
    suffix = (row >= col).astype(BF16)
    bias = bias_ref[...] * LOG2E
    bias_hi = bias.astype(BF16).astype(F32)
    bias_lo = bias - bias_hi
    krow = lax.broadcasted_iota(jnp.int32, (SB_DH, TK), 0)

    def keys(h):
        b = slice(h * SB_DH, h * SB_DH + 1)
        extra = jnp.where(krow == 0, bias_hi[:, b], jnp.where(krow == 1, bias_lo[:, b], 0.0))
        return jnp.concatenate([kt_ref[head[h], :].astype(BF16), extra.astype(BF16)], axis=0)

    def run(rows, diagonal):
        valid = col < row
        for g in range(0, SB_HEADS, SB_HEAD_GROUP):
            hh = range(g, g + SB_HEAD_GROUP)
            zs = [_mm(qa_ref[h, rows, :], keys(h)) for h in hh]
            sps = [_softplus2(z) for z in zs]
            if diagonal:
                sps = [jnp.where(valid, sp, 0.0) for sp in sps]
            rests = [_mm(sp.astype(BF16), suffix) for sp in sps]
            rs = [r_ref[rows, h:h + 1] for h in hh]
            probs = [jnp.exp2(z - rest - r) for z, rest, r in zip(zs, rests, rs)]
            if diagonal:
                probs = [jnp.where(valid, a, 0.0) for a in probs]
            for h, a, r, rest in zip(hh, probs, rs, rests):
                acc_ref[h, rows, :] += _nt(a.astype(BF16), vt_ref[head[h], :].astype(BF16))
                r_ref[rows, h:h + 1] = r + rest[:, 0:1]

    for part in range(ratio):
        rows = slice(part * TK, (part + 1) * TK)
        pl.when(j == ratio * i + part)(functools.partial(run, rows, True))
        if part > 0:
            pl.when(jnp.logical_and(j < ratio * i + part, j >= ratio * i))(functools.partial(run, rows, False))
    pl.when(j < ratio * i)(functools.partial(run, slice(0, TQ), False))

    @pl.when(j == 0)
    def _():
        for h in range(SB_HEADS):
            o_ref[:, head[h]] = acc_ref[h].astype(o_ref.dtype)


def _sb_prompt(main, kbuf, vbuf, bias_row, layer, *, batch, seq, tq, tk):
    nq = seq // tq
    ratio = tq // tk
    qi = np.concatenate([np.full(ratio * (i + 1), i) for i in range(nq)]).astype(np.int32)
    kj = np.concatenate([np.arange(ratio * (i + 1) - 1, -1, -1) for i in range(nq)]).astype(np.int32)
    t = batch * seq
    kern = functools.partial(_sb_prompt_kernel, TQ=tq, TK=tk)
    kv_spec = pl.BlockSpec((None, None, SB_HEADS * SB_DH, tk), lambda b, p, qi, kj: (layer, b, 0, kj[p]))
    grid_spec = pltpu.PrefetchScalarGridSpec(
        num_scalar_prefetch=2,
        grid=(batch, len(qi)),
        in_specs=[
            pl.BlockSpec((tq, 512), lambda b, p, qi, kj: (b * nq + qi[p], BLK_SB_Q)),
            kv_spec,
            kv_spec,
            pl.BlockSpec((None, 1, 512), lambda b, p, qi, kj: (layer, 0, 0)),
        ],
        out_specs=pl.BlockSpec((tq, 512), lambda b, p, qi, kj: (b * nq + qi[p], 0)),
        scratch_shapes=[pltpu.VMEM((SB_HEADS, tq, SB_DH), F32), pltpu.VMEM((tq, LANES), F32),
                        pltpu.VMEM((SB_HEADS, tq, LANES), BF16)],
    )
    return pl.pallas_call(
        kern,
        grid_spec=grid_spec,
        out_shape=jax.ShapeDtypeStruct((t, 512), BF16),
        compiler_params=_params(("arbitrary", "arbitrary")),
        name="sb_prompt",
    )(jnp.asarray(qi), jnp.asarray(kj), main, kbuf, vbuf, bias_row)


def _sb_sample_kernel(pt_ref, q_ref, bias_ref, kn_ref, vn_ref, *rest, PP, PS, NQ):
    kp = rest[:PP]
    vp = rest[PP:2 * PP]
    o_ref, acc_ref, r_ref = rest[2 * PP:]
    s = pl.program_id(1)
    hq = SB_HEADS * NQ
    q = (q_ref[...] * (SB_DH ** -0.5 * LOG2E)).astype(BF16)
    bias = bias_ref[...] * LOG2E
    group = 2 if PP % 2 == 0 else 1
    row = lax.broadcasted_iota(jnp.int32, (group * PS, group * PS), 0)
    col = lax.broadcasted_iota(jnp.int32, (group * PS, group * PS), 1)
    suffix = (row >= col).astype(BF16)

    def logits(get_kt):
        return jnp.concatenate(
            [_mm(q[:, h * SB_DH:(h + 1) * SB_DH], get_kt(h).astype(BF16)) for h in range(SB_HEADS)], axis=0) + bias

    def suffix_sum(sp):
        nk = sp.shape[1]
        return _mm(sp.astype(BF16), suffix[:nk, :nk])

    def attend(z, rest_, r, get_vt, valid):
        a = jnp.exp2(z - rest_ - r)
        if valid is not None:
            a = jnp.where(valid, a, 0.0)
        for h in range(SB_HEADS):
            rows = slice(h * NQ, (h + 1) * NQ)
            acc_ref[rows, :] += _nt(a[rows, :].astype(BF16), get_vt(h).astype(BF16))
        return r + rest_[:, 0:1]

    @pl.when(s == 0)
    def _():
        acc_ref[...] = jnp.zeros_like(acc_ref)
        key = lax.broadcasted_iota(jnp.int32, (hq, PS), 1)
        qpos = lax.broadcasted_iota(jnp.int32, (hq, PS), 0) % NQ
        valid = key < qpos
        z = logits(lambda h: kn_ref[h * SB_DH:(h + 1) * SB_DH, :])
        rest_ = suffix_sum(jnp.where(valid, _softplus2(z), 0.0))
        r = attend(z, rest_, 0.0, lambda h: vn_ref[h * SB_DH:(h + 1) * SB_DH, :], valid)
        r_ref[...] = jnp.broadcast_to(r, r_ref.shape)

    def pages(refs, p0):
        return lambda h: jnp.concatenate([refs[p][h] for p in range(p0 + group - 1, p0 - 1, -1)], axis=1)

    starts = range(0, PP, group)
    zs = [logits(pages(kp, p0)) for p0 in starts]
    rests = [suffix_sum(_softplus2(z)) for z in zs]
    r = r_ref[:, 0:1]
    for z, rest_, p0 in zip(zs, rests, starts):
        r = attend(z, rest_, r, pages(vp, p0), None)
    r_ref[...] = jnp.broadcast_to(r, r_ref.shape)

    @pl.when(s == pl.num_programs(1) - 1)
    def _():
        for h in range(SB_HEADS):
            o_ref[:, h * SB_DH:(h + 1) * SB_DH] = acc_ref[h * NQ:(h + 1) * NQ, :]


def _sb_sample(page_table, q, bias_col, k_new_pad, v_new_pad, cache_k, cache_v, layer, *, pages_per_step):
    bs, n_pages = page_table.shape
    ps = cache_k.shape[-1]
    nq = q.shape[1]
    pp = pages_per_step
    steps = n_pages // pp

    def page_spec(r):
        def idx(b, s, pt):
            return (layer, pt[b * n_pages + (n_pages - 1 - (s * pp + r))], 0, 0, 0)
        return pl.BlockSpec((None, None, SB_HEADS, SB_DH, ps), idx)

    kern = functools.partial(_sb_sample_kernel, PP=pp, PS=ps, NQ=nq)
    grid_spec = pltpu.PrefetchScalarGridSpec(
        num_scalar_prefetch=1,
        grid=(bs, steps),
        in_specs=[
            pl.BlockSpec((None, nq, 512), lambda b, s, pt: (b, 0, 0)),
            pl.BlockSpec((None, SB_HEADS * nq, 1), lambda b, s, pt: (layer, 0, 0)),
            pl.BlockSpec((None, 512, ps), lambda b, s, pt: (b, 0, 0)),
            pl.BlockSpec((None, 512, ps), lambda b, s, pt: (b, 0, 0)),
        ] + [page_spec(r) for r in range(pp)] + [page_spec(r) for r in range(pp)],
        out_specs=pl.BlockSpec((None, nq, 512), lambda b, s, pt: (b, 0, 0)),
        scratch_shapes=[pltpu.VMEM((SB_HEADS * nq, SB_DH), F32), pltpu.VMEM((SB_HEADS * nq, LANES), F32)],
    )
    return pl.pallas_call(
        kern,
        grid_spec=grid_spec,
        out_shape=jax.ShapeDtypeStruct((bs, nq, 512), F32),
        compiler_params=_params(("arbitrary", "arbitrary")),
        name="sb_sample",
    )(page_table.reshape(-1), q, bias_col, k_new_pad, v_new_pad, *([cache_k] * pp), *([cache_v] * pp))


def _merge_kernel(hm_ref, hg_ref, hs_ref, g0_ref, g1_ref, g2_ref, x_ref, gate_ref, wb_ref, wo_ref, o_ref):
    merged = jax.nn.sigmoid(g0_ref[...].astype(F32)) * _mm(hm_ref[...].astype(BF16), wb_ref[0])
    merged += jax.nn.sigmoid(g1_ref[...].astype(F32)) * _mm(hg_ref[...].astype(BF16), wb_ref[1])
    merged += jax.nn.sigmoid(g2_ref[...].astype(F32)) * _mm(hs_ref[...].astype(BF16), wb_ref[2])
    mix = _mm(merged.astype(BF16), wo_ref[...])
    o_ref[...] = x_ref[...] + gate_ref[...] * mix


def _merge(hm, hg, hs, main, x, mods, w_branch, w_out, layer, *, tm, rows_per_group):
    t, d = x.shape
    tiles_per_group = rows_per_group // tm
    tok = lambda blk: (lambda i: (i, blk))
    return pl.pallas_call(
        _merge_kernel,
        grid=(t // tm,),
        in_specs=[
            pl.BlockSpec((tm, 512), tok(0)),
            pl.BlockSpec((tm, 512), tok(0)),
            pl.BlockSpec((tm, 512), tok(0)),
            pl.BlockSpec((tm, d), tok(0)),
            pl.BlockSpec((tm, d), tok(1)),
            pl.BlockSpec((tm, d), tok(2)),
            pl.BlockSpec((tm, d), tok(0)),
            _mod_spec(mods, 2, layer, lambda i: i // tiles_per_group),
            pl.BlockSpec((None, N_BRANCH, BRANCH_W, d), lambda i: (layer, 0, 0, 0)),
            pl.BlockSpec((None, d, d), lambda i: (layer, 0, 0)),
        ],
        out_specs=pl.BlockSpec((tm, d), tok(0)),
        out_shape=jax.ShapeDtypeStruct((t, d), F32),
        compiler_params=_params(("arbitrary",)),
        name="branch_merge",
    )(hm, hg, hs, main, main, main, x, mods[0], w_branch, w_out)


FF_CHUNK = 256


def _ffn_kernel(x_ref, nw_ref, sc_ref, sh_ref, gate_ref, wu_ref, cw_ref, cb_ref, wd_ref, p1_ref, p2_ref, fw_ref,
                o_ref, u_ref, carry_ref, act_ref, *, TM, SEQ_IN_TILE, FINAL_NORM):
    x = x_ref[...]
    y = x * lax.rsqrt(jnp.mean(x * x, axis=-1, keepdims=True) + EPS)
    hb = (y * nw_ref[...] * (1.0 + sc_ref[...]) + sh_ref[...]).astype(BF16)
    row = lax.broadcasted_iota(jnp.int32, (TM, FF_CHUNK), 0)
    if SEQ_IN_TILE is None:
        @pl.when(pl.program_id(1) == 0)
        def _():
            carry_ref[...] = p1_ref[...]
        pos = row
    else:
        pos = row % SEQ_IN_TILE
    for c in range(D_FF // FF_CHUNK):
        cs = slice(c * FF_CHUNK, (c + 1) * FF_CHUNK)
        vs = slice(D_FF + c * FF_CHUNK, D_FF + (c + 1) * FF_CHUNK)
        u = _mm(hb, wu_ref[:, cs])
        val = _mm(hb, wu_ref[:, vs])
        if SEQ_IN_TILE is None:
            prev = carry_ref[:, cs]
            prev1 = prev[SUBLANES - 1:SUBLANES, :]
            prev2 = jnp.where(row == 0, prev[SUBLANES - 2:SUBLANES - 1, :], prev1)
            carry_ref[:, cs] = u[TM - SUBLANES:TM, :]
            u_ref[:, cs] = u[TM - SUBLANES:TM, :]
        else:
            prev1 = p1_ref[:, cs]
            prev2 = p2_ref[:, cs]
            u_ref[:, cs] = u
        u_m1 = jnp.where(pos >= 1, pltpu.roll(u, 1, axis=0), prev1)
        u_m2 = jnp.where(pos >= 2, pltpu.roll(u, 2, axis=0), prev2)
        cw = cw_ref[:, cs]
        conv = cb_ref[:, cs] + cw[0:1, :] * u_m2 + cw[1:2, :] * u_m1 + cw[2:3, :] * u
        act = 0.5 * conv * (1.0 + lax.erf(conv * (2.0 ** -0.5))) * val
        act_ref[:, cs] = act.astype(BF16)
    out = x + gate_ref[...] * _mm(act_ref[...], wd_ref[...])
    if FINAL_NORM:
        out = out * lax.rsqrt(jnp.mean(out * out, axis=-1, keepdims=True) + EPS) * fw_ref[...]
    o_ref[...] = out


def _ffn(x, nw, mods, w_up, conv_w, conv_b, w_down, p1, p2, final_w, layer, *, batch, seq, tm, seq_in_tile,
         final_norm):
    t, d = x.shape
    per_tile = seq_in_tile is not None
    if per_tile:
        grid = (1, t // tm)
        rowmap = lambda b, i: (i, 0)
        grp = lambda b, i: i
        pspec = pl.BlockSpec((None, tm, D_FF), lambda b, i: (layer, i, 0))
        uspec = pl.BlockSpec((tm, D_FF), rowmap)
        ushape = jax.ShapeDtypeStruct((t, D_FF), F32)
    else:
        nt = seq // tm
        grid = (batch, nt)
        rowmap = lambda b, i: (b * nt + i, 0)
        grp = lambda b, i: b
        pspec = pl.BlockSpec((None, None, SUBLANES, D_FF), lambda b, i: (layer, b, 0, 0))
        uspec = pl.BlockSpec((None, SUBLANES, D_FF), lambda b, i: (b, 0, 0))
        ushape = jax.ShapeDtypeStruct((batch, SUBLANES, D_FF), F32)
    const3 = lambda b, i: (layer, 0, 0)
    once = pl.Buffered(1)
    kern = functools.partial(_ffn_kernel, TM=tm, SEQ_IN_TILE=seq_in_tile, FINAL_NORM=final_norm)
    return pl.pallas_call(
        kern,
        grid=grid,
        in_specs=[
            pl.BlockSpec((tm, d), rowmap),
            pl.BlockSpec((None, 1, d), const3),
            _mod_spec(mods, 4, layer, grp),
            _mod_spec(mods, 3, layer, grp),
            _mod_spec(mods, 5, layer, grp),
            pl.BlockSpec((None, d, 2 * D_FF), const3, pipeline_mode=once),
            pl.BlockSpec((None, CONV_W, D_FF), const3),
            pl.BlockSpec((None, 1, D_FF), const3),
            pl.BlockSpec((None, D_FF, d), const3, pipeline_mode=once),
            pspec,
            pspec,
            pl.BlockSpec((1, d), lambda b, i: (0, 0)),
        ],
        out_specs=[pl.BlockSpec((tm, d), rowmap), uspec],
        out_shape=[jax.ShapeDtypeStruct((t, d), F32), ushape],
        scratch_shapes=[pltpu.VMEM((SUBLANES, D_FF), F32), pltpu.VMEM((tm, D_FF), BF16)],
        compiler_params=_params(("arbitrary", "arbitrary")),
        name="conv_ffn",
    )(x, nw, mods[0], mods[0], mods[0], w_up, conv_w, conv_b, w_down, p1, p2, final_w)


def _prep_weights(w_in, gla_w_gate, w_branch, w_out, w_up, w_down):
    widths = (256, 256, 512, 4, 4, 512, 256, 256, 512, GLA_RANK, 512, 512, 512, 512, N_BRANCH * D_MODEL)
    offs = np.concatenate([[0], np.cumsum(widths)])
    names = ("ml_q", "ml_k", "ml_v", "ml_i", "ml_f", "ml_o", "gla_q", "gla_k", "gla_v", "gla_lr", "gla_o",
             "sb_q", "sb_k", "sb_v", "gates")
    w_t = jnp.swapaxes(w_in, 1, 2)
    part = {nm: w_t[:, int(offs[i]):int(offs[i + 1]), :] for i, nm in enumerate(names)}
    order = ("gates", "ml_v", "ml_o", "gla_v", "gla_o", "sb_q", "ml_q", "ml_k", "gla_q", "gla_k")
    w_main = jnp.concatenate([part[nm] for nm in order], axis=1).astype(BF16)
    w_kt = part["sb_k"].astype(BF16)
    w_vt = part["sb_v"].astype(BF16)
    small = jnp.concatenate([part["ml_i"], part["ml_f"], part["gla_lr"]], axis=1)
    w_small = jnp.pad(small, ((0, 0), (0, SMALL_W - small.shape[1]), (0, 0))).astype(BF16)
    wg_pad = jnp.pad(gla_w_gate, ((0, 0), (2 * ML_HEADS, SMALL_W - 2 * ML_HEADS - GLA_RANK), (0, 0))).astype(BF16)
    return ((w_main, w_small, w_kt, w_vt), wg_pad, w_branch.astype(BF16), w_out.astype(BF16),
            w_up.astype(BF16), w_down.astype(BF16))


def _run_group(x, mod, states, paged, weights, params, *, tm, scan_len, sb_block):
    (w_proj, wg_pad, w_branch, w_out, w_up, w_down) = weights
    (norm1_w, norm2_w, ml_b_i, ml_b_f, ml_norm_w, gla_b_gate, gla_norm_w, sb_bias, conv_w, conv_b, final_w) = params
    ml_c0, ml_n0, ml_m0, gla_s0, conv0 = states
    batch, seq, d = x.shape
    depth = w_proj[0].shape[0]
    t = batch * seq
    per_seq = seq < tm
    xt = x.reshape(t, d)
    scan_dtype = F32 if per_seq else BF16
    kv_groups = 1 if per_seq else batch
    kbuf = jnp.zeros((depth, kv_groups, SB_HEADS * SB_DH, t // kv_groups), F32)
    vbuf = jnp.zeros((depth, kv_groups, SB_HEADS * SB_DH, t // kv_groups), F32)
    mcs, mns, mms, gss, bufs = [], [], [], [], []
    rows_per_group = tm if per_seq else seq
    tm_proj = tm if per_seq else math.gcd(seq, 2 * tm)
    if per_seq:
        mods = (jnp.repeat(mod, seq, axis=1).transpose(2, 0, 1, 3).reshape(6 * depth * (t // tm), tm, d), depth)
    else:
        mods = (mod.transpose(2, 0, 1, 3).reshape(6 * depth * batch, 1, d), depth)
    gate_bias = jnp.concatenate([ml_b_i, ml_b_f], axis=1)
    b_row = jnp.pad(gate_bias, ((0, 0), (0, SMALL_W - 2 * ML_HEADS))).reshape(depth, 1, SMALL_W)
    b_col = jnp.pad(gate_bias, ((0, 0), (0, SMALL_T_ROWS - 2 * ML_HEADS))).reshape(depth, SMALL_T_ROWS, 1)
    ml_m0 = ml_m0.reshape(depth, batch, ML_HEADS, 1)
    gla_s0t = jnp.swapaxes(gla_s0, -1, -2)
    if paged is None:
        sb_bias_all = jnp.repeat(sb_bias, SB_DH, axis=1).reshape(depth, 1, SB_HEADS * SB_DH)
        conv_p1 = conv_p2 = jnp.pad(conv0, ((0, 0), (0, 0), (SUBLANES - (CONV_W - 1), 0), (0, 0)))
    else:
        sb_bias_all = jnp.repeat(sb_bias, seq, axis=1).reshape(depth, SB_HEADS * seq, 1)
        zero = jnp.zeros((depth, batch, seq - 1, D_FF), F32)
        conv_p1 = jnp.concatenate([conv0[:, :, 1:2], zero], axis=2).reshape(depth, t, D_FF)
        conv_p2 = jnp.concatenate([conv0, zero[:, :, 1:]], axis=2).reshape(depth, t, D_FF)
    for l in range(depth):
        last = l == depth - 1
        main, small, small_t, kbuf, vbuf = _norm_proj(xt, norm1_w, mods, w_proj, kbuf, vbuf, l,
                                                      tm=tm_proj, rows_per_group=rows_per_group,
                                                      main_dtype=scan_dtype)
        if per_seq:
            small_t3 = small_t.reshape(SMALL_T_ROWS, batch, seq).transpose(1, 0, 2)
        else:
            small_t3 = small_t.reshape(1, SMALL_T_ROWS, t)
        ml = _mlstm(main, small, small_t3, b_row, b_col, ml_norm_w, ml_c0, ml_n0, ml_m0, l,
                    batch=batch, seq=seq, L=scan_len, out_dtype=scan_dtype, per_seq=per_seq)
        gla = _gla(main, small, wg_pad, gla_b_gate, gla_norm_w, gla_s0t, l,
                   batch=batch, seq=seq, L=scan_len, out_dtype=scan_dtype)
        hm, mc, mn, mm, hg, gst = _scans(ml, gla, batch=batch, seq=seq, L=scan_len)
        if paged is None:
            hs = _sb_prompt(main, kbuf, vbuf, sb_bias_all, l, batch=batch, seq=seq, tq=sb_block[0], tk=sb_block[1])
        else:
            cache_kt, cache_vt, page_table = paged
            ps = cache_kt.shape[-1]
            q = main[:, BLK_SB_Q * 512:(BLK_SB_Q + 1) * 512].reshape(batch, seq, 512)
            new_t = lambda buf: jnp.pad(buf[l, 0].reshape(-1, batch, seq).transpose(1, 0, 2),
                                        ((0, 0), (0, 0), (0, ps - seq)))
            hs = _sb_sample(page_table, q, sb_bias_all, new_t(kbuf), new_t(vbuf), cache_kt, cache_vt, l,
                            pages_per_step=math.gcd(page_table.shape[1], 16)).reshape(t, 512)
        xt = _merge(hm, hg, hs, main, xt, mods, w_branch, w_out, l, tm=tm, rows_per_group=rows_per_group)
        if per_seq:
            xt, u = _ffn(xt, norm2_w, mods, w_up, conv_w, conv_b, w_down, conv_p1, conv_p2, final_w, l,
                         batch=batch, seq=seq, tm=tm, seq_in_tile=seq, final_norm=last)
            full = jnp.concatenate([conv0[l], u.reshape(batch, seq, D_FF)], axis=1)
            buf = full[:, seq:]
        else:
            xt, tail = _ffn(xt, norm2_w, mods, w_up, conv_w, conv_b, w_down, conv_p1, conv_p2, final_w, l,
                            batch=batch, seq=seq, tm=tm, seq_in_tile=None, final_norm=last)
            buf = tail[:, SUBLANES - (CONV_W - 1):]
        mcs.append(mc)
        mns.append(mn)
        mms.append(mm.reshape(batch, ML_HEADS))
        gss.append(jnp.swapaxes(gst, -1, -2))
        bufs.append(buf)
    st = jnp.stack

    def kv_out(buf):
        b6 = buf.reshape(depth, kv_groups, SB_HEADS, SB_DH, batch // kv_groups, seq)
        return b6.transpose(0, 1, 4, 5, 2, 3).reshape(depth, batch, seq, SB_HEADS, SB_DH)

    return xt, kv_out(kbuf), kv_out(vbuf), st(mcs), st(mns), st(mms), st(gss), st(bufs)


def kernel(x_prompt, x_sample, cache_k, cache_v, state_mlstm_c, state_mlstm_n, state_mlstm_m, state_gla, state_conv, page_table, c_prompt, c_sample, w_ada, b_ada, norm1_w, norm2_w, w_in, ml_b_i, ml_b_f, ml_norm_w, gla_w_gate, gla_b_gate, gla_norm_w, sb_bias, w_branch, w_out, w_up, conv_w, conv_b, w_down, final_norm_w):
    depth = w_in.shape[0]
    bp, sp, d = x_prompt.shape
    bs, ss, _ = x_sample.shape
    weights = _prep_weights(w_in, gla_w_gate, w_branch, w_out, w_up, w_down)
    params = (norm1_w.reshape(depth, 1, d), norm2_w.reshape(depth, 1, d), ml_b_i, ml_b_f,
              ml_norm_w.reshape(depth, 1, -1),
              gla_b_gate.reshape(depth, 1, -1), gla_norm_w.reshape(depth, 1, -1), sb_bias, conv_w,
              conv_b.reshape(depth, 1, -1), final_norm_w.reshape(1, d))
    mod = _modulation(jnp.concatenate([c_prompt, c_sample], axis=0), w_ada, b_ada)
    mod_p = mod[:, :bp].reshape(depth, bp, 6, d)
    mod_s = mod[:, bp:].reshape(depth, bs, 6, d)

    zeros = lambda *shape: jnp.zeros((depth, bp) + shape, F32)
    states_p = (zeros(ML_HEADS, ML_DK, ML_DV), zeros(ML_HEADS, ML_DK),
                jnp.full((depth, bp, ML_HEADS), NEG_BIG, F32), zeros(GLA_HEADS, GLA_DK, GLA_DV),
                zeros(CONV_W - 1, D_FF))
    tm_p = math.gcd(sp, 512)
    scan_p = math.gcd(sp, 512)
    out_p = _run_group(x_prompt, mod_p, states_p, None, weights, params, tm=tm_p, scan_len=scan_p,
                       sb_block=(math.gcd(sp, 512), math.gcd(sp, 256)))

    states_s = (state_mlstm_c, state_mlstm_n, state_mlstm_m, state_gla, state_conv)
    cache_kt = jnp.transpose(cache_k, (0, 1, 3, 4, 2))
    cache_vt = jnp.transpose(cache_v, (0, 1, 3, 4, 2))
    out_s = _run_group(x_sample, mod_s, states_s, (cache_kt, cache_vt, page_table), weights, params,
                       tm=bs * ss, scan_len=ss, sb_block=None)

    y_p = out_p[0].reshape(bp, sp, d)
    y_s = out_s[0].reshape(bs, ss, d)
    (_, k_p, v_p, mc_p, mn_p, mm_p, g_p, cb_p) = out_p
    (_, k_s, v_s, mc_s, mn_s, mm_s, g_s, cb_s) = out_s
    return (y_p, y_s, k_p, v_p, k_s, v_s, mc_p, mn_p, mm_p, mc_s, mn_s, mm_s, g_p, g_s, cb_p, cb_s)
```

```python
import functools
import math

import numpy as np
import jax
import jax.numpy as jnp
from jax import lax
from jax.experimental import pallas as pl
from jax.experimental.pallas import tpu as pltpu

F32 = jnp.float32
BF16 = jnp.bfloat16

D_MODEL = 1024
ML_HEADS, ML_DK, ML_DV = 4, 64, 128
GLA_HEADS, GLA_DK, GLA_DV = 4, 64, 128
GLA_RANK = 16
GLA_TAU = 16.0
SB_HEADS, SB_DH = 8, 64
BRANCH_W = 512
N_BRANCH = 3
D_FF = 2816
CONV_W = 3
EPS = 1e-6
NEG_BIG = -1e30
SCAN_CHUNK = 64

LANES = 128
SUBLANES = 8
VMEM_LIMIT = 56 * 1024 * 1024

N_MAIN = 6656
BLK_ML_V, BLK_ML_O, BLK_GLA_V, BLK_GLA_O, BLK_SB_Q = 6, 7, 8, 9, 10
BLK_ML_Q, BLK_ML_K, BLK_GLA_Q, BLK_GLA_K = 22, 23, 24, 25
MAIN_TN = 3328
LOG2E = 1.4426950408889634
SMALL_W = 128
SMALL_T_ROWS = 32


def _nt(a, b, precision=None):
    return lax.dot_general(a, b, (((1,), (1,)), ((), ())), preferred_element_type=F32, precision=precision)


def _tn(a, b):
    return lax.dot_general(a, b, (((0,), (0,)), ((), ())), preferred_element_type=F32)


def _mm(a, b, precision=None):
    return jnp.dot(a, b, preferred_element_type=F32, precision=precision)


def _cumsum_mm(tri, x, *, tri_first):
    t = tri.astype(BF16)
    hi = x.astype(BF16)
    r1 = x - hi.astype(F32)
    mid = r1.astype(BF16)
    lo = (r1 - mid.astype(F32)).astype(BF16)
    parts = (hi, mid, lo)
    terms = [_mm(t, p) if tri_first else _mm(p, t) for p in parts]
    return terms[0] + (terms[1] + terms[2])


def _log_sigmoid(x):
    return jnp.minimum(x, 0.0) - jnp.log1p(jnp.exp(-jnp.abs(x)))


def _softplus2(x):
    neg_abs = lax.bitcast_convert_type(lax.bitcast_convert_type(x, jnp.uint32) | jnp.uint32(0x80000000), F32)
    return jnp.maximum(x, 0.0) + jnp.log2(1.0 + jnp.exp2(neg_abs))


def _params(sem):
    return pltpu.CompilerParams(dimension_semantics=sem, vmem_limit_bytes=VMEM_LIMIT)


def _mod_kernel(c_ref, w_ref, b_ref, o_ref):
    c = c_ref[...]
    s = c * jax.nn.sigmoid(c)
    o_ref[...] = _mm(s.astype(BF16), w_ref[...].astype(BF16)) + b_ref[...]


def _modulation(c_all, w_ada, b_ada):
    depth, d, n = w_ada.shape
    nb = c_all.shape[0]
    tn = 1536
    return pl.pallas_call(
        _mod_kernel,
        grid=(depth, n // tn),
        in_specs=[
            pl.BlockSpec((nb, d), lambda l, j: (0, 0)),
            pl.BlockSpec((None, d, tn), lambda l, j: (l, 0, j)),
            pl.BlockSpec((None, 1, tn), lambda l, j: (l, 0, j)),
        ],
        out_specs=pl.BlockSpec((None, nb, tn), lambda l, j: (l, 0, j)),
        out_shape=jax.ShapeDtypeStruct((depth, nb, n), F32),
        compiler_params=_params(("arbitrary", "arbitrary")),
        name="adaln_mod",
    )(c_all, w_ada, b_ada.reshape(depth, 1, n))


def _mod_spec(mods, k, layer, group_of):
    arr, depth = mods
    groups = arr.shape[0] // (6 * depth)
    base = (k * depth + layer) * groups
    return pl.BlockSpec((None,) + arr.shape[1:], lambda *g: (base + group_of(*g), 0, 0))


def _normmm_kernel(x_ref, nw_ref, sc_ref, sh_ref, w_ref, ws_ref, wk_ref, wv_ref, kbuf_ref, vbuf_ref,
                   o_ref, os_ref, ot_ref, kt_ref, vt_ref, h_scr):
    del kbuf_ref, vbuf_ref

    @pl.when(pl.program_id(1) == 0)
    def _():
        x = x_ref[...]
        y = x * lax.rsqrt(jnp.mean(x * x, axis=-1, keepdims=True) + EPS)
        h = y * nw_ref[...] * (1.0 + sc_ref[...]) + sh_ref[...]
        hb = h.astype(BF16)
        h_scr[...] = hb
        os_ref[...] = _nt(hb, ws_ref[...])
        ot_ref[...] = _nt(ws_ref[0:SMALL_T_ROWS, :], hb)
        kt_ref[...] = _nt(wk_ref[...], hb)
        vt_ref[...] = _nt(wv_ref[...], hb)

    o_ref[...] = _nt(h_scr[...], w_ref[...]).astype(o_ref.dtype)


def _norm_proj(x, nw, mods, weights, kbuf, vbuf, layer, *, tm, rows_per_group, main_dtype):
    w_main, w_small, w_kt, w_vt = weights
    t, d = x.shape
    n = w_main.shape[1]
    tn = MAIN_TN
    tiles_per_group = rows_per_group // tm
    grp = lambda i, j: i // tiles_per_group
    kv_rows = w_kt.shape[1]
    kv_map = lambda i, j: (layer, i // tiles_per_group, 0, i % tiles_per_group)
    wmap = lambda i, j: (layer, 0, 0)
    return pl.pallas_call(
        _normmm_kernel,
        grid=(t // tm, n // tn),
        in_specs=[
            pl.BlockSpec((tm, d), lambda i, j: (i, 0)),
            pl.BlockSpec((None, 1, d), wmap),
            _mod_spec(mods, 1, layer, grp),
            _mod_spec(mods, 0, layer, grp),
            pl.BlockSpec((None, tn, d), lambda i, j: (layer, j, 0)),
            pl.BlockSpec((None, SMALL_W, d), wmap),
            pl.BlockSpec((None, kv_rows, d), wmap),
            pl.BlockSpec((None, kv_rows, d), wmap),
            pl.BlockSpec(memory_space=pl.ANY),
            pl.BlockSpec(memory_space=pl.ANY),
        ],
        out_specs=[
            pl.BlockSpec((tm, tn), lambda i, j: (i, j)),
            pl.BlockSpec((tm, SMALL_W), lambda i, j: (i, 0)),
            pl.BlockSpec((SMALL_T_ROWS, tm), lambda i, j: (0, i)),
            pl.BlockSpec((None, None, kv_rows, tm), kv_map),
            pl.BlockSpec((None, None, kv_rows, tm), kv_map),
        ],
        out_shape=[
            jax.ShapeDtypeStruct((t, n), main_dtype),
            jax.ShapeDtypeStruct((t, SMALL_W), F32),
            jax.ShapeDtypeStruct((SMALL_T_ROWS, t), F32),
            jax.ShapeDtypeStruct(kbuf.shape, F32),
            jax.ShapeDtypeStruct(vbuf.shape, F32),
        ],
        input_output_aliases={8: 3, 9: 4},
        scratch_shapes=[pltpu.VMEM((tm, d), BF16)],
        compiler_params=_params(("arbitrary", "arbitrary")),
        name="norm_in_proj",
    )(x, nw, mods[0], mods[0], w_main, w_small, w_kt, w_vt, kbuf, vbuf)


def _mlstm_kernel(q_ref, k_ref, v_ref, o_ref, sc_ref, st_ref, brow_ref, bcol_ref, nw_ref, c0_ref, n0_ref, m0_ref,
                  hn_ref, c_ref, n_ref, m_ref, *, L):
    @pl.when(pl.program_id(1) == 0)
    def _():
        c_ref[...] = c0_ref[...]
        n_ref[...] = n0_ref[...]
        m_ref[...] = m0_ref[...]

    row = lax.broadcasted_iota(jnp.int32, (L, L), 0)
    col = lax.broadcasted_iota(jnp.int32, (L, L), 1)
    causal = col <= row
    small = sc_ref[...] + brow_ref[...]
    small_t = st_ref[...] + bcol_ref[...]
    bc_all = _cumsum_mm(causal, _log_sigmoid(small), tri_first=True)
    br_all = _cumsum_mm(row <= col, _log_sigmoid(small_t), tri_first=False)
    q = q_ref[...].astype(F32) * (ML_DK ** -0.5)
    k = k_ref[...].astype(F32)
    v = v_ref[...].astype(F32)
    og = o_ref[...].astype(F32)
    nw = nw_ref[...]
    heads = range(ML_HEADS)
    ks = [slice(h * ML_DK, (h + 1) * ML_DK) for h in heads]
    vs = [slice(h * ML_DV, (h + 1) * ML_DV) for h in heads]
    qb = [q[:, ks[h]].astype(BF16) for h in heads]
    vb = [v[:, vs[h]].astype(BF16) for h in heads]
    qk = [_nt(qb[h], k[:, ks[h]].astype(BF16)) for h in heads]
    b_col = [bc_all[:, ML_HEADS + h:ML_HEADS + h + 1] for h in heads]
    b_row = [br_all[ML_HEADS + h:ML_HEADS + h + 1, :] for h in heads]
    m_prev = [m_ref[h:h + 1, :] for h in heads]
    c_st = [c_ref[h] for h in heads]
    n_st = [n_ref[h:h + 1, :] for h in heads]
    dmat = [jnp.where(causal, b_col[h] - b_row[h] + small_t[h:h + 1, :], -jnp.inf) for h in heads]
    inter = [b_col[h] + m_prev[h] for h in heads]
    m_t = [jnp.maximum(inter[h], jnp.max(dmat[h], axis=-1, keepdims=True)) for h in heads]
    s = [qk[h] * jnp.exp(dmat[h] - m_t[h]) for h in heads]
    e_inter = [jnp.exp(inter[h] - m_t[h]) for h in heads]
    qc = [_mm(qb[h], c_st[h].astype(BF16)) for h in heads]
    num = [_mm(s[h].astype(BF16), vb[h]) + e_inter[h] * qc[h] for h in heads]
    den = [jnp.sum(s[h], axis=-1, keepdims=True)
           + e_inter[h] * jnp.sum(q[:, ks[h]] * n_st[h], axis=-1, keepdims=True) for h in heads]
    hh = [num[h] / jnp.maximum(jnp.abs(den[h]), jnp.exp(-m_t[h])) for h in heads]
    for h in heads:
        m_new = m_t[h][L - 1:L, :]
        b_last = b_col[h][L - 1:L, :]
        wk = jnp.exp(b_last - b_col[h] + small[:, h:h + 1] - m_new)
        decay = jnp.exp(b_last + m_prev[h] - m_new)
        kw = k[:, ks[h]] * wk
        c_ref[h] = decay * c_st[h] + _tn(kw.astype(BF16), vb[h])
        n_ref[h:h + 1, :] = decay * n_st[h] + jnp.sum(kw, axis=0, keepdims=True)
        m_ref[h:h + 1, :] = m_new
    for h in heads:
        y = hh[h] * lax.rsqrt(jnp.mean(hh[h] * hh[h], axis=-1, keepdims=True) + EPS) * nw[:, vs[h]]
        hn_ref[:, vs[h]] = (y * jax.nn.sigmoid(og[:, vs[h]])).astype(hn_ref.dtype)


def _small_t_spec(L, nc, per_seq):
    if per_seq:
        return pl.BlockSpec((None, SMALL_T_ROWS, L), lambda b, c: (b, 0, 0))
    return pl.BlockSpec((None, SMALL_T_ROWS, L), lambda b, c: (0, 0, b * nc + c))


def _mlstm(main, small, small_t3, b_row, b_col, nw, c0, n0, m0, layer, *, batch, seq, L, out_dtype, per_seq):
    nc = seq // L
    t = batch * seq
    tok = lambda blk: (lambda b, c: (b * nc + c, blk))
    st = lambda b, c: (b, 0, 0)
    return dict(
        in_specs=[
            pl.BlockSpec((L, 256), tok(BLK_ML_Q)),
            pl.BlockSpec((L, 256), tok(BLK_ML_K)),
            pl.BlockSpec((L, 512), tok(BLK_ML_V)),
            pl.BlockSpec((L, 512), tok(BLK_ML_O)),
            pl.BlockSpec((L, SMALL_W), tok(0)),
            _small_t_spec(L, nc, per_seq),
            pl.BlockSpec((None, 1, SMALL_W), lambda b, c: (layer, 0, 0)),
            pl.BlockSpec((None, SMALL_T_ROWS, 1), lambda b, c: (layer, 0, 0)),
            pl.BlockSpec((None, 1, 512), lambda b, c: (layer, 0, 0)),
            pl.BlockSpec((None, None, ML_HEADS, ML_DK, ML_DV), lambda b, c: (layer, b, 0, 0, 0)),
            pl.BlockSpec((None, None, ML_HEADS, ML_DK), lambda b, c: (layer, b, 0, 0)),
            pl.BlockSpec((None, None, ML_HEADS, 1), lambda b, c: (layer, b, 0, 0)),
        ],
        out_specs=[
            pl.BlockSpec((L, 512), tok(0)),
            pl.BlockSpec((None, ML_HEADS, ML_DK, ML_DV), lambda b, c: (b, 0, 0, 0)),
            pl.BlockSpec((None, ML_HEADS, ML_DK), st),
            pl.BlockSpec((None, ML_HEADS, 1), st),
        ],
        out_shape=[
            jax.ShapeDtypeStruct((t, 512), out_dtype),
            jax.ShapeDtypeStruct((batch, ML_HEADS, ML_DK, ML_DV), F32),
            jax.ShapeDtypeStruct((batch, ML_HEADS, ML_DK), F32),
            jax.ShapeDtypeStruct((batch, ML_HEADS, 1), F32),
        ],
        operands=(main, main, main, main, small, small_t3, b_row, b_col, nw, c0, n0, m0),
    )


def _gla_kernel(q_ref, k_ref, v_ref, o_ref, sc_ref, wg_ref, bg_ref, nw_ref, s0_ref, hn_ref, s_ref, *, L, LS):
    @pl.when(pl.program_id(1) == 0)
    def _():
        s_ref[...] = s0_ref[...]

    row = lax.broadcasted_iota(jnp.int32, (LS, LS), 0)
    col = lax.broadcasted_iota(jnp.int32, (LS, LS), 1)
    causal = col <= row
    la_all = _log_sigmoid(_mm(sc_ref[...].astype(BF16), wg_ref[...]) + bg_ref[...]) * (1.0 / GLA_TAU)
    nw = nw_ref[...]
    heads = range(GLA_HEADS)
    ks = [slice(h * GLA_DK, (h + 1) * GLA_DK) for h in heads]
    vs = [slice(h * GLA_DV, (h + 1) * GLA_DV) for h in heads]
    pre = []
    for c in range(L // LS):
        rs = slice(c * LS, (c + 1) * LS)
        bc = _cumsum_mm(causal, la_all[rs, :], tri_first=True)
        ref_row = bc[LS // 2:LS // 2 + 1, :]
        last = bc[LS - 1:LS, :]
        q = q_ref[rs, :].astype(F32) * (GLA_DK ** -0.5)
        k = k_ref[rs, :].astype(F32)
        qe = (q * jnp.exp(bc - ref_row)).astype(BF16)
        ke = (k * jnp.exp(ref_row - bc)).astype(BF16)
        qs = (q * jnp.exp(bc)).astype(BF16)
        kl = (k * jnp.exp(last - bc)).astype(BF16)
        el = jnp.exp(last)
        v = v_ref[rs, :].astype(F32)
        vb = [v[:, vs[h]].astype(BF16) for h in heads]
        av = [_mm(jnp.where(causal, _nt(qe[:, ks[h]], ke[:, ks[h]]), 0.0).astype(BF16), vb[h]) for h in heads]
        pre.append((rs, qs, kl, el, vb, av))
    state = [s_ref[h] for h in heads]
    outs = []
    for rs, qs, kl, el, vb, av in pre:
        o = [av[h] + _nt(qs[:, ks[h]], state[h].astype(BF16)) for h in heads]
        state = [el[:, ks[h]] * state[h] + _tn(vb[h], kl[:, ks[h]]) for h in heads]
        outs.append((rs, o))
    for h in heads:
        s_ref[h] = state[h]
    for rs, o in outs:
        og = o_ref[rs, :].astype(F32)
        for h in heads:
            y = o[h] * lax.rsqrt(jnp.mean(o[h] * o[h], axis=-1, keepdims=True) + EPS) * nw[:, vs[h]]
            g = og[:, vs[h]]
            hn_ref[rs, vs[h]] = (y * (g * jax.nn.sigmoid(g))).astype(hn_ref.dtype)


def _gla(main, small, wg_pad, bg, nw, s0t, layer, *, batch, seq, L, out_dtype):
    nc = seq // L
    t = batch * seq
    tok = lambda blk: (lambda b, c: (b * nc + c, blk))
    return dict(
        in_specs=[
            pl.BlockSpec((L, 256), tok(BLK_GLA_Q)),
            pl.BlockSpec((L, 256), tok(BLK_GLA_K)),
            pl.BlockSpec((L, 512), tok(BLK_GLA_V)),
            pl.BlockSpec((L, 512), tok(BLK_GLA_O)),
            pl.BlockSpec((L, SMALL_W), tok(0)),
            pl.BlockSpec((None, SMALL_W, 256), lambda b, c: (layer, 0, 0)),
            pl.BlockSpec((None, 1, 256), lambda b, c: (layer, 0, 0)),
            pl.BlockSpec((None, 1, 512), lambda b, c: (layer, 0, 0)),
            pl.BlockSpec((None, None, GLA_HEADS, GLA_DV, GLA_DK), lambda b, c: (layer, b, 0, 0, 0)),
        ],
        out_specs=[
            pl.BlockSpec((L, 512), tok(0)),
            pl.BlockSpec((None, GLA_HEADS, GLA_DV, GLA_DK), lambda b, c: (b, 0, 0, 0)),
        ],
        out_shape=[
            jax.ShapeDtypeStruct((t, 512), out_dtype),
            jax.ShapeDtypeStruct((batch, GLA_HEADS, GLA_DV, GLA_DK), F32),
        ],
        operands=(main, main, main, main, small, wg_pad, bg, nw, s0t),
    )


def _scan_kernel(*refs, N_ML_IN, N_GLA_IN, N_ML_OUT, L, LS):
    ml_in = refs[:N_ML_IN]
    gla_in = refs[N_ML_IN:N_ML_IN + N_GLA_IN]
    outs = refs[N_ML_IN + N_GLA_IN:]
    _mlstm_kernel(*ml_in, *outs[:N_ML_OUT], L=L)
    _gla_kernel(*gla_in, *outs[N_ML_OUT:], L=L, LS=LS)


def _scans(ml, gla, *, batch, seq, L):
    kern = functools.partial(_scan_kernel, N_ML_IN=len(ml["in_specs"]), N_GLA_IN=len(gla["in_specs"]),
                             N_ML_OUT=len(ml["out_specs"]), L=L, LS=math.gcd(L, SCAN_CHUNK))
    return pl.pallas_call(
        kern,
        grid=(batch, seq // L),
        in_specs=ml["in_specs"] + gla["in_specs"],
        out_specs=ml["out_specs"] + gla["out_specs"],
        out_shape=ml["out_shape"] + gla["out_shape"],
        compiler_params=_params(("arbitrary", "arbitrary")),
        name="mlstm_gla_scan",
    )(*ml["operands"], *gla["operands"])


SB_HEAD_GROUP = 4


def _sb_prompt_kernel(qi_ref, kj_ref, q_ref, kt_ref, vt_ref, bias_ref, o_ref, acc_ref, r_ref, qa_ref, *, TQ, TK):
    p = pl.program_id(1)
    i = qi_ref[p]
    j = kj_ref[p]
    ratio = TQ // TK
    head = [slice(h * SB_DH, (h + 1) * SB_DH) for h in range(SB_HEADS)]

    @pl.when(j == ratio * i + ratio - 1)
    def _():
        acc_ref[...] = jnp.zeros_like(acc_ref)
        r_ref[...] = jnp.zeros_like(r_ref)
        qf = q_ref[...].astype(F32) * (SB_DH ** -0.5 * LOG2E)
        lane = lax.broadcasted_iota(jnp.int32, (TQ, LANES), 1)
        ones = jnp.where(lane < SB_DH + 2, 1.0, 0.0)
        for h in range(SB_HEADS):
            pair = qf[:, (h // 2) * LANES:(h // 2 + 1) * LANES]
            if h % 2:
                pair = pltpu.roll(pair, SB_DH, axis=1)
            qa_ref[h] = jnp.where(lane < SB_DH, pair, ones).astype(BF16)

    row = lax.broadcasted_iota(jnp.int32, (TK, TK), 0)
    col = lax.broadcasted_iota(jnp.int32, (TK, TK), 1)
    suffix = (row >= col).astype(BF16)
    bias = bias_ref[...] * LOG2E
    bias_hi = bias.astype(BF16).astype(F32)
    bias_lo = bias - bias_hi
    krow = lax.broadcasted_iota(jnp.int32, (SB_DH, TK), 0)

    def keys(h):
        b = slice(h * SB_DH, h * SB_DH + 1)
        extra = jnp.where(krow == 0, bias_hi[:, b], jnp.where(krow == 1, bias_lo[:, b], 0.0))
        return jnp.concatenate([kt_ref[head[h], :].astype(BF16), extra.astype(BF16)], axis=0)

    def run(rows, diagonal):
        valid = col < row
        for g in range(0, SB_HEADS, SB_HEAD_GROUP):
            hh = range(g, g + SB_HEAD_GROUP)
            zs = [_mm(qa_ref[h, rows, :], keys(h)) for h in hh]
            sps = [_softplus2(z) for z in zs]
            if diagonal:
                sps = [jnp.where(valid, sp, 0.0) for sp in sps]
            rests = [_mm(sp.astype(BF16), suffix) for sp in sps]
            rs = [r_ref[rows, h:h + 1] for h in hh]
            probs = [jnp.exp2(z - rest - r) for z, rest, r in zip(zs, rests, rs)]
            if diagonal:
                probs = [jnp.where(valid, a, 0.0) for a in probs]
            for h, a, r, rest in zip(hh, probs, rs, rests):
                acc_ref[h, rows, :] += _nt(a.astype(BF16), vt_ref[head[h], :].astype(BF16))
                r_ref[rows, h:h + 1] = r + rest[:, 0:1]

    for part in range(ratio):
        rows = slice(part * TK, (part + 1) * TK)
        pl.when(j == ratio * i + part)(functools.partial(run, rows, True))
        if part > 0:
            pl.when(jnp.logical_and(j < ratio * i + part, j >= ratio * i))(functools.partial(run, rows, False))
    pl.when(j < ratio * i)(functools.partial(run, slice(0, TQ), False))

    @pl.when(j == 0)
    def _():
        for h in range(SB_HEADS):
            o_ref[:, head[h]] = acc_ref[h].astype(o_ref.dtype)


def _sb_prompt(main, kbuf, vbuf, bias_row, layer, *, batch, seq, tq, tk):
    nq = seq // tq
    ratio = tq // tk
    qi = np.concatenate([np.full(ratio * (i + 1), i) for i in range(nq)]).astype(np.int32)
    kj = np.concatenate([np.arange(ratio * (i + 1) - 1, -1, -1) for i in range(nq)]).astype(np.int32)
    t = batch * seq
    kern = functools.partial(_sb_prompt_kernel, TQ=tq, TK=tk)
    kv_spec = pl.BlockSpec((None, None, SB_HEADS * SB_DH, tk), lambda b, p, qi, kj: (layer, b, 0, kj[p]))
    grid_spec = pltpu.PrefetchScalarGridSpec(
        num_scalar_prefetch=2,
        grid=(batch, len(qi)),
        in_specs=[
            pl.BlockSpec((tq, 512), lambda b, p, qi, kj: (b * nq + qi[p], BLK_SB_Q)),
            kv_spec,
            kv_spec,
            pl.BlockSpec((None, 1, 512), lambda b, p, qi, kj: (layer, 0, 0)),
        ],
        out_specs=pl.BlockSpec((tq, 512), lambda b, p, qi, kj: (b * nq + qi[p], 0)),
        scratch_shapes=[pltpu.VMEM((SB_HEADS, tq, SB_DH), F32), pltpu.VMEM((tq, LANES), F32),
                        pltpu.VMEM((SB_HEADS, tq, LANES), BF16)],
    )
    return pl.pallas_call(
        kern,
        grid_spec=grid_spec,
        out_shape=jax.ShapeDtypeStruct((t, 512), BF16),
        compiler_params=_params(("arbitrary", "arbitrary")),
        name="sb_prompt",
    )(jnp.asarray(qi), jnp.asarray(kj), main, kbuf, vbuf, bias_row)


def _sb_sample_kernel(pt_ref, q_ref, bias_ref, kn_ref, vn_ref, *rest, PP, PS, NQ):
    kp = rest[:PP]
    vp = rest[PP:2 * PP]
    o_ref, acc_ref, r_ref = rest[2 * PP:]
    s = pl.program_id(1)
    hq = SB_HEADS * NQ
    q = (q_ref[...] * (SB_DH ** -0.5 * LOG2E)).astype(BF16)
    bias = bias_ref[...] * LOG2E
    group = 2 if PP % 2 == 0 else 1
    row = lax.broadcasted_iota(jnp.int32, (group * PS, group * PS), 0)
    col = lax.broadcasted_iota(jnp.int32, (group * PS, group * PS), 1)
    suffix = (row >= col).astype(BF16)

    def logits(get_kt):
        return jnp.concatenate(
            [_mm(q[:, h * SB_DH:(h + 1) * SB_DH], get_kt(h).astype(BF16)) for h in range(SB_HEADS)], axis=0) + bias

    def suffix_sum(sp):
        nk = sp.shape[1]
        return _mm(sp.astype(BF16), suffix[:nk, :nk])

    def attend(z, rest_, r, get_vt, valid):
        a = jnp.exp2(z - rest_ - r)
        if valid is not None:
            a = jnp.where(valid, a, 0.0)
        for h in range(SB_HEADS):
            rows = slice(h * NQ, (h + 1) * NQ)
            acc_ref[rows, :] += _nt(a[rows, :].astype(BF16), get_vt(h).astype(BF16))
        return r + rest_[:, 0:1]

    @pl.when(s == 0)
    def _():
        acc_ref[...] = jnp.zeros_like(acc_ref)
        key = lax.broadcasted_iota(jnp.int32, (hq, PS), 1)
        qpos = lax.broadcasted_iota(jnp.int32, (hq, PS), 0) % NQ
        valid = key < qpos
        z = logits(lambda h: kn_ref[h * SB_DH:(h + 1) * SB_DH, :])
        rest_ = suffix_sum(jnp.where(valid, _softplus2(z), 0.0))
        r = attend(z, rest_, 0.0, lambda h: vn_ref[h * SB_DH:(h + 1) * SB_DH, :], valid)
        r_ref[...] = jnp.broadcast_to(r, r_ref.shape)

    def pages(refs, p0):
        return lambda h: jnp.concatenate([refs[p][h] for p in range(p0 + group - 1, p0 - 1, -1)], axis=1)

    starts = range(0, PP, group)
    zs = [logits(pages(kp, p0)) for p0 in starts]
    rests = [suffix_sum(_softplus2(z)) for z in zs]
    r = r_ref[:, 0:1]
    for z, rest_, p0 in zip(zs, rests, starts):
        r = attend(z, rest_, r, pages(vp, p0), None)
    r_ref[...] = jnp.broadcast_to(r, r_ref.shape)

    @pl.when(s == pl.num_programs(1) - 1)
    def _():
        for h in range(SB_HEADS):
            o_ref[:, h * SB_DH:(h + 1) * SB_DH] = acc_ref[h * NQ:(h + 1) * NQ, :]


def _sb_sample(page_table, q, bias_col, k_new_pad, v_new_pad, cache_k, cache_v, layer, *, pages_per_step):
    bs, n_pages = page_table.shape
    ps = cache_k.shape[-1]
    nq = q.shape[1]
    pp = pages_per_step
    steps = n_pages // pp

    def page_spec(r):
        def idx(b, s, pt):
            return (layer, pt[b * n_pages + (n_pages - 1 - (s * pp + r))], 0, 0, 0)
        return pl.BlockSpec((None, None, SB_HEADS, SB_DH, ps), idx)

    kern = functools.partial(_sb_sample_kernel, PP=pp, PS=ps, NQ=nq)
    grid_spec = pltpu.PrefetchScalarGridSpec(
        num_scalar_prefetch=1,
        grid=(bs, steps),
        in_specs=[
            pl.BlockSpec((None, nq, 512), lambda b, s, pt: (b, 0, 0)),
            pl.BlockSpec((None, SB_HEADS * nq, 1), lambda b, s, pt: (layer, 0, 0)),
            pl.BlockSpec((None, 512, ps), lambda b, s, pt: (b, 0, 0)),
            pl.BlockSpec((None, 512, ps), lambda b, s, pt: (b, 0, 0)),
        ] + [page_spec(r) for r in range(pp)] + [page_spec(r) for r in range(pp)],
        out_specs=pl.BlockSpec((None, nq, 512), lambda b, s, pt: (b, 0, 0)),
        scratch_shapes=[pltpu.VMEM((SB_HEADS * nq, SB_DH), F32), pltpu.VMEM((SB_HEADS * nq, LANES), F32)],
    )
    return pl.pallas_call(
        kern,
        grid_spec=grid_spec,
        out_shape=jax.ShapeDtypeStruct((bs, nq, 512), F32),
        compiler_params=_params(("arbitrary", "arbitrary")),
        name="sb_sample",
    )(page_table.reshape(-1), q, bias_col, k_new_pad, v_new_pad, *([cache_k] * pp), *([cache_v] * pp))


def _merge_kernel(hm_ref, hg_ref, hs_ref, g0_ref, g1_ref, g2_ref, x_ref, gate_ref, wb_ref, wo_ref, o_ref):
    merged = jax.nn.sigmoid(g0_ref[...].astype(F32)) * _mm(hm_ref[...].astype(BF16), wb_ref[0])
    merged += jax.nn.sigmoid(g1_ref[...].astype(F32)) * _mm(hg_ref[...].astype(BF16), wb_ref[1])
    merged += jax.nn.sigmoid(g2_ref[...].astype(F32)) * _mm(hs_ref[...].astype(BF16), wb_ref[2])
    mix = _mm(merged.astype(BF16), wo_ref[...])
    o_ref[...] = x_ref[...] + gate_ref[...] * mix


def _merge(hm, hg, hs, main, x, mods, w_branch, w_out, layer, *, tm, rows_per_group):
    t, d = x.shape
    tiles_per_group = rows_per_group // tm
    tok = lambda blk: (lambda i: (i, blk))
    return pl.pallas_call(
        _merge_kernel,
        grid=(t // tm,),
        in_specs=[
            pl.BlockSpec((tm, 512), tok(0)),
            pl.BlockSpec((tm, 512), tok(0)),
            pl.BlockSpec((tm, 512), tok(0)),
            pl.BlockSpec((tm, d), tok(0)),
            pl.BlockSpec((tm, d), tok(1)),
            pl.BlockSpec((tm, d), tok(2)),
            pl.BlockSpec((tm, d), tok(0)),
            _mod_spec(mods, 2, layer, lambda i: i // tiles_per_group),
            pl.BlockSpec((None, N_BRANCH, BRANCH_W, d), lambda i: (layer, 0, 0, 0)),
            pl.BlockSpec((None, d, d), lambda i: (layer, 0, 0)),
        ],
        out_specs=pl.BlockSpec((tm, d), tok(0)),
        out_shape=jax.ShapeDtypeStruct((t, d), F32),
        compiler_params=_params(("arbitrary",)),
        name="branch_merge",
    )(hm, hg, hs, main, main, main, x, mods[0], w_branch, w_out)


FF_CHUNK = 256


def _ffn_kernel(x_ref, nw_ref, sc_ref, sh_ref, gate_ref, wu_ref, cw_ref, cb_ref, wd_ref, p1_ref, p2_ref, fw_ref,
                o_ref, u_ref, carry_ref, act_ref, *, TM, SEQ_IN_TILE, FINAL_NORM):
    x = x_ref[...]
    y = x * lax.rsqrt(jnp.mean(x * x, axis=-1, keepdims=True) + EPS)
    hb = (y * nw_ref[...] * (1.0 + sc_ref[...]) + sh_ref[...]).astype(BF16)
    row = lax.broadcasted_iota(jnp.int32, (TM, FF_CHUNK), 0)
    if SEQ_IN_TILE is None:
        @pl.when(pl.program_id(1) == 0)
        def _():
            carry_ref[...] = p1_ref[...]
        pos = row
    else:
        pos = row % SEQ_IN_TILE
    for c in range(D_FF // FF_CHUNK):
        cs = slice(c * FF_CHUNK, (c + 1) * FF_CHUNK)
        vs = slice(D_FF + c * FF_CHUNK, D_FF + (c + 1) * FF_CHUNK)
        u = _mm(hb, wu_ref[:, cs])
        val = _mm(hb, wu_ref[:, vs])
        if SEQ_IN_TILE is None:
            prev = carry_ref[:, cs]
            prev1 = prev[SUBLANES - 1:SUBLANES, :]
            prev2 = jnp.where(row == 0, prev[SUBLANES - 2:SUBLANES - 1, :], prev1)
            carry_ref[:, cs] = u[TM - SUBLANES:TM, :]
            u_ref[:, cs] = u[TM - SUBLANES:TM, :]
        else:
            prev1 = p1_ref[:, cs]
            prev2 = p2_ref[:, cs]
            u_ref[:, cs] = u
        u_m1 = jnp.where(pos >= 1, pltpu.roll(u, 1, axis=0), prev1)
        u_m2 = jnp.where(pos >= 2, pltpu.roll(u, 2, axis=0), prev2)
        cw = cw_ref[:, cs]
        conv = cb_ref[:, cs] + cw[0:1, :] * u_m2 + cw[1:2, :] * u_m1 + cw[2:3, :] * u
        act = 0.5 * conv * (1.0 + lax.erf(conv * (2.0 ** -0.5))) * val
        act_ref[:, cs] = act.astype(BF16)
    out = x + gate_ref[...] * _mm(act_ref[...], wd_ref[...])
    if FINAL_NORM:
        out = out * lax.rsqrt(jnp.mean(out * out, axis=-1, keepdims=True) + EPS) * fw_ref[...]
    o_ref[...] = out


def _ffn(x, nw, mods, w_up, conv_w, conv_b, w_down, p1, p2, final_w, layer, *, batch, seq, tm, seq_in_tile,
         final_norm):
    t, d = x.shape
    per_tile = seq_in_tile is not None
    if per_tile:
        grid = (1, t // tm)
        rowmap = lambda b, i: (i, 0)
        grp = lambda b, i: i
        pspec = pl.BlockSpec((None, tm, D_FF), lambda b, i: (layer, i, 0))
        uspec = pl.BlockSpec((tm, D_FF), rowmap)
        ushape = jax.ShapeDtypeStruct((t, D_FF), F32)
    else:
        nt = seq // tm
        grid = (batch, nt)
        rowmap = lambda b, i: (b * nt + i, 0)
        grp = lambda b, i: b
        pspec = pl.BlockSpec((None, None, SUBLANES, D_FF), lambda b, i: (layer, b, 0, 0))
        uspec = pl.BlockSpec((None, SUBLANES, D_FF), lambda b, i: (b, 0, 0))
        ushape = jax.ShapeDtypeStruct((batch, SUBLANES, D_FF), F32)
    const3 = lambda b, i: (layer, 0, 0)
    once = pl.Buffered(1)
    kern = functools.partial(_ffn_kernel, TM=tm, SEQ_IN_TILE=seq_in_tile, FINAL_NORM=final_norm)
    return pl.pallas_call(
        kern,
        grid=grid,
        in_specs=[
            pl.BlockSpec((tm, d), rowmap),
            pl.BlockSpec((None, 1, d), const3),
            _mod_spec(mods, 4, layer, grp),
            _mod_spec(mods, 3, layer, grp),
            _mod_spec(mods, 5, layer, grp),
            pl.BlockSpec((None, d, 2 * D_FF), const3, pipeline_mode=once),
            pl.BlockSpec((None, CONV_W, D_FF), const3),
            pl.BlockSpec((None, 1, D_FF), const3),
            pl.BlockSpec((None, D_FF, d), const3, pipeline_mode=once),
            pspec,
            pspec,
            pl.BlockSpec((1, d), lambda b, i: (0, 0)),
        ],
        out_specs=[pl.BlockSpec((tm, d), rowmap), uspec],
        out_shape=[jax.ShapeDtypeStruct((t, d), F32), ushape],
        scratch_shapes=[pltpu.VMEM((SUBLANES, D_FF), F32), pltpu.VMEM((tm, D_FF), BF16)],
        compiler_params=_params(("arbitrary", "arbitrary")),
        name="conv_ffn",
    )(x, nw, mods[0], mods[0], mods[0], w_up, conv_w, conv_b, w_down, p1, p2, final_w)


def _prep_weights(w_in, gla_w_gate, w_branch, w_out, w_up, w_down):
    widths = (256, 256, 512, 4, 4, 512, 256, 256, 512, GLA_RANK, 512, 512, 512, 512, N_BRANCH * D_MODEL)
    offs = np.concatenate([[0], np.cumsum(widths)])
    names = ("ml_q", "ml_k", "ml_v", "ml_i", "ml_f", "ml_o", "gla_q", "gla_k", "gla_v", "gla_lr", "gla_o",
             "sb_q", "sb_k", "sb_v", "gates")
    w_t = jnp.swapaxes(w_in, 1, 2)
    part = {nm: w_t[:, int(offs[i]):int(offs[i + 1]), :] for i, nm in enumerate(names)}
    order = ("gates", "ml_v", "ml_o", "gla_v", "gla_o", "sb_q", "ml_q", "ml_k", "gla_q", "gla_k")
    w_main = jnp.concatenate([part[nm] for nm in order], axis=1).astype(BF16)
    w_kt = part["sb_k"].astype(BF16)
    w_vt = part["sb_v"].astype(BF16)
    small = jnp.concatenate([part["ml_i"], part["ml_f"], part["gla_lr"]], axis=1)
    w_small = jnp.pad(small, ((0, 0), (0, SMALL_W - small.shape[1]), (0, 0))).astype(BF16)
    wg_pad = jnp.pad(gla_w_gate, ((0, 0), (2 * ML_HEADS, SMALL_W - 2 * ML_HEADS - GLA_RANK), (0, 0))).astype(BF16)
    return ((w_main, w_small, w_kt, w_vt), wg_pad, w_branch.astype(BF16), w_out.astype(BF16),
            w_up.astype(BF16), w_down.astype(BF16))


def _run_group(x, mod, states, paged, weights, params, *, tm, scan_len, sb_block):
    (w_proj, wg_pad, w_branch, w_out, w_up, w_down) = weights
    (norm1_w, norm2_w, ml_b_i, ml_b_f, ml_norm_w, gla_b_gate, gla_norm_w, sb_bias, conv_w, conv_b, final_w) = params
    ml_c0, ml_n0, ml_m0, gla_s0, conv0 = states
    batch, seq, d = x.shape
    depth = w_proj[0].shape[0]
    t = batch * seq
    per_seq = seq < tm
    xt = x.reshape(t, d)
    scan_dtype = F32 if per_seq else BF16
    kv_groups = 1 if per_seq else batch
    kbuf = jnp.zeros((depth, kv_groups, SB_HEADS * SB_DH, t // kv_groups), F32)
    vbuf = jnp.zeros((depth, kv_groups, SB_HEADS * SB_DH, t // kv_groups), F32)
    mcs, mns, mms, gss, bufs = [], [], [], [], []
    rows_per_group = tm if per_seq else seq
    tm_proj = tm if per_seq else math.gcd(seq, 2 * tm)
    if per_seq:
        mods = (jnp.repeat(mod, seq, axis=1).transpose(2, 0, 1, 3).reshape(6 * depth * (t // tm), tm, d), depth)
    else:
        mods = (mod.transpose(2, 0, 1, 3).reshape(6 * depth * batch, 1, d), depth)
    gate_bias = jnp.concatenate([ml_b_i, ml_b_f], axis=1)
    b_row = jnp.pad(gate_bias, ((0, 0), (0, SMALL_W - 2 * ML_HEADS))).reshape(depth, 1, SMALL_W)
    b_col = jnp.pad(gate_bias, ((0, 0), (0, SMALL_T_ROWS - 2 * ML_HEADS))).reshape(depth, SMALL_T_ROWS, 1)
    ml_m0 = ml_m0.reshape(depth, batch, ML_HEADS, 1)
    gla_s0t = jnp.swapaxes(gla_s0, -1, -2)
    if paged is None:
        sb_bias_all = jnp.repeat(sb_bias, SB_DH, axis=1).reshape(depth, 1, SB_HEADS * SB_DH)
        conv_p1 = conv_p2 = jnp.pad(conv0, ((0, 0), (0, 0), (SUBLANES - (CONV_W - 1), 0), (0, 0)))
    else:
        sb_bias_all = jnp.repeat(sb_bias, seq, axis=1).reshape(depth, SB_HEADS * seq, 1)
        zero = jnp.zeros((depth, batch, seq - 1, D_FF), F32)
        conv_p1 = jnp.concatenate([conv0[:, :, 1:2], zero], axis=2).reshape(depth, t, D_FF)
        conv_p2 = jnp.concatenate([conv0, zero[:, :, 1:]], axis=2).reshape(depth, t, D_FF)
    for l in range(depth):
        last = l == depth - 1
        main, small, small_t, kbuf, vbuf = _norm_proj(xt, norm1_w, mods, w_proj, kbuf, vbuf, l,
                                                      tm=tm_proj, rows_per_group=rows_per_group,
                                                      main_dtype=scan_dtype)
        if per_seq:
            small_t3 = small_t.reshape(SMALL_T_ROWS, batch, seq).transpose(1, 0, 2)
        else:
            small_t3 = small_t.reshape(1, SMALL_T_ROWS, t)
        ml = _mlstm(main, small, small_t3, b_row, b_col, ml_norm_w, ml_c0, ml_n0, ml_m0, l,
                    batch=batch, seq=seq, L=scan_len, out_dtype=scan_dtype, per_seq=per_seq)
        gla = _gla(main, small, wg_pad, gla_b_gate, gla_norm_w, gla_s0t, l,
                   batch=batch, seq=seq, L=scan_len, out_dtype=scan_dtype)
        hm, mc, mn, mm, hg, gst = _scans(ml, gla, batch=batch, seq=seq, L=scan_len)
        if paged is None:
            hs = _sb_prompt(main, kbuf, vbuf, sb_bias_all, l, batch=batch, seq=seq, tq=sb_block[0], tk=sb_block[1])
        else:
            cache_kt, cache_vt, page_table = paged
            ps = cache_kt.shape[-1]
            q = main[:, BLK_SB_Q * 512:(BLK_SB_Q + 1) * 512].reshape(batch, seq, 512)
            new_t = lambda buf: jnp.pad(buf[l, 0].reshape(-1, batch, seq).transpose(1, 0, 2),
                                        ((0, 0), (0, 0), (0, ps - seq)))
            hs = _sb_sample(page_table, q, sb_bias_all, new_t(kbuf), new_t(vbuf), cache_kt, cache_vt, l,
                            pages_per_step=math.gcd(page_table.shape[1], 16)).reshape(t, 512)
        xt = _merge(hm, hg, hs, main, xt, mods, w_branch, w_out, l, tm=tm, rows_per_group=rows_per_group)
        if per_seq:
            xt, u = _ffn(xt, norm2_w, mods, w_up, conv_w, conv_b, w_down, conv_p1, conv_p2, final_w, l,
                         batch=batch, seq=seq, tm=tm, seq_in_tile=seq, final_norm=last)
            full = jnp.concatenate([conv0[l], u.reshape(batch, seq, D_FF)], axis=1)
            buf = full[:, seq:]
        else:
            xt, tail = _ffn(xt, norm2_w, mods, w_up, conv_w, conv_b, w_down, conv_p1, conv_p2, final_w, l,
                            batch=batch, seq=seq, tm=tm, seq_in_tile=None, final_norm=last)
            buf = tail[:, SUBLANES - (CONV_W - 1):]
        mcs.append(mc)
        mns.append(mn)
        mms.append(mm.reshape(batch, ML_HEADS))
        gss.append(jnp.swapaxes(gst, -1, -2))
        bufs.append(buf)
    st = jnp.stack

    def kv_out(buf):
        b6 = buf.reshape(depth, kv_groups, SB_HEADS, SB_DH, batch // kv_groups, seq)
        return b6.transpose(0, 1, 4, 5, 2, 3).reshape(depth, batch, seq, SB_HEADS, SB_DH)

    return xt, kv_out(kbuf), kv_out(vbuf), st(mcs), st(mns), st(mms), st(gss), st(bufs)


def kernel(x_prompt, x_sample, cache_k, cache_v, state_mlstm_c, state_mlstm_n, state_mlstm_m, state_gla, state_conv, page_table, c_prompt, c_sample, w_ada, b_ada, norm1_w, norm2_w, w_in, ml_b_i, ml_b_f, ml_norm_w, gla_w_gate, gla_b_gate, gla_norm_w, sb_bias, w_branch, w_out, w_up, conv_w, conv_b, w_down, final_norm_w):
    depth = w_in.shape[0]
    bp, sp, d = x_prompt.shape
    bs, ss, _ = x_sample.shape
    weights = _prep_weights(w_in, gla_w_gate, w_branch, w_out, w_up, w_down)
    params = (norm1_w.reshape(depth, 1, d), norm2_w.reshape(depth, 1, d), ml_b_i, ml_b_f,
              ml_norm_w.reshape(depth, 1, -1),
              gla_b_gate.reshape(depth, 1, -1), gla_norm_w.reshape(depth, 1, -1), sb_bias, conv_w,
              conv_b.reshape(depth, 1, -1), final_norm_w.reshape(1, d))
    mod = _modulation(jnp.concatenate([c_prompt, c_sample], axis=0), w_ada, b_ada)
    mod_p = mod[:, :bp].reshape(depth, bp, 6, d)
    mod_s = mod[:, bp:].reshape(depth, bs, 6, d)

    zeros = lambda *shape: jnp.zeros((depth, bp) + shape, F32)
    states_p = (zeros(ML_HEADS, ML_DK, ML_DV), zeros(ML_HEADS, ML_DK),
                jnp.full((depth, bp, ML_HEADS), NEG_BIG, F32), zeros(GLA_HEADS, GLA_DK, GLA_DV),
                zeros(CONV_W - 1, D_FF))
    tm_p = math.gcd(sp, 512)
    scan_p = math.gcd(sp, 512)
    out_p = _run_group(x_prompt, mod_p, states_p, None, weights, params, tm=tm_p, scan_len=scan_p,
                       sb_block=(math.gcd(sp, 512), math.gcd(sp, 256)))

    states_s = (state_mlstm_c, state_mlstm_n, state_mlstm_m, state_gla, state_conv)
    cache_kt = jnp.transpose(cache_k, (0, 1, 3, 4, 2))
    cache_vt = jnp.transpose(cache_v, (0, 1, 3, 4, 2))
    out_s = _run_group(x_sample, mod_s, states_s, (cache_kt, cache_vt, page_table), weights, params,
                       tm=bs * ss, scan_len=ss, sb_block=None)

    y_p = out_p[0].reshape(bp, sp, d)
    y_s = out_s[0].reshape(bs, ss, d)
    (_, k_p, v_p, mc_p, mn_p, mm_p, g_p, cb_p) = out_p
    (_, k_s, v_s, mc_s, mn_s, mm_s, g_s, cb_s) = out_s
    return (y_p, y_s, k_p, v_p, k_s, v_s, mc_p, mn_p, mm_p, mc_s, mn_s, mm_s, g_p, g_s, cb_p, cb_s)
```

```python
import functools
import math

import numpy as np
import jax
import jax.numpy as jnp
from jax import lax
from jax.experimental import pallas as pl
from jax.experimental.pallas import tpu as pltpu

F32 = jnp.float32
BF16 = jnp.bfloat16

D_MODEL = 1024
ML_HEADS, ML_DK, ML_DV = 4, 64, 128
GLA_HEADS, GLA_DK, GLA_DV = 4, 64, 128
GLA_RANK = 16
GLA_TAU = 16.0
SB_HEADS, SB_DH = 8, 64
BRANCH_W = 512
N_BRANCH = 3
D_FF = 2816
CONV_W = 3
EPS = 1e-6
NEG_BIG = -1e30
SCAN_CHUNK = 64

LANES = 128
SUBLANES = 8
VMEM_LIMIT = 56 * 1024 * 1024

N_MAIN = 6656
BLK_ML_V, BLK_ML_O, BLK_GLA_V, BLK_GLA_O, BLK_SB_Q = 6, 7, 8, 9, 10
BLK_ML_Q, BLK_ML_K, BLK_GLA_Q, BLK_GLA_K = 22, 23, 24, 25
MAIN_TN = 6656
LOG2E = 1.4426950408889634
SMALL_W = 128
SMALL_T_ROWS = 32


def _nt(a, b, precision=None):
    return lax.dot_general(a, b, (((1,), (1,)), ((), ())), preferred_element_type=F32, precision=precision)


def _tn(a, b):
    return lax.dot_general(a, b, (((0,), (0,)), ((), ())), preferred_element_type=F32)


def _mm(a, b, precision=None):
    return jnp.dot(a, b, preferred_element_type=F32, precision=precision)


def _cumsum_mm(tri, x, *, tri_first):
    t = tri.astype(BF16)
    hi = x.astype(BF16)
    r1 = x - hi.astype(F32)
    mid = r1.astype(BF16)
    lo = (r1 - mid.astype(F32)).astype(BF16)
    parts = (hi, mid, lo)
    terms = [_mm(t, p) if tri_first else _mm(p, t) for p in parts]
    return terms[0] + (terms[1] + terms[2])


def _log_sigmoid(x):
    return jnp.minimum(x, 0.0) - jnp.log1p(jnp.exp(-jnp.abs(x)))


def _softplus2(x):
    neg_abs = lax.bitcast_convert_type(lax.bitcast_convert_type(x, jnp.uint32) | jnp.uint32(0x80000000), F32)
    return jnp.maximum(x, 0.0) + jnp.log2(1.0 + jnp.exp2(neg_abs))


def _params(sem):
    return pltpu.CompilerParams(dimension_semantics=sem, vmem_limit_bytes=VMEM_LIMIT)


def _mod_kernel(c_ref, w_ref, b_ref, o_ref):
    c = c_ref[...]
    s = c * jax.nn.sigmoid(c)
    o_ref[...] = _mm(s.astype(BF16), w_ref[...].astype(BF16)) + b_ref[...]


def _modulation(c_all, w_ada, b_ada):
    depth, d, n = w_ada.shape
    nb = c_all.shape[0]
    tn = 1536
    return pl.pallas_call(
        _mod_kernel,
        grid=(depth, n // tn),
        in_specs=[
            pl.BlockSpec((nb, d), lambda l, j: (0, 0)),
            pl.BlockSpec((None, d, tn), lambda l, j: (l, 0, j)),
            pl.BlockSpec((None, 1, tn), lambda l, j: (l, 0, j)),
        ],
        out_specs=pl.BlockSpec((None, nb, tn), lambda l, j: (l, 0, j)),
        out_shape=jax.ShapeDtypeStruct((depth, nb, n), F32),
        compiler_params=_params(("arbitrary", "arbitrary")),
        name="adaln_mod",
    )(c_all, w_ada, b_ada.reshape(depth, 1, n))


def _mod_spec(mods, k, layer, group_of):
    arr, depth = mods
    groups = arr.shape[0] // (6 * depth)
    base = (k * depth + layer) * groups
    return pl.BlockSpec((None,) + arr.shape[1:], lambda *g: (base + group_of(*g), 0, 0))


def _normmm_kernel(x_ref, nw_ref, sc_ref, sh_ref, w_ref, ws_ref, wk_ref, wv_ref, kbuf_ref, vbuf_ref,
                   o_ref, os_ref, ot_ref, kt_ref, vt_ref, h_scr):
    del kbuf_ref, vbuf_ref

    @pl.when(pl.program_id(1) == 0)
    def _():
        x = x_ref[...]
        y = x * lax.rsqrt(jnp.mean(x * x, axis=-1, keepdims=True) + EPS)
        h = y * nw_ref[...] * (1.0 + sc_ref[...]) + sh_ref[...]
        hb = h.astype(BF16)
        h_scr[...] = hb
        os_ref[...] = _nt(hb, ws_ref[...])
        ot_ref[...] = _nt(ws_ref[0:SMALL_T_ROWS, :], hb)
        kt_ref[...] = _nt(wk_ref[...], hb)
        vt_ref[...] = _nt(wv_ref[...], hb)

    o_ref[...] = _nt(h_scr[...], w_ref[...]).astype(o_ref.dtype)


def _norm_proj(x, nw, mods, weights, kbuf, vbuf, layer, *, tm, rows_per_group, main_dtype):
    w_main, w_small, w_kt, w_vt = weights
    t, d = x.shape
    n = w_main.shape[1]
    tn = MAIN_TN
    tiles_per_group = rows_per_group // tm
    grp = lambda i, j: i // tiles_per_group
    kv_rows = w_kt.shape[1]
    kv_map = lambda i, j: (layer, i // tiles_per_group, 0, i % tiles_per_group)
    wmap = lambda i, j: (layer, 0, 0)
    return pl.pallas_call(
        _normmm_kernel,
        grid=(t // tm, n // tn),
        in_specs=[
            pl.BlockSpec((tm, d), lambda i, j: (i, 0)),
            pl.BlockSpec((None, 1, d), wmap),
            _mod_spec(mods, 1, layer, grp),
            _mod_spec(mods, 0, layer, grp),
            pl.BlockSpec((None, tn, d), lambda i, j: (layer, j, 0), pipeline_mode=pl.Buffered(1) if tn == n else None),
            pl.BlockSpec((None, SMALL_W, d), wmap),
            pl.BlockSpec((None, kv_rows, d), wmap),
            pl.BlockSpec((None, kv_rows, d), wmap),
            pl.BlockSpec(memory_space=pl.ANY),
            pl.BlockSpec(memory_space=pl.ANY),
        ],
        out_specs=[
            pl.BlockSpec((tm, tn), lambda i, j: (i, j)),
            pl.BlockSpec((tm, SMALL_W), lambda i, j: (i, 0)),
            pl.BlockSpec((SMALL_T_ROWS, tm), lambda i, j: (0, i)),
            pl.BlockSpec((None, None, kv_rows, tm), kv_map),
            pl.BlockSpec((None, None, kv_rows, tm), kv_map),
        ],
        out_shape=[
            jax.ShapeDtypeStruct((t, n), main_dtype),
            jax.ShapeDtypeStruct((t, SMALL_W), F32),
            jax.ShapeDtypeStruct((SMALL_T_ROWS, t), F32),
            jax.ShapeDtypeStruct(kbuf.shape, F32),
            jax.ShapeDtypeStruct(vbuf.shape, F32),
        ],
        input_output_aliases={8: 3, 9: 4},
        scratch_shapes=[pltpu.VMEM((tm, d), BF16)],
        compiler_params=_params(("arbitrary", "arbitrary")),
        name="norm_in_proj",
    )(x, nw, mods[0], mods[0], w_main, w_small, w_kt, w_vt, kbuf, vbuf)


def _mlstm_kernel(q_ref, k_ref, v_ref, o_ref, sc_ref, st_ref, brow_ref, bcol_ref, nw_ref, c0_ref, n0_ref, m0_ref,
                  hn_ref, c_ref, n_ref, m_ref, *, L):
    @pl.when(pl.program_id(1) == 0)
    def _():
        c_ref[...] = c0_ref[...]
        n_ref[...] = n0_ref[...]
        m_ref[...] = m0_ref[...]

    row = lax.broadcasted_iota(jnp.int32, (L, L), 0)
    col = lax.broadcasted_iota(jnp.int32, (L, L), 1)
    causal = col <= row
    small = sc_ref[...] + brow_ref[...]
    small_t = st_ref[...] + bcol_ref[...]
    bc_all = _cumsum_mm(causal, _log_sigmoid(small), tri_first=True)
    br_all = _cumsum_mm(row <= col, _log_sigmoid(small_t), tri_first=False)
    q = q_ref[...].astype(F32) * (ML_DK ** -0.5)
    k = k_ref[...].astype(F32)
    v = v_ref[...].astype(F32)
    og = o_ref[...].astype(F32)
    nw = nw_ref[...]
    heads = range(ML_HEADS)
    ks = [slice(h * ML_DK, (h + 1) * ML_DK) for h in heads]
    vs = [slice(h * ML_DV, (h + 1) * ML_DV) for h in heads]
    qb = [q[:, ks[h]].astype(BF16) for h in heads]
    vb = [v[:, vs[h]].astype(BF16) for h in heads]
    qk = [_nt(qb[h], k[:, ks[h]].astype(BF16)) for h in heads]
    b_col = [bc_all[:, ML_HEADS + h:ML_HEADS + h + 1] for h in heads]
    b_row = [br_all[ML_HEADS + h:ML_HEADS + h + 1, :] for h in heads]
    m_prev = [m_ref[h:h + 1, :] for h in heads]
    c_st = [c_ref[h] for h in heads]
    n_st = [n_ref[h:h + 1, :] for h in heads]
    dmat = [jnp.where(causal, b_col[h] - b_row[h] + small_t[h:h + 1, :], -jnp.inf) for h in heads]
    inter = [b_col[h] + m_prev[h] for h in heads]
    m_t = [jnp.maximum(inter[h], jnp.max(dmat[h], axis=-1, keepdims=True)) for h in heads]
    s = [qk[h] * jnp.exp(dmat[h] - m_t[h]) for h in heads]
    e_inter = [jnp.exp(inter[h] - m_t[h]) for h in heads]
    qc = [_mm(qb[h], c_st[h].astype(BF16)) for h in heads]
    num = [_mm(s[h].astype(BF16), vb[h]) + e_inter[h] * qc[h] for h in heads]
    den = [jnp.sum(s[h], axis=-1, keepdims=True)
           + e_inter[h] * jnp.sum(q[:, ks[h]] * n_st[h], axis=-1, keepdims=True) for h in heads]
    hh = [num[h] / jnp.maximum(jnp.abs(den[h]), jnp.exp(-m_t[h])) for h in heads]
    for h in heads:
        m_new = m_t[h][L - 1:L, :]
        b_last = b_col[h][L - 1:L, :]
        wk = jnp.exp(b_last - b_col[h] + small[:, h:h + 1] - m_new)
        decay = jnp.exp(b_last + m_prev[h] - m_new)
        kw = k[:, ks[h]] * wk
        c_ref[h] = decay * c_st[h] + _tn(kw.astype(BF16), vb[h])
        n_ref[h:h + 1, :] = decay * n_st[h] + jnp.sum(kw, axis=0, keepdims=True)
        m_ref[h:h + 1, :] = m_new
    for h in heads:
        y = hh[h] * lax.rsqrt(jnp.mean(hh[h] * hh[h], axis=-1, keepdims=True) + EPS) * nw[:, vs[h]]
        hn_ref[:, vs[h]] = (y * jax.nn.sigmoid(og[:, vs[h]])).astype(hn_ref.dtype)


def _small_t_spec(L, nc, per_seq):
    if per_seq:
        return pl.BlockSpec((None, SMALL_T_ROWS, L), lambda b, c: (b, 0, 0))
    return pl.BlockSpec((None, SMALL_T_ROWS, L), lambda b, c: (0, 0, b * nc + c))


def _mlstm(main, small, small_t3, b_row, b_col, nw, c0, n0, m0, layer, *, batch, seq, L, out_dtype, per_seq):
    nc = seq // L
    t = batch * seq
    tok = lambda blk: (lambda b, c: (b * nc + c, blk))
    st = lambda b, c: (b, 0, 0)
    return dict(
        in_specs=[
            pl.BlockSpec((L, 256), tok(BLK_ML_Q)),
            pl.BlockSpec((L, 256), tok(BLK_ML_K)),
            pl.BlockSpec((L, 512), tok(BLK_ML_V)),
            pl.BlockSpec((L, 512), tok(BLK_ML_O)),
            pl.BlockSpec((L, SMALL_W), tok(0)),
            _small_t_spec(L, nc, per_seq),
            pl.BlockSpec((None, 1, SMALL_W), lambda b, c: (layer, 0, 0)),
            pl.BlockSpec((None, SMALL_T_ROWS, 1), lambda b, c: (layer, 0, 0)),
            pl.BlockSpec((None, 1, 512), lambda b, c: (layer, 0, 0)),
            pl.BlockSpec((None, None, ML_HEADS, ML_DK, ML_DV), lambda b, c: (layer, b, 0, 0, 0)),
            pl.BlockSpec((None, None, ML_HEADS, ML_DK), lambda b, c: (layer, b, 0, 0)),
            pl.BlockSpec((None, None, ML_HEADS, 1), lambda b, c: (layer, b, 0, 0)),
        ],
        out_specs=[
            pl.BlockSpec((L, 512), tok(0)),
            pl.BlockSpec((None, ML_HEADS, ML_DK, ML_DV), lambda b, c: (b, 0, 0, 0)),
            pl.BlockSpec((None, ML_HEADS, ML_DK), st),
            pl.BlockSpec((None, ML_HEADS, 1), st),
        ],
        out_shape=[
            jax.ShapeDtypeStruct((t, 512), out_dtype),
            jax.ShapeDtypeStruct((batch, ML_HEADS, ML_DK, ML_DV), F32),
            jax.ShapeDtypeStruct((batch, ML_HEADS, ML_DK), F32),
            jax.ShapeDtypeStruct((batch, ML_HEADS, 1), F32),
        ],
        operands=(main, main, main, main, small, small_t3, b_row, b_col, nw, c0, n0, m0),
    )


def _gla_kernel(q_ref, k_ref, v_ref, o_ref, sc_ref, wg_ref, bg_ref, nw_ref, s0_ref, hn_ref, s_ref, *, L, LS):
    @pl.when(pl.program_id(1) == 0)
    def _():
        s_ref[...] = s0_ref[...]

    row = lax.broadcasted_iota(jnp.int32, (LS, LS), 0)
    col = lax.broadcasted_iota(jnp.int32, (LS, LS), 1)
    causal = col <= row
    la_all = _log_sigmoid(_mm(sc_ref[...].astype(BF16), wg_ref[...]) + bg_ref[...]) * (1.0 / GLA_TAU)
    nw = nw_ref[...]
    heads = range(GLA_HEADS)
    ks = [slice(h * GLA_DK, (h + 1) * GLA_DK) for h in heads]
    vs = [slice(h * GLA_DV, (h + 1) * GLA_DV) for h in heads]
    pre = []
    for c in range(L // LS):
        rs = slice(c * LS, (c + 1) * LS)
        bc = _cumsum_mm(causal, la_all[rs, :], tri_first=True)
        ref_row = bc[LS // 2:LS // 2 + 1, :]
        last = bc[LS - 1:LS, :]
        q = q_ref[rs, :].astype(F32) * (GLA_DK ** -0.5)
        k = k_ref[rs, :].astype(F32)
        qe = (q * jnp.exp(bc - ref_row)).astype(BF16)
        ke = (k * jnp.exp(ref_row - bc)).astype(BF16)
        qs = (q * jnp.exp(bc)).astype(BF16)
        kl = (k * jnp.exp(last - bc)).astype(BF16)
        el = jnp.exp(last)
        v = v_ref[rs, :].astype(F32)
        vb = [v[:, vs[h]].astype(BF16) for h in heads]
        av = [_mm(jnp.where(causal, _nt(qe[:, ks[h]], ke[:, ks[h]]), 0.0).astype(BF16), vb[h]) for h in heads]
        pre.append((rs, qs, kl, el, vb, av))
    state = [s_ref[h] for h in heads]
    outs = []
    for rs, qs, kl, el, vb, av in pre:
        o = [av[h] + _nt(qs[:, ks[h]], state[h].astype(BF16)) for h in heads]
        state = [el[:, ks[h]] * state[h] + _tn(vb[h], kl[:, ks[h]]) for h in heads]
        outs.append((rs, o))
    for h in heads:
        s_ref[h] = state[h]
    for rs, o in outs:
        og = o_ref[rs, :].astype(F32)
        for h in heads:
            y = o[h] * lax.rsqrt(jnp.mean(o[h] * o[h], axis=-1, keepdims=True) + EPS) * nw[:, vs[h]]
            g = og[:, vs[h]]
            hn_ref[rs, vs[h]] = (y * (g * jax.nn.sigmoid(g))).astype(hn_ref.dtype)


def _gla(main, small, wg_pad, bg, nw, s0t, layer, *, batch, seq, L, out_dtype):
    nc = seq // L
    t = batch * seq
    tok = lambda blk: (lambda b, c: (b * nc + c, blk))
    return dict(
        in_specs=[
            pl.BlockSpec((L, 256), tok(BLK_GLA_Q)),
            pl.BlockSpec((L, 256), tok(BLK_GLA_K)),
            pl.BlockSpec((L, 512), tok(BLK_GLA_V)),
            pl.BlockSpec((L, 512), tok(BLK_GLA_O)),
            pl.BlockSpec((L, SMALL_W), tok(0)),
            pl.BlockSpec((None, SMALL_W, 256), lambda b, c: (layer, 0, 0)),
            pl.BlockSpec((None, 1, 256), lambda b, c: (layer, 0, 0)),
            pl.BlockSpec((None, 1, 512), lambda b, c: (layer, 0, 0)),
            pl.BlockSpec((None, None, GLA_HEADS, GLA_DV, GLA_DK), lambda b, c: (layer, b, 0, 0, 0)),
        ],
        out_specs=[
            pl.BlockSpec((L, 512), tok(0)),
            pl.BlockSpec((None, GLA_HEADS, GLA_DV, GLA_DK), lambda b, c: (b, 0, 0, 0)),
        ],
        out_shape=[
            jax.ShapeDtypeStruct((t, 512), out_dtype),
            jax.ShapeDtypeStruct((batch, GLA_HEADS, GLA_DV, GLA_DK), F32),
        ],
        operands=(main, main, main, main, small, wg_pad, bg, nw, s0t),
    )


def _scan_kernel(*refs, N_ML_IN, N_GLA_IN, N_ML_OUT, L, LS):
    ml_in = refs[:N_ML_IN]
    gla_in = refs[N_ML_IN:N_ML_IN + N_GLA_IN]
    outs = refs[N_ML_IN + N_GLA_IN:]
    _mlstm_kernel(*ml_in, *outs[:N_ML_OUT], L=L)
    _gla_kernel(*gla_in, *outs[N_ML_OUT:], L=L, LS=LS)


def _scans(ml, gla, *, batch, seq, L):
    kern = functools.partial(_scan_kernel, N_ML_IN=len(ml["in_specs"]), N_GLA_IN=len(gla["in_specs"]),
                             N_ML_OUT=len(ml["out_specs"]), L=L, LS=math.gcd(L, SCAN_CHUNK))
    return pl.pallas_call(
        kern,
        grid=(batch, seq // L),
        in_specs=ml["in_specs"] + gla["in_specs"],
        out_specs=ml["out_specs"] + gla["out_specs"],
        out_shape=ml["out_shape"] + gla["out_shape"],
        compiler_params=_params(("arbitrary", "arbitrary")),
        name="mlstm_gla_scan",
    )(*ml["operands"], *gla["operands"])


SB_HEAD_GROUP = 4


def _sb_prompt_kernel(qi_ref, kj_ref, q_ref, kt_ref, vt_ref, bias_ref, o_ref, acc_ref, r_ref, qa_ref, *, TQ, TK):
    p = pl.program_id(1)
    i = qi_ref[p]
    j = kj_ref[p]
    ratio = TQ // TK
    head = [slice(h * SB_DH, (h + 1) * SB_DH) for h in range(SB_HEADS)]

    @pl.when(j == ratio * i + ratio - 1)
    def _():
        acc_ref[...] = jnp.zeros_like(acc_ref)
        r_ref[...] = jnp.zeros_like(r_ref)
        qf = q_ref[...].astype(F32) * (SB_DH ** -0.5 * LOG2E)
        lane = lax.broadcasted_iota(jnp.int32, (TQ, LANES), 1)
        ones = jnp.where(lane < SB_DH + 2, 1.0, 0.0)
        for h in range(SB_HEADS):
            pair = qf[:, (h // 2) * LANES:(h // 2 + 1) * LANES]
            if h % 2:
                pair = pltpu.roll(pair, SB_DH, axis=1)
            qa_ref[h] = jnp.where(lane < SB_DH, pair, ones).astype(BF16)

    row = lax.broadcasted_iota(jnp.int32, (TK, TK), 0)
    col = lax.broadcasted_iota(jnp.int32, (TK, TK), 1)
    suffix = (row >= col).astype(BF16)
    bias = bias_ref[...] * LOG2E
    bias_hi = bias.astype(BF16).astype(F32)
    bias_lo = bias - bias_hi
    krow = lax.broadcasted_iota(jnp.int32, (SB_DH, TK), 0)

    def keys(h):
        b = slice(h * SB_DH, h * SB_DH + 1)
        extra = jnp.where(krow == 0, bias_hi[:, b], jnp.where(krow == 1, bias_lo[:, b], 0.0))
        return jnp.concatenate([kt_ref[head[h], :].astype(BF16), extra.astype(BF16)], axis=0)

    def run(rows, diagonal):
        valid = col < row
        for g in range(0, SB_HEADS, SB_HEAD_GROUP):
            hh = range(g, g + SB_HEAD_GROUP)
            zs = [_mm(qa_ref[h, rows, :], keys(h)) for h in hh]
            sps = [_softplus2(z) for z in zs]
            if diagonal:
                sps = [jnp.where(valid, sp, 0.0) for sp in sps]
            rests = [_mm(sp.astype(BF16), suffix) for sp in sps]
            rs = [r_ref[rows, h:h + 1] for h in hh]
            probs = [jnp.exp2(z - rest - r) for z, rest, r in zip(zs, rests, rs)]
            if diagonal:
                probs = [jnp.where(valid, a, 0.0) for a in probs]
            for h, a, r, rest in zip(hh, probs, rs, rests):
                acc_ref[h, rows, :] += _nt(a.astype(BF16), vt_ref[head[h], :].astype(BF16))
                r_ref[rows, h:h + 1] = r + rest[:, 0:1]

    for part in range(ratio):
        rows = slice(part * TK, (part + 1) * TK)
        pl.when(j == ratio * i + part)(functools.partial(run, rows, True))
        if part > 0:
            pl.when(jnp.logical_and(j < ratio * i + part, j >= ratio * i))(functools.partial(run, rows, False))
    pl.when(j < ratio * i)(functools.partial(run, slice(0, TQ), False))

    @pl.when(j == 0)
    def _():
        for h in range(SB_HEADS):
            o_ref[:, head[h]] = acc_ref[h].astype(o_ref.dtype)


def _sb_prompt(main, kbuf, vbuf, bias_row, layer, *, batch, seq, tq, tk):
    nq = seq // tq
    ratio = tq // tk
    qi = np.concatenate([np.full(ratio * (i + 1), i) for i in range(nq)]).astype(np.int32)
    kj = np.concatenate([np.arange(ratio * (i + 1) - 1, -1, -1) for i in range(nq)]).astype(np.int32)
    t = batch * seq
    kern = functools.partial(_sb_prompt_kernel, TQ=tq, TK=tk)
    kv_spec = pl.BlockSpec((None, None, SB_HEADS * SB_DH, tk), lambda b, p, qi, kj: (layer, b, 0, kj[p]))
    grid_spec = pltpu.PrefetchScalarGridSpec(
        num_scalar_prefetch=2,
        grid=(batch, len(qi)),
        in_specs=[
            pl.BlockSpec((tq, 512), lambda b, p, qi, kj: (b * nq + qi[p], BLK_SB_Q)),
            kv_spec,
            kv_spec,
            pl.BlockSpec((None, 1, 512), lambda b, p, qi, kj: (layer, 0, 0)),
        ],
        out_specs=pl.BlockSpec((tq, 512), lambda b, p, qi, kj: (b * nq + qi[p], 0)),
        scratch_shapes=[pltpu.VMEM((SB_HEADS, tq, SB_DH), F32), pltpu.VMEM((tq, LANES), F32),
                        pltpu.VMEM((SB_HEADS, tq, LANES), BF16)],
    )
    return pl.pallas_call(
        kern,
        grid_spec=grid_spec,
        out_shape=jax.ShapeDtypeStruct((t, 512), BF16),
        compiler_params=_params(("arbitrary", "arbitrary")),
        name="sb_prompt",
    )(jnp.asarray(qi), jnp.asarray(kj), main, kbuf, vbuf, bias_row)


def _sb_sample_kernel(pt_ref, q_ref, bias_ref, kn_ref, vn_ref, *rest, PP, PS, NQ):
    kp = rest[:PP]
    vp = rest[PP:2 * PP]
    o_ref, acc_ref, r_ref = rest[2 * PP:]
    s = pl.program_id(1)
    hq = SB_HEADS * NQ
    q = (q_ref[...] * (SB_DH ** -0.5 * LOG2E)).astype(BF16)
    bias = bias_ref[...] * LOG2E
    group = 2 if PP % 2 == 0 else 1
    row = lax.broadcasted_iota(jnp.int32, (group * PS, group * PS), 0)
    col = lax.broadcasted_iota(jnp.int32, (group * PS, group * PS), 1)
    suffix = (row >= col).astype(BF16)

    def logits(get_kt):
        return jnp.concatenate(
            [_mm(q[:, h * SB_DH:(h + 1) * SB_DH], get_kt(h).astype(BF16)) for h in range(SB_HEADS)], axis=0) + bias

    def suffix_sum(sp):
        nk = sp.shape[1]
        return _mm(sp.astype(BF16), suffix[:nk, :nk])

    def attend(z, rest_, r, get_vt, valid):
        a = jnp.exp2(z - rest_ - r)
        if valid is not None:
            a = jnp.where(valid, a, 0.0)
        for h in range(SB_HEADS):
            rows = slice(h * NQ, (h + 1) * NQ)
            acc_ref[rows, :] += _nt(a[rows, :].astype(BF16), get_vt(h).astype(BF16))
        return r + rest_[:, 0:1]

    @pl.when(s == 0)
    def _():
        acc_ref[...] = jnp.zeros_like(acc_ref)
        key = lax.broadcasted_iota(jnp.int32, (hq, PS), 1)
        qpos = lax.broadcasted_iota(jnp.int32, (hq, PS), 0) % NQ
        valid = key < qpos
        z = logits(lambda h: kn_ref[h * SB_DH:(h + 1) * SB_DH, :])
        rest_ = suffix_sum(jnp.where(valid, _softplus2(z), 0.0))
        r = attend(z, rest_, 0.0, lambda h: vn_ref[h * SB_DH:(h + 1) * SB_DH, :], valid)
        r_ref[...] = jnp.broadcast_to(r, r_ref.shape)

    def pages(refs, p0):
        return lambda h: jnp.concatenate([refs[p][h] for p in range(p0 + group - 1, p0 - 1, -1)], axis=1)

    starts = range(0, PP, group)
    zs = [logits(pages(kp, p0)) for p0 in starts]
    rests = [suffix_sum(_softplus2(z)) for z in zs]
    r = r_ref[:, 0:1]
    for z, rest_, p0 in zip(zs, rests, starts):
        r = attend(z, rest_, r, pages(vp, p0), None)
    r_ref[...] = jnp.broadcast_to(r, r_ref.shape)

    @pl.when(s == pl.num_programs(1) - 1)
    def _():
        for h in range(SB_HEADS):
            o_ref[:, h * SB_DH:(h + 1) * SB_DH] = acc_ref[h * NQ:(h + 1) * NQ, :]


def _sb_sample(page_table, q, bias_col, k_new_pad, v_new_pad, cache_k, cache_v, layer, *, pages_per_step):
    bs, n_pages = page_table.shape
    ps = cache_k.shape[-1]
    nq = q.shape[1]
    pp = pages_per_step
    steps = n_pages // pp

    def page_spec(r):
        def idx(b, s, pt):
            return (layer, pt[b * n_pages + (n_pages - 1 - (s * pp + r))], 0, 0, 0)
        return pl.BlockSpec((None, None, SB_HEADS, SB_DH, ps), idx)

    kern = functools.partial(_sb_sample_kernel, PP=pp, PS=ps, NQ=nq)
    grid_spec = pltpu.PrefetchScalarGridSpec(
        num_scalar_prefetch=1,
        grid=(bs, steps),
        in_specs=[
            pl.BlockSpec((None, nq, 512), lambda b, s, pt: (b, 0, 0)),
            pl.BlockSpec((None, SB_HEADS * nq, 1), lambda b, s, pt: (layer, 0, 0)),
            pl.BlockSpec((None, 512, ps), lambda b, s, pt: (b, 0, 0)),
            pl.BlockSpec((None, 512, ps), lambda b, s, pt: (b, 0, 0)),
        ] + [page_spec(r) for r in range(pp)] + [page_spec(r) for r in range(pp)],
        out_specs=pl.BlockSpec((None, nq, 512), lambda b, s, pt: (b, 0, 0)),
        scratch_shapes=[pltpu.VMEM((SB_HEADS * nq, SB_DH), F32), pltpu.VMEM((SB_HEADS * nq, LANES), F32)],
    )
    return pl.pallas_call(
        kern,
        grid_spec=grid_spec,
        out_shape=jax.ShapeDtypeStruct((bs, nq, 512), F32),
        compiler_params=_params(("arbitrary", "arbitrary")),
        name="sb_sample",
    )(page_table.reshape(-1), q, bias_col, k_new_pad, v_new_pad, *([cache_k] * pp), *([cache_v] * pp))


def _merge_kernel(hm_ref, hg_ref, hs_ref, g0_ref, g1_ref, g2_ref, x_ref, gate_ref, wb_ref, wo_ref, o_ref):
    merged = jax.nn.sigmoid(g0_ref[...].astype(F32)) * _mm(hm_ref[...].astype(BF16), wb_ref[0])
    merged += jax.nn.sigmoid(g1_ref[...].astype(F32)) * _mm(hg_ref[...].astype(BF16), wb_ref[1])
    merged += jax.nn.sigmoid(g2_ref[...].astype(F32)) * _mm(hs_ref[...].astype(BF16), wb_ref[2])
    mix = _mm(merged.astype(BF16), wo_ref[...])
    o_ref[...] = x_ref[...] + gate_ref[...] * mix


def _merge(hm, hg, hs, main, x, mods, w_branch, w_out, layer, *, tm, rows_per_group):
    t, d = x.shape
    tiles_per_group = rows_per_group // tm
    tok = lambda blk: (lambda i: (i, blk))
    return pl.pallas_call(
        _merge_kernel,
        grid=(t // tm,),
        in_specs=[
            pl.BlockSpec((tm, 512), tok(0)),
            pl.BlockSpec((tm, 512), tok(0)),
            pl.BlockSpec((tm, 512), tok(0)),
            pl.BlockSpec((tm, d), tok(0)),
            pl.BlockSpec((tm, d), tok(1)),
            pl.BlockSpec((tm, d), tok(2)),
            pl.BlockSpec((tm, d), tok(0)),
            _mod_spec(mods, 2, layer, lambda i: i // tiles_per_group),
            pl.BlockSpec((None, N_BRANCH, BRANCH_W, d), lambda i: (layer, 0, 0, 0)),
            pl.BlockSpec((None, d, d), lambda i: (layer, 0, 0)),
        ],
        out_specs=pl.BlockSpec((tm, d), tok(0)),
        out_shape=jax.ShapeDtypeStruct((t, d), F32),
        compiler_params=_params(("arbitrary",)),
        name="branch_merge",
    )(hm, hg, hs, main, main, main, x, mods[0], w_branch, w_out)


FF_CHUNK = 256


def _ffn_kernel(x_ref, nw_ref, sc_ref, sh_ref, gate_ref, wu_ref, cw_ref, cb_ref, wd_ref, p1_ref, p2_ref, fw_ref,
                o_ref, u_ref, carry_ref, act_ref, *, TM, SEQ_IN_TILE, FINAL_NORM):
    x = x_ref[...]
    y = x * lax.rsqrt(jnp.mean(x * x, axis=-1, keepdims=True) + EPS)
    hb = (y * nw_ref[...] * (1.0 + sc_ref[...]) + sh_ref[...]).astype(BF16)
    row = lax.broadcasted_iota(jnp.int32, (TM, FF_CHUNK), 0)
    if SEQ_IN_TILE is None:
        @pl.when(pl.program_id(1) == 0)
        def _():
            carry_ref[...] = p1_ref[...]
        pos = row
    else:
        pos = row % SEQ_IN_TILE
    for c in range(D_FF // FF_CHUNK):
        cs = slice(c * FF_CHUNK, (c + 1) * FF_CHUNK)
        vs = slice(D_FF + c * FF_CHUNK, D_FF + (c + 1) * FF_CHUNK)
        u = _mm(hb, wu_ref[:, cs])
        val = _mm(hb, wu_ref[:, vs])
        if SEQ_IN_TILE is None:
            prev = carry_ref[:, cs]
            prev1 = prev[SUBLANES - 1:SUBLANES, :]
            prev2 = jnp.where(row == 0, prev[SUBLANES - 2:SUBLANES - 1, :], prev1)
            carry_ref[:, cs] = u[TM - SUBLANES:TM, :]
            u_ref[:, cs] = u[TM - SUBLANES:TM, :]
        else:
            prev1 = p1_ref[:, cs]
            prev2 = p2_ref[:, cs]
            u_ref[:, cs] = u
        u_m1 = jnp.where(pos >= 1, pltpu.roll(u, 1, axis=0), prev1)
        u_m2 = jnp.where(pos >= 2, pltpu.roll(u, 2, axis=0), prev2)
        cw = cw_ref[:, cs]
        conv = cb_ref[:, cs] + cw[0:1, :] * u_m2 + cw[1:2, :] * u_m1 + cw[2:3, :] * u
        act = 0.5 * conv * (1.0 + lax.erf(conv * (2.0 ** -0.5))) * val
        act_ref[:, cs] = act.astype(BF16)
    out = x + gate_ref[...] * _mm(act_ref[...], wd_ref[...])
    if FINAL_NORM:
        out = out * lax.rsqrt(jnp.mean(out * out, axis=-1, keepdims=True) + EPS) * fw_ref[...]
    o_ref[...] = out


def _ffn(x, nw, mods, w_up, conv_w, conv_b, w_down, p1, p2, final_w, layer, *, batch, seq, tm, seq_in_tile,
         final_norm):
    t, d = x.shape
    per_tile = seq_in_tile is not None
    if per_tile:
        grid = (1, t // tm)
        rowmap = lambda b, i: (i, 0)
        grp = lambda b, i: i
        pspec = pl.BlockSpec((None, tm, D_FF), lambda b, i: (layer, i, 0))
        uspec = pl.BlockSpec((tm, D_FF), rowmap)
        ushape = jax.ShapeDtypeStruct((t, D_FF), F32)
    else:
        nt = seq // tm
        grid = (batch, nt)
        rowmap = lambda b, i: (b * nt + i, 0)
        grp = lambda b, i: b
        pspec = pl.BlockSpec((None, None, SUBLANES, D_FF), lambda b, i: (layer, b, 0, 0))
        uspec = pl.BlockSpec((None, SUBLANES, D_FF), lambda b, i: (b, 0, 0))
        ushape = jax.ShapeDtypeStruct((batch, SUBLANES, D_FF), F32)
    const3 = lambda b, i: (layer, 0, 0)
    once = pl.Buffered(1)
    kern = functools.partial(_ffn_kernel, TM=tm, SEQ_IN_TILE=seq_in_tile, FINAL_NORM=final_norm)
    return pl.pallas_call(
        kern,
        grid=grid,
        in_specs=[
            pl.BlockSpec((tm, d), rowmap),
            pl.BlockSpec((None, 1, d), const3),
            _mod_spec(mods, 4, layer, grp),
            _mod_spec(mods, 3, layer, grp),
            _mod_spec(mods, 5, layer, grp),
            pl.BlockSpec((None, d, 2 * D_FF), const3, pipeline_mode=once),
            pl.BlockSpec((None, CONV_W, D_FF), const3),
            pl.BlockSpec((None, 1, D_FF), const3),
            pl.BlockSpec((None, D_FF, d), const3, pipeline_mode=once),
            pspec,
            pspec,
            pl.BlockSpec((1, d), lambda b, i: (0, 0)),
        ],
        out_specs=[pl.BlockSpec((tm, d), rowmap), uspec],
        out_shape=[jax.ShapeDtypeStruct((t, d), F32), ushape],
        scratch_shapes=[pltpu.VMEM((SUBLANES, D_FF), F32), pltpu.VMEM((tm, D_FF), BF16)],
        compiler_params=_params(("arbitrary", "arbitrary")),
        name="conv_ffn",
    )(x, nw, mods[0], mods[0], mods[0], w_up, conv_w, conv_b, w_down, p1, p2, final_w)


def _prep_weights(w_in, gla_w_gate, w_branch, w_out, w_up, w_down):
    widths = (256, 256, 512, 4, 4, 512, 256, 256, 512, GLA_RANK, 512, 512, 512, 512, N_BRANCH * D_MODEL)
    offs = np.concatenate([[0], np.cumsum(widths)])
    names = ("ml_q", "ml_k", "ml_v", "ml_i", "ml_f", "ml_o", "gla_q", "gla_k", "gla_v", "gla_lr", "gla_o",
             "sb_q", "sb_k", "sb_v", "gates")
    w_t = jnp.swapaxes(w_in, 1, 2)
    part = {nm: w_t[:, int(offs[i]):int(offs[i + 1]), :] for i, nm in enumerate(names)}
    order = ("gates", "ml_v", "ml_o", "gla_v", "gla_o", "sb_q", "ml_q", "ml_k", "gla_q", "gla_k")
    w_main = jnp.concatenate([part[nm] for nm in order], axis=1).astype(BF16)
    w_kt = part["sb_k"].astype(BF16)
    w_vt = part["sb_v"].astype(BF16)
    small = jnp.concatenate([part["ml_i"], part["ml_f"], part["gla_lr"]], axis=1)
    w_small = jnp.pad(small, ((0, 0), (0, SMALL_W - small.shape[1]), (0, 0))).astype(BF16)
    wg_pad = jnp.pad(gla_w_gate, ((0, 0), (2 * ML_HEADS, SMALL_W - 2 * ML_HEADS - GLA_RANK), (0, 0))).astype(BF16)
    return ((w_main, w_small, w_kt, w_vt), wg_pad, w_branch.astype(BF16), w_out.astype(BF16),
            w_up.astype(BF16), w_down.astype(BF16))


def _run_group(x, mod, states, paged, weights, params, *, tm, scan_len, sb_block):
    (w_proj, wg_pad, w_branch, w_out, w_up, w_down) = weights
    (norm1_w, norm2_w, ml_b_i, ml_b_f, ml_norm_w, gla_b_gate, gla_norm_w, sb_bias, conv_w, conv_b, final_w) = params
    ml_c0, ml_n0, ml_m0, gla_s0, conv0 = states
    batch, seq, d = x.shape
    depth = w_proj[0].shape[0]
    t = batch * seq
    per_seq = seq < tm
    xt = x.reshape(t, d)
    scan_dtype = F32 if per_seq else BF16
    kv_groups = 1 if per_seq else batch
    kbuf = jnp.zeros((depth, kv_groups, SB_HEADS * SB_DH, t // kv_groups), F32)
    vbuf = jnp.zeros((depth, kv_groups, SB_HEADS * SB_DH, t // kv_groups), F32)
    mcs, mns, mms, gss, bufs = [], [], [], [], []
    rows_per_group = tm if per_seq else seq
    tm_proj = tm
    tm_big = tm if per_seq else math.gcd(seq, 2 * tm)
    if per_seq:
        mods = (jnp.repeat(mod, seq, axis=1).transpose(2, 0, 1, 3).reshape(6 * depth * (t // tm), tm, d), depth)
    else:
        mods = (mod.transpose(2, 0, 1, 3).reshape(6 * depth * batch, 1, d), depth)
    gate_bias = jnp.concatenate([ml_b_i, ml_b_f], axis=1)
    b_row = jnp.pad(gate_bias, ((0, 0), (0, SMALL_W - 2 * ML_HEADS))).reshape(depth, 1, SMALL_W)
    b_col = jnp.pad(gate_bias, ((0, 0), (0, SMALL_T_ROWS - 2 * ML_HEADS))).reshape(depth, SMALL_T_ROWS, 1)
    ml_m0 = ml_m0.reshape(depth, batch, ML_HEADS, 1)
    gla_s0t = jnp.swapaxes(gla_s0, -1, -2)
    if paged is None:
        sb_bias_all = jnp.repeat(sb_bias, SB_DH, axis=1).reshape(depth, 1, SB_HEADS * SB_DH)
        conv_p1 = conv_p2 = jnp.pad(conv0, ((0, 0), (0, 0), (SUBLANES - (CONV_W - 1), 0), (0, 0)))
    else:
        sb_bias_all = jnp.repeat(sb_bias, seq, axis=1).reshape(depth, SB_HEADS * seq, 1)
        zero = jnp.zeros((depth, batch, seq - 1, D_FF), F32)
        conv_p1 = jnp.concatenate([conv0[:, :, 1:2], zero], axis=2).reshape(depth, t, D_FF)
        conv_p2 = jnp.concatenate([conv0, zero[:, :, 1:]], axis=2).reshape(depth, t, D_FF)
    for l in range(depth):
        last = l == depth - 1
        main, small, small_t, kbuf, vbuf = _norm_proj(xt, norm1_w, mods, w_proj, kbuf, vbuf, l,
                                                      tm=tm_proj, rows_per_group=rows_per_group,
                                                      main_dtype=scan_dtype)
        if per_seq:
            small_t3 = small_t.reshape(SMALL_T_ROWS, batch, seq).transpose(1, 0, 2)
        else:
            small_t3 = small_t.reshape(1, SMALL_T_ROWS, t)
        ml = _mlstm(main, small, small_t3, b_row, b_col, ml_norm_w, ml_c0, ml_n0, ml_m0, l,
                    batch=batch, seq=seq, L=scan_len, out_dtype=scan_dtype, per_seq=per_seq)
        gla = _gla(main, small, wg_pad, gla_b_gate, gla_norm_w, gla_s0t, l,
                   batch=batch, seq=seq, L=scan_len, out_dtype=scan_dtype)
        hm, mc, mn, mm, hg, gst = _scans(ml, gla, batch=batch, seq=seq, L=scan_len)
        if paged is None:
            hs = _sb_prompt(main, kbuf, vbuf, sb_bias_all, l, batch=batch, seq=seq, tq=sb_block[0], tk=sb_block[1])
        else:
            cache_kt, cache_vt, page_table = paged
            ps = cache_kt.shape[-1]
            q = main[:, BLK_SB_Q * 512:(BLK_SB_Q + 1) * 512].reshape(batch, seq, 512)
            new_t = lambda buf: jnp.pad(buf[l, 0].reshape(-1, batch, seq).transpose(1, 0, 2),
                                        ((0, 0), (0, 0), (0, ps - seq)))
            hs = _sb_sample(page_table, q, sb_bias_all, new_t(kbuf), new_t(vbuf), cache_kt, cache_vt, l,
                            pages_per_step=math.gcd(page_table.shape[1], 16)).reshape(t, 512)
        xt = _merge(hm, hg, hs, main, xt, mods, w_branch, w_out, l, tm=tm_big, rows_per_group=rows_per_group)
        if per_seq:
            xt, u = _ffn(xt, norm2_w, mods, w_up, conv_w, conv_b, w_down, conv_p1, conv_p2, final_w, l,
                         batch=batch, seq=seq, tm=tm, seq_in_tile=seq, final_norm=last)
            full = jnp.concatenate([conv0[l], u.reshape(batch, seq, D_FF)], axis=1)
            buf = full[:, seq:]
        else:
            xt, tail = _ffn(xt, norm2_w, mods, w_up, conv_w, conv_b, w_down, conv_p1, conv_p2, final_w, l,
                            batch=batch, seq=seq, tm=tm_big, seq_in_tile=None, final_norm=last)
            buf = tail[:, SUBLANES - (CONV_W - 1):]
        mcs.append(mc)
        mns.append(mn)
        mms.append(mm.reshape(batch, ML_HEADS))
        gss.append(jnp.swapaxes(gst, -1, -2))
        bufs.append(buf)
    st = jnp.stack

    def kv_out(buf):
        b6 = buf.reshape(depth, kv_groups, SB_HEADS, SB_DH, batch // kv_groups, seq)
        return b6.transpose(0, 1, 4, 5, 2, 3).reshape(depth, batch, seq, SB_HEADS, SB_DH)

    return xt, kv_out(kbuf), kv_out(vbuf), st(mcs), st(mns), st(mms), st(gss), st(bufs)


def kernel(x_prompt, x_sample, cache_k, cache_v, state_mlstm_c, state_mlstm_n, state_mlstm_m, state_gla, state_conv, page_table, c_prompt, c_sample, w_ada, b_ada, norm1_w, norm2_w, w_in, ml_b_i, ml_b_f, ml_norm_w, gla_w_gate, gla_b_gate, gla_norm_w, sb_bias, w_branch, w_out, w_up, conv_w, conv_b, w_down, final_norm_w):
    depth = w_in.shape[0]
    bp, sp, d = x_prompt.shape
    bs, ss, _ = x_sample.shape
    weights = _prep_weights(w_in, gla_w_gate, w_branch, w_out, w_up, w_down)
    params = (norm1_w.reshape(depth, 1, d), norm2_w.reshape(depth, 1, d), ml_b_i, ml_b_f,
              ml_norm_w.reshape(depth, 1, -1),
              gla_b_gate.reshape(depth, 1, -1), gla_norm_w.reshape(depth, 1, -1), sb_bias, conv_w,
              conv_b.reshape(depth, 1, -1), final_norm_w.reshape(1, d))
    mod = _modulation(jnp.concatenate([c_prompt, c_sample], axis=0), w_ada, b_ada)
    mod_p = mod[:, :bp].reshape(depth, bp, 6, d)
    mod_s = mod[:, bp:].reshape(depth, bs, 6, d)

    zeros = lambda *shape: jnp.zeros((depth, bp) + shape, F32)
    states_p = (zeros(ML_HEADS, ML_DK, ML_DV), zeros(ML_HEADS, ML_DK),
                jnp.full((depth, bp, ML_HEADS), NEG_BIG, F32), zeros(GLA_HEADS, GLA_DK, GLA_DV),
                zeros(CONV_W - 1, D_FF))
    tm_p = math.gcd(sp, 512)
    scan_p = math.gcd(sp, 512)
    out_p = _run_group(x_prompt, mod_p, states_p, None, weights, params, tm=tm_p, scan_len=scan_p,
                       sb_block=(math.gcd(sp, 512), math.gcd(sp, 256)))

    states_s = (state_mlstm_c, state_mlstm_n, state_mlstm_m, state_gla, state_conv)
    cache_kt = jnp.transpose(cache_k, (0, 1, 3, 4, 2))
    cache_vt = jnp.transpose(cache_v, (0, 1, 3, 4, 2))
    out_s = _run_group(x_sample, mod_s, states_s, (cache_kt, cache_vt, page_table), weights, params,
                       tm=bs * ss, scan_len=ss, sb_block=None)

    y_p = out_p[0].reshape(bp, sp, d)
    y_s = out_s[0].reshape(bs, ss, d)
    (_, k_p, v_p, mc_p, mn_p, mm_p, g_p, cb_p) = out_p
    (_, k_s, v_s, mc_s, mn_s, mm_s, g_s, cb_s) = out_s
    return (y_p, y_s, k_p, v_p, k_s, v_s, mc_p, mn_p, mm_p, mc_s, mn_s, mm_s, g_p, g_s, cb_p, cb_s)
```

```python
import functools
import math

import numpy as np
import jax
import jax.numpy as jnp
from jax import lax
from jax.experimental import pallas as pl
from jax.experimental.pallas import tpu as pltpu

F32 = jnp.float32
BF16 = jnp.bfloat16

D_MODEL = 1024
ML_HEADS, ML_DK, ML_DV = 4, 64, 128
GLA_HEADS, GLA_DK, GLA_DV = 4, 64, 128
GLA_RANK = 16
GLA_TAU = 16.0
SB_HEADS, SB_DH = 8, 64
BRANCH_W = 512
N_BRANCH = 3
D_FF = 2816
CONV_W = 3
EPS = 1e-6
NEG_BIG = -1e30
SCAN_CHUNK = 64

LANES = 128
SUBLANES = 8
VMEM_LIMIT = 56 * 1024 * 1024

N_MAIN = 6656
BLK_ML_V, BLK_ML_O, BLK_GLA_V, BLK_GLA_O, BLK_SB_Q = 6, 7, 8, 9, 10
BLK_ML_Q, BLK_ML_K, BLK_GLA_Q, BLK_GLA_K = 22, 23, 24, 25
MAIN_TN = 6656
LOG2E = 1.4426950408889634
SMALL_W = 128
SMALL_T_ROWS = 32


def _nt(a, b, precision=None):
    return lax.dot_general(a, b, (((1,), (1,)), ((), ())), preferred_element_type=F32, precision=precision)


def _tn(a, b):
    return lax.dot_general(a, b, (((0,), (0,)), ((), ())), preferred_element_type=F32)


def _mm(a, b, precision=None):
    return jnp.dot(a, b, preferred_element_type=F32, precision=precision)


def _cumsum_mm(tri, x, *, tri_first):
    t = tri.astype(BF16)
    hi = x.astype(BF16)
    r1 = x - hi.astype(F32)
    mid = r1.astype(BF16)
    lo = (r1 - mid.astype(F32)).astype(BF16)
    parts = (hi, mid, lo)
    terms = [_mm(t, p) if tri_first else _mm(p, t) for p in parts]
    return terms[0] + (terms[1] + terms[2])


def _log_sigmoid(x):
    return jnp.minimum(x, 0.0) - jnp.log1p(jnp.exp(-jnp.abs(x)))


def _softplus2(x):
    neg_abs = lax.bitcast_convert_type(lax.bitcast_convert_type(x, jnp.uint32) | jnp.uint32(0x80000000), F32)
    return jnp.maximum(x, 0.0) + jnp.log2(1.0 + jnp.exp2(neg_abs))


def _params(sem):
    return pltpu.CompilerParams(dimension_semantics=sem, vmem_limit_bytes=VMEM_LIMIT)


def _mod_kernel(c_ref, w_ref, b_ref, o_ref):
    c = c_ref[...]
    s = c * jax.nn.sigmoid(c)
    o_ref[...] = _mm(s.astype(BF16), w_ref[...].astype(BF16)) + b_ref[...]


def _modulation(c_all, w_ada, b_ada):
    depth, d, n = w_ada.shape
    nb = c_all.shape[0]
    tn = 1536
    return pl.pallas_call(
        _mod_kernel,
        grid=(depth, n // tn),
        in_specs=[
            pl.BlockSpec((nb, d), lambda l, j: (0, 0)),
            pl.BlockSpec((None, d, tn), lambda l, j: (l, 0, j)),
            pl.BlockSpec((None, 1, tn), lambda l, j: (l, 0, j)),
        ],
        out_specs=pl.BlockSpec((None, nb, tn), lambda l, j: (l, 0, j)),
        out_shape=jax.ShapeDtypeStruct((depth, nb, n), F32),
        compiler_params=_params(("arbitrary", "arbitrary")),
        name="adaln_mod",
    )(c_all, w_ada, b_ada.reshape(depth, 1, n))


def _mod_spec(mods, k, layer, group_of):
    arr, depth = mods
    groups = arr.shape[0] // (6 * depth)
    base = (k * depth + layer) * groups
    return pl.BlockSpec((None,) + arr.shape[1:], lambda *g: (base + group_of(*g), 0, 0))


def _normmm_kernel(x_ref, nw_ref, sc_ref, sh_ref, w_ref, ws_ref, wk_ref, wv_ref, kbuf_ref, vbuf_ref,
                   o_ref, os_ref, ot_ref, kt_ref, vt_ref, h_scr):
    del kbuf_ref, vbuf_ref

    @pl.when(pl.program_id(1) == 0)
    def _():
        x = x_ref[...]
        y = x * lax.rsqrt(jnp.mean(x * x, axis=-1, keepdims=True) + EPS)
        h = y * nw_ref[...] * (1.0 + sc_ref[...]) + sh_ref[...]
        hb = h.astype(BF16)
        h_scr[...] = hb
        os_ref[...] = _nt(hb, ws_ref[...])
        ot_ref[...] = _nt(ws_ref[0:SMALL_T_ROWS, :], hb)
        kt_ref[...] = _nt(wk_ref[...], hb)
        vt_ref[...] = _nt(wv_ref[...], hb)

    o_ref[...] = _nt(h_scr[...], w_ref[...]).astype(o_ref.dtype)


def _norm_proj(x, nw, mods, weights, kbuf, vbuf, layer, *, tm, rows_per_group, main_dtype):
    w_main, w_small, w_kt, w_vt = weights
    t, d = x.shape
    n = w_main.shape[1]
    tn = MAIN_TN
    tiles_per_group = rows_per_group // tm
    grp = lambda i, j: i // tiles_per_group
    kv_rows = w_kt.shape[1]
    kv_map = lambda i, j: (layer, i // tiles_per_group, 0, i % tiles_per_group)
    wmap = lambda i, j: (layer, 0, 0)
    return pl.pallas_call(
        _normmm_kernel,
        grid=(t // tm, n // tn),
        in_specs=[
            pl.BlockSpec((tm, d), lambda i, j: (i, 0)),
            pl.BlockSpec((None, 1, d), wmap),
            _mod_spec(mods, 1, layer, grp),
            _mod_spec(mods, 0, layer, grp),
            pl.BlockSpec((None, tn, d), lambda i, j: (layer, j, 0), pipeline_mode=pl.Buffered(1) if tn == n else None),
            pl.BlockSpec((None, SMALL_W, d), wmap),
            pl.BlockSpec((None, kv_rows, d), wmap),
            pl.BlockSpec((None, kv_rows, d), wmap),
            pl.BlockSpec(memory_space=pl.ANY),
            pl.BlockSpec(memory_space=pl.ANY),
        ],
        out_specs=[
            pl.BlockSpec((tm, tn), lambda i, j: (i, j)),
            pl.BlockSpec((tm, SMALL_W), lambda i, j: (i, 0)),
            pl.BlockSpec((SMALL_T_ROWS, tm), lambda i, j: (0, i)),
            pl.BlockSpec((None, None, kv_rows, tm), kv_map),
            pl.BlockSpec((None, None, kv_rows, tm), kv_map),
        ],
        out_shape=[
            jax.ShapeDtypeStruct((t, n), main_dtype),
            jax.ShapeDtypeStruct((t, SMALL_W), F32),
            jax.ShapeDtypeStruct((SMALL_T_ROWS, t), F32),
            jax.ShapeDtypeStruct(kbuf.shape, F32),
            jax.ShapeDtypeStruct(vbuf.shape, F32),
        ],
        input_output_aliases={8: 3, 9: 4},
        scratch_shapes=[pltpu.VMEM((tm, d), BF16)],
        compiler_params=_params(("arbitrary", "arbitrary")),
        name="norm_in_proj",
    )(x, nw, mods[0], mods[0], w_main, w_small, w_kt, w_vt, kbuf, vbuf)


def _mlstm_kernel(q_ref, k_ref, v_ref, o_ref, sc_ref, st_ref, brow_ref, bcol_ref, nw_ref, c0_ref, n0_ref, m0_ref,
                  hn_ref, c_ref, n_ref, m_ref, *, L):
    @pl.when(pl.program_id(1) == 0)
    def _():
        c_ref[...] = c0_ref[...]
        n_ref[...] = n0_ref[...]
        m_ref[...] = m0_ref[...]

    row = lax.broadcasted_iota(jnp.int32, (L, L), 0)
    col = lax.broadcasted_iota(jnp.int32, (L, L), 1)
    causal = col <= row
    small = sc_ref[...] + brow_ref[...]
    small_t = st_ref[...] + bcol_ref[...]
    bc_all = _cumsum_mm(causal, _log_sigmoid(small), tri_first=True)
    br_all = _cumsum_mm(row <= col, _log_sigmoid(small_t), tri_first=False)
    q = q_ref[...].astype(F32) * (ML_DK ** -0.5)
    k = k_ref[...].astype(F32)
    v = v_ref[...].astype(F32)
    og = o_ref[...].astype(F32)
    nw = nw_ref[...]
    heads = range(ML_HEADS)
    ks = [slice(h * ML_DK, (h + 1) * ML_DK) for h in heads]
    vs = [slice(h * ML_DV, (h + 1) * ML_DV) for h in heads]
    qb = [q[:, ks[h]].astype(BF16) for h in heads]
    vb = [v[:, vs[h]].astype(BF16) for h in heads]
    qk = [_nt(qb[h], k[:, ks[h]].astype(BF16)) for h in heads]
    b_col = [bc_all[:, ML_HEADS + h:ML_HEADS + h + 1] for h in heads]
    b_row = [br_all[ML_HEADS + h:ML_HEADS + h + 1, :] for h in heads]
    m_prev = [m_ref[h:h + 1, :] for h in heads]
    c_st = [c_ref[h] for h in heads]
    n_st = [n_ref[h:h + 1, :] for h in heads]
    dmat = [jnp.where(causal, b_col[h] - b_row[h] + small_t[h:h + 1, :], -jnp.inf) for h in heads]
    inter = [b_col[h] + m_prev[h] for h in heads]
    m_t = [jnp.maximum(inter[h], jnp.max(dmat[h], axis=-1, keepdims=True)) for h in heads]
    s = [qk[h] * jnp.exp(dmat[h] - m_t[h]) for h in heads]
    e_inter = [jnp.exp(inter[h] - m_t[h]) for h in heads]
    qc = [_mm(qb[h], c_st[h].astype(BF16)) for h in heads]
    num = [_mm(s[h].astype(BF16), vb[h]) + e_inter[h] * qc[h] for h in heads]
    den = [jnp.sum(s[h], axis=-1, keepdims=True)
           + e_inter[h] * jnp.sum(q[:, ks[h]] * n_st[h], axis=-1, keepdims=True) for h in heads]
    hh = [num[h] / jnp.maximum(jnp.abs(den[h]), jnp.exp(-m_t[h])) for h in heads]
    for h in heads:
        m_new = m_t[h][L - 1:L, :]
        b_last = b_col[h][L - 1:L, :]
        wk = jnp.exp(b_last - b_col[h] + small[:, h:h + 1] - m_new)
        decay = jnp.exp(b_last + m_prev[h] - m_new)
        kw = k[:, ks[h]] * wk
        c_ref[h] = decay * c_st[h] + _tn(kw.astype(BF16), vb[h])
        n_ref[h:h + 1, :] = decay * n_st[h] + jnp.sum(kw, axis=0, keepdims=True)
        m_ref[h:h + 1, :] = m_new
    for h in heads:
        y = hh[h] * lax.rsqrt(jnp.mean(hh[h] * hh[h], axis=-1, keepdims=True) + EPS) * nw[:, vs[h]]
        hn_ref[:, vs[h]] = (y * jax.nn.sigmoid(og[:, vs[h]])).astype(hn_ref.dtype)


def _small_t_spec(L, nc, per_seq):
    if per_seq:
        return pl.BlockSpec((None, SMALL_T_ROWS, L), lambda b, c: (b, 0, 0))
    return pl.BlockSpec((None, SMALL_T_ROWS, L), lambda b, c: (0, 0, b * nc + c))


def _mlstm(main, small, small_t3, b_row, b_col, nw, c0, n0, m0, layer, *, batch, seq, L, out_dtype, per_seq):
    nc = seq // L
    t = batch * seq
    tok = lambda blk: (lambda b, c: (b * nc + c, blk))
    st = lambda b, c: (b, 0, 0)
    return dict(
        in_specs=[
            pl.BlockSpec((L, 256), tok(BLK_ML_Q)),
            pl.BlockSpec((L, 256), tok(BLK_ML_K)),
            pl.BlockSpec((L, 512), tok(BLK_ML_V)),
            pl.BlockSpec((L, 512), tok(BLK_ML_O)),
            pl.BlockSpec((L, SMALL_W), tok(0)),
            _small_t_spec(L, nc, per_seq),
            pl.BlockSpec((None, 1, SMALL_W), lambda b, c: (layer, 0, 0)),
            pl.BlockSpec((None, SMALL_T_ROWS, 1), lambda b, c: (layer, 0, 0)),
            pl.BlockSpec((None, 1, 512), lambda b, c: (layer, 0, 0)),
            pl.BlockSpec((None, None, ML_HEADS, ML_DK, ML_DV), lambda b, c: (layer, b, 0, 0, 0)),
            pl.BlockSpec((None, None, ML_HEADS, ML_DK), lambda b, c: (layer, b, 0, 0)),
            pl.BlockSpec((None, None, ML_HEADS, 1), lambda b, c: (layer, b, 0, 0)),
        ],
        out_specs=[
            pl.BlockSpec((L, 512), tok(0)),
            pl.BlockSpec((None, ML_HEADS, ML_DK, ML_DV), lambda b, c: (b, 0, 0, 0)),
            pl.BlockSpec((None, ML_HEADS, ML_DK), st),
            pl.BlockSpec((None, ML_HEADS, 1), st),
        ],
        out_shape=[
            jax.ShapeDtypeStruct((t, 512), out_dtype),
            jax.ShapeDtypeStruct((batch, ML_HEADS, ML_DK, ML_DV), F32),
            jax.ShapeDtypeStruct((batch, ML_HEADS, ML_DK), F32),
            jax.ShapeDtypeStruct((batch, ML_HEADS, 1), F32),
        ],
        operands=(main, main, main, main, small, small_t3, b_row, b_col, nw, c0, n0, m0),
    )


def _gla_kernel(q_ref, k_ref, v_ref, o_ref, sc_ref, wg_ref, bg_ref, nw_ref, s0_ref, hn_ref, s_ref, *, L, LS):
    @pl.when(pl.program_id(1) == 0)
    def _():
        s_ref[...] = s0_ref[...]

    row = lax.broadcasted_iota(jnp.int32, (LS, LS), 0)
    col = lax.broadcasted_iota(jnp.int32, (LS, LS), 1)
    causal = col <= row
    la_all = _log_sigmoid(_mm(sc_ref[...].astype(BF16), wg_ref[...]) + bg_ref[...]) * (1.0 / GLA_TAU)
    nw = nw_ref[...]
    heads = range(GLA_HEADS)
    ks = [slice(h * GLA_DK, (h + 1) * GLA_DK) for h in heads]
    vs = [slice(h * GLA_DV, (h + 1) * GLA_DV) for h in heads]
    pre = []
    for c in range(L // LS):
        rs = slice(c * LS, (c + 1) * LS)
        bc = _cumsum_mm(causal, la_all[rs, :], tri_first=True)
        ref_row = bc[LS // 2:LS // 2 + 1, :]
        last = bc[LS - 1:LS, :]
        q = q_ref[rs, :].astype(F32) * (GLA_DK ** -0.5)
        k = k_ref[rs, :].astype(F32)
        qe = (q * jnp.exp(bc - ref_row)).astype(BF16)
        ke = (k * jnp.exp(ref_row - bc)).astype(BF16)
        qs = (q * jnp.exp(bc)).astype(BF16)
        kl = (k * jnp.exp(last - bc)).astype(BF16)
        el = jnp.exp(last)
        v = v_ref[rs, :].astype(F32)
        vb = [v[:, vs[h]].astype(BF16) for h in heads]
        av = [_mm(jnp.where(causal, _nt(qe[:, ks[h]], ke[:, ks[h]]), 0.0).astype(BF16), vb[h]) for h in heads]
        pre.append((rs, qs, kl, el, vb, av))
    state = [s_ref[h] for h in heads]
    outs = []
    for rs, qs, kl, el, vb, av in pre:
        o = [av[h] + _nt(qs[:, ks[h]], state[h].astype(BF16)) for h in heads]
        state = [el[:, ks[h]] * state[h] + _tn(vb[h], kl[:, ks[h]]) for h in heads]
        outs.append((rs, o))
    for h in heads:
        s_ref[h] = state[h]
    for rs, o in outs:
        og = o_ref[rs, :].astype(F32)
        for h in heads:
            y = o[h] * lax.rsqrt(jnp.mean(o[h] * o[h], axis=-1, keepdims=True) + EPS) * nw[:, vs[h]]
            g = og[:, vs[h]]
            hn_ref[rs, vs[h]] = (y * (g * jax.nn.sigmoid(g))).astype(hn_ref.dtype)


def _gla(main, small, wg_pad, bg, nw, s0t, layer, *, batch, seq, L, out_dtype):
    nc = seq // L
    t = batch * seq
    tok = lambda blk: (lambda b, c: (b * nc + c, blk))
    return dict(
        in_specs=[
            pl.BlockSpec((L, 256), tok(BLK_GLA_Q)),
            pl.BlockSpec((L, 256), tok(BLK_GLA_K)),
            pl.BlockSpec((L, 512), tok(BLK_GLA_V)),
            pl.BlockSpec((L, 512), tok(BLK_GLA_O)),
            pl.BlockSpec((L, SMALL_W), tok(0)),
            pl.BlockSpec((None, SMALL_W, 256), lambda b, c: (layer, 0, 0)),
            pl.BlockSpec((None, 1, 256), lambda b, c: (layer, 0, 0)),
            pl.BlockSpec((None, 1, 512), lambda b, c: (layer, 0, 0)),
            pl.BlockSpec((None, None, GLA_HEADS, GLA_DV, GLA_DK), lambda b, c: (layer, b, 0, 0, 0)),
        ],
        out_specs=[
            pl.BlockSpec((L, 512), tok(0)),
            pl.BlockSpec((None, GLA_HEADS, GLA_DV, GLA_DK), lambda b, c: (b, 0, 0, 0)),
        ],
        out_shape=[
            jax.ShapeDtypeStruct((t, 512), out_dtype),
            jax.ShapeDtypeStruct((batch, GLA_HEADS, GLA_DV, GLA_DK), F32),
        ],
        operands=(main, main, main, main, small, wg_pad, bg, nw, s0t),
    )


def _scan_kernel(*refs, N_ML_IN, N_GLA_IN, N_ML_OUT, L, LS):
    ml_in = refs[:N_ML_IN]
    gla_in = refs[N_ML_IN:N_ML_IN + N_GLA_IN]
    outs = refs[N_ML_IN + N_GLA_IN:]
    _mlstm_kernel(*ml_in, *outs[:N_ML_OUT], L=L)
    _gla_kernel(*gla_in, *outs[N_ML_OUT:], L=L, LS=LS)


def _scans(ml, gla, *, batch, seq, L):
    kern = functools.partial(_scan_kernel, N_ML_IN=len(ml["in_specs"]), N_GLA_IN=len(gla["in_specs"]),
                             N_ML_OUT=len(ml["out_specs"]), L=L, LS=math.gcd(L, SCAN_CHUNK))
    return pl.pallas_call(
        kern,
        grid=(batch, seq // L),
        in_specs=ml["in_specs"] + gla["in_specs"],
        out_specs=ml["out_specs"] + gla["out_specs"],
        out_shape=ml["out_shape"] + gla["out_shape"],
        compiler_params=_params(("arbitrary", "arbitrary")),
        name="mlstm_gla_scan",
    )(*ml["operands"], *gla["operands"])


SB_HEAD_GROUP = 4


def _sb_prompt_kernel(qi_ref, kj_ref, q_ref, kt_ref, vt_ref, bias_ref, o_ref, acc_ref, r_ref, qa_ref, *, TQ, TK):
    p = pl.program_id(1)
    i = qi_ref[p]
    j = kj_ref[p]
    ratio = TQ // TK
    head = [slice(h * SB_DH, (h + 1) * SB_DH) for h in range(SB_HEADS)]

    @pl.when(j == ratio * i + ratio - 1)
    def _():
        acc_ref[...] = jnp.zeros_like(acc_ref)
        r_ref[...] = jnp.zeros_like(r_ref)
        qf = q_ref[...].astype(F32) * (SB_DH ** -0.5 * LOG2E)
        lane = lax.broadcasted_iota(jnp.int32, (TQ, LANES), 1)
        ones = jnp.where(lane < SB_DH + 2, 1.0, 0.0)
        for h in range(SB_HEADS):
            pair = qf[:, (h // 2) * LANES:(h // 2 + 1) * LANES]
            if h % 2:
                pair = pltpu.roll(pair, SB_DH, axis=1)
            qa_ref[h] = jnp.where(lane < SB_DH, pair, ones).astype(BF16)

    row = lax.broadcasted_iota(jnp.int32, (TK, TK), 0)
    col = lax.broadcasted_iota(jnp.int32, (TK, TK), 1)
    suffix = (row >= col).astype(BF16)
    bias = bias_ref[...] * LOG2E
    bias_hi = bias.astype(BF16).astype(F32)
    bias_lo = bias - bias_hi
    krow = lax.broadcasted_iota(jnp.int32, (SB_DH, TK), 0)

    def keys(h):
        b = slice(h * SB_DH, h * SB_DH + 1)
        extra = jnp.where(krow == 0, bias_hi[:, b], jnp.where(krow == 1, bias_lo[:, b], 0.0))
        return jnp.concatenate([kt_ref[head[h], :].astype(BF16), extra.astype(BF16)], axis=0)

    def run(rows, diagonal):
        valid = col < row
        for g in range(0, SB_HEADS, SB_HEAD_GROUP):
            hh = range(g, g + SB_HEAD_GROUP)
            zs = [_mm(qa_ref[h, rows, :], keys(h)) for h in hh]
            sps = [_softplus2(z) for z in zs]
            if diagonal:
                sps = [jnp.where(valid, sp, 0.0) for sp in sps]
            rests = [_mm(sp.astype(BF16), suffix) for sp in sps]
            rs = [r_ref[rows, h:h + 1] for h in hh]
            probs = [jnp.exp2(z - rest - r) for z, rest, r in zip(zs, rests, rs)]
            if diagonal:
                probs = [jnp.where(valid, a, 0.0) for a in probs]
            for h, a, r, rest in zip(hh, probs, rs, rests):
                acc_ref[h, rows, :] += _nt(a.astype(BF16), vt_ref[head[h], :].astype(BF16))
                r_ref[rows, h:h + 1] = r + rest[:, 0:1]

    for part in range(ratio):
        rows = slice(part * TK, (part + 1) * TK)
        pl.when(j == ratio * i + part)(functools.partial(run, rows, True))
        if part > 0:
            pl.when(jnp.logical_and(j < ratio * i + part, j >= ratio * i))(functools.partial(run, rows, False))
    pl.when(j < ratio * i)(functools.partial(run, slice(0, TQ), False))

    @pl.when(j == 0)
    def _():
        for h in range(SB_HEADS):
            o_ref[:, head[h]] = acc_ref[h].astype(o_ref.dtype)


def _sb_prompt(main, kbuf, vbuf, bias_row, layer, *, batch, seq, tq, tk):
    nq = seq // tq
    ratio = tq // tk
    qi = np.concatenate([np.full(ratio * (i + 1), i) for i in range(nq)]).astype(np.int32)
    kj = np.concatenate([np.arange(ratio * (i + 1) - 1, -1, -1) for i in range(nq)]).astype(np.int32)
    t = batch * seq
    kern = functools.partial(_sb_prompt_kernel, TQ=tq, TK=tk)
    kv_spec = pl.BlockSpec((None, None, SB_HEADS * SB_DH, tk), lambda b, p, qi, kj: (layer, b, 0, kj[p]))
    grid_spec = pltpu.PrefetchScalarGridSpec(
        num_scalar_prefetch=2,
        grid=(batch, len(qi)),
        in_specs=[
            pl.BlockSpec((tq, 512), lambda b, p, qi, kj: (b * nq + qi[p], BLK_SB_Q)),
            kv_spec,
            kv_spec,
            pl.BlockSpec((None, 1, 512), lambda b, p, qi, kj: (layer, 0, 0)),
        ],
        out_specs=pl.BlockSpec((tq, 512), lambda b, p, qi, kj: (b * nq + qi[p], 0)),
        scratch_shapes=[pltpu.VMEM((SB_HEADS, tq, SB_DH), F32), pltpu.VMEM((tq, LANES), F32),
                        pltpu.VMEM((SB_HEADS, tq, LANES), BF16)],
    )
    return pl.pallas_call(
        kern,
        grid_spec=grid_spec,
        out_shape=jax.ShapeDtypeStruct((t, 512), BF16),
        compiler_params=_params(("arbitrary", "arbitrary")),
        name="sb_prompt",
    )(jnp.asarray(qi), jnp.asarray(kj), main, kbuf, vbuf, bias_row)


def _sb_sample_kernel(pt_ref, q_ref, bias_ref, kn_ref, vn_ref, *rest, PP, PS, NQ):
    kp = rest[:PP]
    vp = rest[PP:2 * PP]
    o_ref, acc_ref, r_ref = rest[2 * PP:]
    s = pl.program_id(1)
    hq = SB_HEADS * NQ
    q = (q_ref[...] * (SB_DH ** -0.5 * LOG2E)).astype(BF16)
    bias = bias_ref[...] * LOG2E
    group = 2 if PP % 2 == 0 else 1
    row = lax.broadcasted_iota(jnp.int32, (group * PS, group * PS), 0)
    col = lax.broadcasted_iota(jnp.int32, (group * PS, group * PS), 1)
    suffix = (row >= col).astype(BF16)

    def logits(get_kt):
        return jnp.concatenate(
            [_mm(q[:, h * SB_DH:(h + 1) * SB_DH], get_kt(h).astype(BF16)) for h in range(SB_HEADS)], axis=0) + bias

    def suffix_sum(sp):
        nk = sp.shape[1]
        return _mm(sp.astype(BF16), suffix[:nk, :nk])

    def attend(z, rest_, r, get_vt, valid):
        a = jnp.exp2(z - rest_ - r)
        if valid is not None:
            a = jnp.where(valid, a, 0.0)
        for h in range(SB_HEADS):
            rows = slice(h * NQ, (h + 1) * NQ)
            acc_ref[rows, :] += _nt(a[rows, :].astype(BF16), get_vt(h).astype(BF16))
        return r + rest_[:, 0:1]

    @pl.when(s == 0)
    def _():
        acc_ref[...] = jnp.zeros_like(acc_ref)
        key = lax.broadcasted_iota(jnp.int32, (hq, PS), 1)
        qpos = lax.broadcasted_iota(jnp.int32, (hq, PS), 0) % NQ
        valid = key < qpos
        z = logits(lambda h: kn_ref[h * SB_DH:(h + 1) * SB_DH, :])
        rest_ = suffix_sum(jnp.where(valid, _softplus2(z), 0.0))
        r = attend(z, rest_, 0.0, lambda h: vn_ref[h * SB_DH:(h + 1) * SB_DH, :], valid)
        r_ref[...] = jnp.broadcast_to(r, r_ref.shape)

    def pages(refs, p0):
        return lambda h: jnp.concatenate([refs[p][h] for p in range(p0 + group - 1, p0 - 1, -1)], axis=1)

    starts = range(0, PP, group)
    zs = [logits(pages(kp, p0)) for p0 in starts]
    rests = [suffix_sum(_softplus2(z)) for z in zs]
    r = r_ref[:, 0:1]
    for z, rest_, p0 in zip(zs, rests, starts):
        r = attend(z, rest_, r, pages(vp, p0), None)
    r_ref[...] = jnp.broadcast_to(r, r_ref.shape)

    @pl.when(s == pl.num_programs(1) - 1)
    def _():
        for h in range(SB_HEADS):
            o_ref[:, h * SB_DH:(h + 1) * SB_DH] = acc_ref[h * NQ:(h + 1) * NQ, :]


def _sb_sample(page_table, q, bias_col, k_new_pad, v_new_pad, cache_k, cache_v, layer, *, pages_per_step):
    bs, n_pages = page_table.shape
    ps = cache_k.shape[-1]
    nq = q.shape[1]
    pp = pages_per_step
    steps = n_pages // pp

    def page_spec(r):
        def idx(b, s, pt):
            return (layer, pt[b * n_pages + (n_pages - 1 - (s * pp + r))], 0, 0, 0)
        return pl.BlockSpec((None, None, SB_HEADS, SB_DH, ps), idx)

    kern = functools.partial(_sb_sample_kernel, PP=pp, PS=ps, NQ=nq)
    grid_spec = pltpu.PrefetchScalarGridSpec(
        num_scalar_prefetch=1,
        grid=(bs, steps),
        in_specs=[
            pl.BlockSpec((None, nq, 512), lambda b, s, pt: (b, 0, 0)),
            pl.BlockSpec((None, SB_HEADS * nq, 1), lambda b, s, pt: (layer, 0, 0)),
            pl.BlockSpec((None, 512, ps), lambda b, s, pt: (b, 0, 0)),
            pl.BlockSpec((None, 512, ps), lambda b, s, pt: (b, 0, 0)),
        ] + [page_spec(r) for r in range(pp)] + [page_spec(r) for r in range(pp)],
        out_specs=pl.BlockSpec((None, nq, 512), lambda b, s, pt: (b, 0, 0)),
        scratch_shapes=[pltpu.VMEM((SB_HEADS * nq, SB_DH), F32), pltpu.VMEM((SB_HEADS * nq, LANES), F32)],
    )
    return pl.pallas_call(
        kern,
        grid_spec=grid_spec,
        out_shape=jax.ShapeDtypeStruct((bs, nq, 512), F32),
        compiler_params=_params(("arbitrary", "arbitrary")),
        name="sb_sample",
    )(page_table.reshape(-1), q, bias_col, k_new_pad, v_new_pad, *([cache_k] * pp), *([cache_v] * pp))


def _merge_kernel(hm_ref, hg_ref, hs_ref, g0_ref, g1_ref, g2_ref, x_ref, gate_ref, wb_ref, wo_ref, o_ref):
    merged = jax.nn.sigmoid(g0_ref[...].astype(F32)) * _mm(hm_ref[...].astype(BF16), wb_ref[0])
    merged += jax.nn.sigmoid(g1_ref[...].astype(F32)) * _mm(hg_ref[...].astype(BF16), wb_ref[1])
    merged += jax.nn.sigmoid(g2_ref[...].astype(F32)) * _mm(hs_ref[...].astype(BF16), wb_ref[2])
    mix = _mm(merged.astype(BF16), wo_ref[...])
    o_ref[...] = x_ref[...] + gate_ref[...] * mix


def _merge(hm, hg, hs, main, x, mods, w_branch, w_out, layer, *, tm, rows_per_group):
    t, d = x.shape
    tiles_per_group = rows_per_group // tm
    tok = lambda blk: (lambda i: (i, blk))
    return pl.pallas_call(
        _merge_kernel,
        grid=(t // tm,),
        in_specs=[
            pl.BlockSpec((tm, 512), tok(0)),
            pl.BlockSpec((tm, 512), tok(0)),
            pl.BlockSpec((tm, 512), tok(0)),
            pl.BlockSpec((tm, d), tok(0)),
            pl.BlockSpec((tm, d), tok(1)),
            pl.BlockSpec((tm, d), tok(2)),
            pl.BlockSpec((tm, d), tok(0)),
            _mod_spec(mods, 2, layer, lambda i: i // tiles_per_group),
            pl.BlockSpec((None, N_BRANCH, BRANCH_W, d), lambda i: (layer, 0, 0, 0)),
            pl.BlockSpec((None, d, d), lambda i: (layer, 0, 0)),
        ],
        out_specs=pl.BlockSpec((tm, d), tok(0)),
        out_shape=jax.ShapeDtypeStruct((t, d), F32),
        compiler_params=_params(("arbitrary",)),
        name="branch_merge",
    )(hm, hg, hs, main, main, main, x, mods[0], w_branch, w_out)


FF_CHUNK = 256


def _ffn_kernel(x_ref, nw_ref, sc_ref, sh_ref, gate_ref, wu_ref, cw_ref, cb_ref, wd_ref, p1_ref, p2_ref, fw_ref,
                o_ref, u_ref, carry_ref, act_ref, *, TM, SEQ_IN_TILE, FINAL_NORM):
    x = x_ref[...]
    y = x * lax.rsqrt(jnp.mean(x * x, axis=-1, keepdims=True) + EPS)
    hb = (y * nw_ref[...] * (1.0 + sc_ref[...]) + sh_ref[...]).astype(BF16)
    row = lax.broadcasted_iota(jnp.int32, (TM, FF_CHUNK), 0)
    if SEQ_IN_TILE is None:
        @pl.when(pl.program_id(1) == 0)
        def _():
            carry_ref[...] = p1_ref[...]
        pos = row
    else:
        pos = row % SEQ_IN_TILE
    for c in range(D_FF // FF_CHUNK):
        cs = slice(c * FF_CHUNK, (c + 1) * FF_CHUNK)
        vs = slice(D_FF + c * FF_CHUNK, D_FF + (c + 1) * FF_CHUNK)
        u = _mm(hb, wu_ref[:, cs])
        val = _mm(hb, wu_ref[:, vs])
        if SEQ_IN_TILE is None:
            prev = carry_ref[:, cs]
            prev1 = prev[SUBLANES - 1:SUBLANES, :]
            prev2 = jnp.where(row == 0, prev[SUBLANES - 2:SUBLANES - 1, :], prev1)
            carry_ref[:, cs] = u[TM - SUBLANES:TM, :]
            u_ref[:, cs] = u[TM - SUBLANES:TM, :]
        else:
            prev1 = p1_ref[:, cs]
            prev2 = p2_ref[:, cs]
            u_ref[:, cs] = u
        u_m1 = jnp.where(pos >= 1, pltpu.roll(u, 1, axis=0), prev1)
        u_m2 = jnp.where(pos >= 2, pltpu.roll(u, 2, axis=0), prev2)
        cw = cw_ref[:, cs]
        conv = cb_ref[:, cs] + cw[0:1, :] * u_m2 + cw[1:2, :] * u_m1 + cw[2:3, :] * u
        act = 0.5 * conv * (1.0 + lax.erf(conv * (2.0 ** -0.5))) * val
        act_ref[:, cs] = act.astype(BF16)
    out = x + gate_ref[...] * _mm(act_ref[...], wd_ref[...])
    if FINAL_NORM:
        out = out * lax.rsqrt(jnp.mean(out * out, axis=-1, keepdims=True) + EPS) * fw_ref[...]
    o_ref[...] = out


def _ffn(x, nw, mods, w_up, conv_w, conv_b, w_down, p1, p2, final_w, layer, *, batch, seq, tm, seq_in_tile,
         final_norm):
    t, d = x.shape
    per_tile = seq_in_tile is not None
    if per_tile:
        grid = (1, t // tm)
        rowmap = lambda b, i: (i, 0)
        grp = lambda b, i: i
        pspec = pl.BlockSpec((None, tm, D_FF), lambda b, i: (layer, i, 0))
        uspec = pl.BlockSpec((tm, D_FF), rowmap)
        ushape = jax.ShapeDtypeStruct((t, D_FF), F32)
    else:
        nt = seq // tm
        grid = (batch, nt)
        rowmap = lambda b, i: (b * nt + i, 0)
        grp = lambda b, i: b
        pspec = pl.BlockSpec((None, None, SUBLANES, D_FF), lambda b, i: (layer, b, 0, 0))
        uspec = pl.BlockSpec((None, SUBLANES, D_FF), lambda b, i: (b, 0, 0))
        ushape = jax.ShapeDtypeStruct((batch, SUBLANES, D_FF), F32)
    const3 = lambda b, i: (layer, 0, 0)
    once = pl.Buffered(1)
    kern = functools.partial(_ffn_kernel, TM=tm, SEQ_IN_TILE=seq_in_tile, FINAL_NORM=final_norm)
    return pl.pallas_call(
        kern,
        grid=grid,
        in_specs=[
            pl.BlockSpec((tm, d), rowmap),
            pl.BlockSpec((None, 1, d), const3),
            _mod_spec(mods, 4, layer, grp),
            _mod_spec(mods, 3, layer, grp),
            _mod_spec(mods, 5, layer, grp),
            pl.BlockSpec((None, d, 2 * D_FF), const3, pipeline_mode=once),
            pl.BlockSpec((None, CONV_W, D_FF), const3),
            pl.BlockSpec((None, 1, D_FF), const3),
            pl.BlockSpec((None, D_FF, d), const3, pipeline_mode=once),
            pspec,
            pspec,
            pl.BlockSpec((1, d), lambda b, i: (0, 0)),
        ],
        out_specs=[pl.BlockSpec((tm, d), rowmap), uspec],
        out_shape=[jax.ShapeDtypeStruct((t, d), F32), ushape],
        scratch_shapes=[pltpu.VMEM((SUBLANES, D_FF), F32), pltpu.VMEM((tm, D_FF), BF16)],
        compiler_params=_params(("arbitrary", "arbitrary")),
        name="conv_ffn",
    )(x, nw, mods[0], mods[0], mods[0], w_up, conv_w, conv_b, w_down, p1, p2, final_w)


def _prep_weights(w_in, gla_w_gate, w_branch, w_out, w_up, w_down):
    widths = (256, 256, 512, 4, 4, 512, 256, 256, 512, GLA_RANK, 512, 512, 512, 512, N_BRANCH * D_MODEL)
    offs = np.concatenate([[0], np.cumsum(widths)])
    names = ("ml_q", "ml_k", "ml_v", "ml_i", "ml_f", "ml_o", "gla_q", "gla_k", "gla_v", "gla_lr", "gla_o",
             "sb_q", "sb_k", "sb_v", "gates")
    w_t = jnp.swapaxes(w_in, 1, 2)
    part = {nm: w_t[:, int(offs[i]):int(offs[i + 1]), :] for i, nm in enumerate(names)}
    order = ("gates", "ml_v", "ml_o", "gla_v", "gla_o", "sb_q", "ml_q", "ml_k", "gla_q", "gla_k")
    w_main = jnp.concatenate([part[nm] for nm in order], axis=1).astype(BF16)
    w_kt = part["sb_k"].astype(BF16)
    w_vt = part["sb_v"].astype(BF16)
    small = jnp.concatenate([part["ml_i"], part["ml_f"], part["gla_lr"]], axis=1)
    w_small = jnp.pad(small, ((0, 0), (0, SMALL_W - small.shape[1]), (0, 0))).astype(BF16)
    wg_pad = jnp.pad(gla_w_gate, ((0, 0), (2 * ML_HEADS, SMALL_W - 2 * ML_HEADS - GLA_RANK), (0, 0))).astype(BF16)
    return ((w_main, w_small, w_kt, w_vt), wg_pad, w_branch.astype(BF16), w_out.astype(BF16),
            w_up.astype(BF16), w_down.astype(BF16))


def _run_group(x, mod, states, paged, weights, params, *, tm, scan_len, sb_block):
    (w_proj, wg_pad, w_branch, w_out, w_up, w_down) = weights
    (norm1_w, norm2_w, ml_b_i, ml_b_f, ml_norm_w, gla_b_gate, gla_norm_w, sb_bias, conv_w, conv_b, final_w) = params
    ml_c0, ml_n0, ml_m0, gla_s0, conv0 = states
    batch, seq, d = x.shape
    depth = w_proj[0].shape[0]
    t = batch * seq
    per_seq = seq < tm
    xt = x.reshape(t, d)
    scan_dtype = F32 if per_seq else BF16
    kv_groups = 1 if per_seq else batch
    kbuf = jnp.zeros((depth, kv_groups, SB_HEADS * SB_DH, t // kv_groups), F32)
    vbuf = jnp.zeros((depth, kv_groups, SB_HEADS * SB_DH, t // kv_groups), F32)
    mcs, mns, mms, gss, bufs = [], [], [], [], []
    rows_per_group = tm if per_seq else seq
    tm_proj = tm
    tm_big = tm if per_seq else math.gcd(seq, 2 * tm)
    if per_seq:
        mods = (jnp.repeat(mod, seq, axis=1).transpose(2, 0, 1, 3).reshape(6 * depth * (t // tm), tm, d), depth)
    else:
        mods = (mod.transpose(2, 0, 1, 3).reshape(6 * depth * batch, 1, d), depth)
    gate_bias = jnp.concatenate([ml_b_i, ml_b_f], axis=1)
    b_row = jnp.pad(gate_bias, ((0, 0), (0, SMALL_W - 2 * ML_HEADS))).reshape(depth, 1, SMALL_W)
    b_col = jnp.pad(gate_bias, ((0, 0), (0, SMALL_T_ROWS - 2 * ML_HEADS))).reshape(depth, SMALL_T_ROWS, 1)
    ml_m0 = ml_m0.reshape(depth, batch, ML_HEADS, 1)
    gla_s0t = jnp.swapaxes(gla_s0, -1, -2)
    if paged is None:
        sb_bias_all = jnp.repeat(sb_bias, SB_DH, axis=1).reshape(depth, 1, SB_HEADS * SB_DH)
        conv_p1 = conv_p2 = jnp.pad(conv0, ((0, 0), (0, 0), (SUBLANES - (CONV_W - 1), 0), (0, 0)))
    else:
        sb_bias_all = jnp.repeat(sb_bias, seq, axis=1).reshape(depth, SB_HEADS * seq, 1)
        zero = jnp.zeros((depth, batch, seq - 1, D_FF), F32)
        conv_p1 = jnp.concatenate([conv0[:, :, 1:2], zero], axis=2).reshape(depth, t, D_FF)
        conv_p2 = jnp.concatenate([conv0, zero[:, :, 1:]], axis=2).reshape(depth, t, D_FF)
    for l in range(depth):
        last = l == depth - 1
        main, small, small_t, kbuf, vbuf = _norm_proj(xt, norm1_w, mods, w_proj, kbuf, vbuf, l,
                                                      tm=tm_proj, rows_per_group=rows_per_group,
                                                      main_dtype=scan_dtype)
        if per_seq:
            small_t3 = small_t.reshape(SMALL_T_ROWS, batch, seq).transpose(1, 0, 2)
        else:
            small_t3 = small_t.reshape(1, SMALL_T_ROWS, t)
        ml = _mlstm(main, small, small_t3, b_row, b_col, ml_norm_w, ml_c0, ml_n0, ml_m0, l,
                    batch=batch, seq=seq, L=scan_len, out_dtype=scan_dtype, per_seq=per_seq)
        gla = _gla(main, small, wg_pad, gla_b_gate, gla_norm_w, gla_s0t, l,
                   batch=batch, seq=seq, L=scan_len, out_dtype=scan_dtype)
        hm, mc, mn, mm, hg, gst = _scans(ml, gla, batch=batch, seq=seq, L=scan_len)
        if paged is None:
            hs = _sb_prompt(main, kbuf, vbuf, sb_bias_all, l, batch=batch, seq=seq, tq=sb_block[0], tk=sb_block[1])
        else:
            cache_kt, cache_vt, page_table = paged
            ps = cache_kt.shape[-1]
            q = main[:, BLK_SB_Q * 512:(BLK_SB_Q + 1) * 512].reshape(batch, seq, 512)
            new_t = lambda buf: jnp.pad(buf[l, 0].reshape(-1, batch, seq).transpose(1, 0, 2),
                                        ((0, 0), (0, 0), (0, ps - seq)))
            hs = _sb_sample(page_table, q, sb_bias_all, new_t(kbuf), new_t(vbuf), cache_kt, cache_vt, l,
                            pages_per_step=math.gcd(page_table.shape[1], 16)).reshape(t, 512)
        xt = _merge(hm, hg, hs, main, xt, mods, w_branch, w_out, l, tm=tm_big, rows_per_group=rows_per_group)
        if per_seq:
            xt, u = _ffn(xt, norm2_w, mods, w_up, conv_w, conv_b, w_down, conv_p1, conv_p2, final_w, l,
                         batch=batch, seq=seq, tm=tm, seq_in_tile=seq, final_norm=last)
            full = jnp.concatenate([conv0[l], u.reshape(batch, seq, D_FF)], axis=1)
            buf = full[:, seq:]
        else:
            xt, tail = _ffn(xt, norm2_w, mods, w_up, conv_w, conv_b, w_down, conv_p1, conv_p2, final_w, l,
                            batch=batch, seq=seq, tm=tm_big, seq_in_tile=None, final_norm=last)
            buf = tail[:, SUBLANES - (CONV_W - 1):]
        mcs.append(mc)
        mns.append(mn)
        mms.append(mm.reshape(batch, ML_HEADS))
        gss.append(jnp.swapaxes(gst, -1, -2))
        bufs.append(buf)
    st = jnp.stack

    def kv_out(buf):
        b6 = buf.reshape(depth, kv_groups, SB_HEADS, SB_DH, batch // kv_groups, seq)
        return b6.transpose(0, 1, 4, 5, 2, 3).reshape(depth, batch, seq, SB_HEADS, SB_DH)

    return xt, kv_out(kbuf), kv_out(vbuf), st(mcs), st(mns), st(mms), st(gss), st(bufs)


def kernel(x_prompt, x_sample, cache_k, cache_v, state_mlstm_c, state_mlstm_n, state_mlstm_m, state_gla, state_conv, page_table, c_prompt, c_sample, w_ada, b_ada, norm1_w, norm2_w, w_in, ml_b_i, ml_b_f, ml_norm_w, gla_w_gate, gla_b_gate, gla_norm_w, sb_bias, w_branch, w_out, w_up, conv_w, conv_b, w_down, final_norm_w):
    depth = w_in.shape[0]
    bp, sp, d = x_prompt.shape
    bs, ss, _ = x_sample.shape
    weights = _prep_weights(w_in, gla_w_gate, w_branch, w_out, w_up, w_down)
    params = (norm1_w.reshape(depth, 1, d), norm2_w.reshape(depth, 1, d), ml_b_i, ml_b_f,
              ml_norm_w.reshape(depth, 1, -1),
              gla_b_gate.reshape(depth, 1, -1), gla_norm_w.reshape(depth, 1, -1), sb_bias, conv_w,
              conv_b.reshape(depth, 1, -1), final_norm_w.reshape(1, d))
    mod = _modulation(jnp.concatenate([c_prompt, c_sample], axis=0), w_ada, b_ada)
    mod_p = mod[:, :bp].reshape(depth, bp, 6, d)
    mod_s = mod[:, bp:].reshape(depth, bs, 6, d)

    zeros = lambda *shape: jnp.zeros((depth, bp) + shape, F32)
    states_p = (zeros(ML_HEADS, ML_DK, ML_DV), zeros(ML_HEADS, ML_DK),
                jnp.full((depth, bp, ML_HEADS), NEG_BIG, F32), zeros(GLA_HEADS, GLA_DK, GLA_DV),
                zeros(CONV_W - 1, D_FF))
    tm_p = math.gcd(sp, 512)
    scan_p = math.gcd(sp, 512)
    out_p = _run_group(x_prompt, mod_p, states_p, None, weights, params, tm=tm_p, scan_len=scan_p,
                       sb_block=(math.gcd(sp, 1024), math.gcd(sp, 256)))

    states_s = (state_mlstm_c, state_mlstm_n, state_mlstm_m, state_gla, state_conv)
    cache_kt = jnp.transpose(cache_k, (0, 1, 3, 4, 2))
    cache_vt = jnp.transpose(cache_v, (0, 1, 3, 4, 2))
    out_s = _run_group(x_sample, mod_s, states_s, (cache_kt, cache_vt, page_table), weights, params,
                       tm=bs * ss, scan_len=ss, sb_block=None)

    y_p = out_p[0].reshape(bp, sp, d)
    y_s = out_s[0].reshape(bs, ss, d)
    (_, k_p, v_p, mc_p, mn_p, mm_p, g_p, cb_p) = out_p
    (_, k_s, v_s, mc_s, mn_s, mm_s, g_s, cb_s) = out_s
    return (y_p, y_s, k_p, v_p, k_s, v_s, mc_p, mn_p, mm_p, mc_s, mn_s, mm_s, g_p, g_s, cb_p, cb_s)
```
